```python
import jax
import jax.numpy as jnp
from jax import lax
import numpy as np

D_MODEL = 1024
BATCH = 8
SEQ = 4096
DEPTH = 2

GRID_W = 64
CTX_LEN = 256
EPS = 1e-6

LRU_WIDTH = 512
LRU_BLOCKS = 8
LRU_BLOCK = LRU_WIDTH // LRU_BLOCKS
CONV_W = 4
CONV_PAD_L = 2
LRU_C = 8.0

MLA_HEADS = 8
MLA_NOPE = 64
MLA_ROPE = 32
MLA_V = 64
Q_LORA = 384
KV_LORA = 256
ROPE_BASE = 10000.0
Q_BLOCK = 128

RET_HEADS = 4
RET_DK = 64
RET_DV = 128
RET_CHUNK = 128

D_FF = -(-8 * D_MODEL // (3 * 256)) * 256

KV_SIZES = (LRU_WIDTH, KV_LORA, MLA_ROPE, RET_HEADS * RET_DK, RET_HEADS * RET_DV)
Q_SIZES = (Q_LORA, RET_HEADS * RET_DK, LRU_WIDTH, RET_HEADS * RET_DV, 3 * D_MODEL)
N_KV = sum(KV_SIZES)
N_IN = N_KV + sum(Q_SIZES)

kernel_name = 'hybrid_rglru_mla_retention_dit'


def rmsnorm(x, g):
    xf = x.astype(jnp.float32)
    y = xf * lax.rsqrt(jnp.mean(xf * xf, axis=-1, keepdims=True) + EPS)
    return (y * g).astype(x.dtype)


def head_norm(o):
    of = o.astype(jnp.float32)
    mu = jnp.mean(of, axis=-1, keepdims=True)
    var = jnp.mean(jnp.square(of - mu), axis=-1, keepdims=True)
    return ((of - mu) * lax.rsqrt(var + EPS)).astype(o.dtype)


def modulate(h, shift, scale):
    return h * (1.0 + scale) + shift


def split_cols(z, sizes):
    parts, off = [], 0
    for s in sizes:
        parts.append(z[..., off:off + s])
        off += s
    return parts


def rotate(x, cos, sin):
    half = x.shape[-1] // 2
    x1, x2 = x[..., :half], x[..., half:]
    return jnp.concatenate([x1 * cos - x2 * sin, x1 * sin + x2 * cos], axis=-1)


def axial_rope_tables(n_tokens, dtype):
    rows = n_tokens // GRID_W
    row = jnp.repeat(jnp.arange(rows, dtype=jnp.float32), GRID_W)
    col = jnp.tile(jnp.arange(GRID_W, dtype=jnp.float32), rows)
    n_freq = MLA_ROPE // 4
    inv = jnp.power(ROPE_BASE, -jnp.arange(n_freq, dtype=jnp.float32) / n_freq)
    ang = jnp.concatenate([row[:, None] * inv, col[:, None] * inv], axis=-1)
    return jnp.cos(ang).astype(dtype), jnp.sin(ang).astype(dtype)


def retention_rope_tables(start, n, dtype):
    theta = 1.0 / jnp.power(10000.0, jnp.linspace(0.0, 1.0, RET_DK // 2, dtype=jnp.float32))
    pos = start + jnp.arange(n, dtype=jnp.float32)
    ang = pos[:, None] * theta
    return jnp.cos(ang)[:, None, :].astype(dtype), jnp.sin(ang)[:, None, :].astype(dtype)


def retention_log_decays():
    h = jnp.arange(RET_HEADS, dtype=jnp.float32)
    return (jnp.log1p(-jnp.exp2(-5.0 - h)), jnp.log1p(-jnp.exp2(-5.5 - h)))


def depthwise_conv(u, w, b):
    out = lax.conv_general_dilated(
        u, w[:, None, :], window_strides=(1,), padding=[(CONV_PAD_L, CONV_W - 1 - CONV_PAD_L)],
        dimension_numbers=('NWC', 'WIO', 'NWC'), feature_group_count=u.shape[-1])
    return out + b


def rglru_coefficients(u, wa, ba, wx, bx, lam):
    uf = u.astype(jnp.float32)
    ub = uf.reshape(*u.shape[:-1], LRU_BLOCKS, LRU_BLOCK)
    r = jax.nn.sigmoid(jnp.einsum('blnd,nde->blne', ub, wa).reshape(u.shape) + ba)
    i = jax.nn.sigmoid(jnp.einsum('blnd,nde->blne', ub, wx).reshape(u.shape) + bx)
    log_a = -LRU_C * r * jax.nn.softplus(-lam)
    a = jnp.exp(log_a)
    b = jnp.sqrt(-jnp.expm1(2.0 * log_a)) * (i * uf)
    return a, b


def _lin_combine(l, r):
    return (l[0] * r[0], r[0] * l[1] + r[1])


def linear_scan(a, b, h0, reverse):
    if reverse:
        a, b = a[:, ::-1], b[:, ::-1]
    b = b.at[:, 0].add(a[:, 0] * h0)
    _, h = lax.associative_scan(_lin_combine, (a, b), axis=1)
    return h[:, ::-1] if reverse else h


def attention(q, k, v):
    s = jnp.einsum('bqhd,bkhd->bhqk', q, k).astype(jnp.float32) * (q.shape[-1] ** -0.5)
    p = jax.nn.softmax(s, axis=-1).astype(v.dtype)
    return jnp.einsum('bhqk,bkhd->bqhd', p, v)


def blocked_attention(q, k, v):
    B, L, H, D = q.shape
    nb = L // Q_BLOCK
    qb = jnp.moveaxis(q.reshape(B, nb, Q_BLOCK, H, D), 1, 0)
    o = lax.map(lambda qi: attention(qi, k, v), qb)
    return jnp.moveaxis(o, 0, 1).reshape(B, L, H, v.shape[-1])


def retention_chunkwise(q, k, v, log_gamma, r0):
    B, L, H, DK = q.shape
    DV = v.shape[-1]
    n = L // RET_CHUNK
    dt = q.dtype
    pos = jnp.arange(RET_CHUNK, dtype=jnp.float32)
    diff = pos[:, None] - pos[None, :]
    inner = jnp.where(diff >= 0, jnp.exp(log_gamma[:, None, None] * jnp.maximum(diff, 0.0)), 0.0).astype(dt)
    q_decay = jnp.exp(log_gamma[:, None] * (pos + 1.0))[..., None].astype(dt)
    k_decay = jnp.exp(log_gamma[:, None] * (RET_CHUNK - 1.0 - pos))[..., None].astype(dt)
    chunk_decay = jnp.exp(log_gamma * RET_CHUNK)[:, None, None].astype(dt)

    def to_chunks(t):
        return t.reshape(B, n, RET_CHUNK, H, t.shape[-1]).transpose(1, 0, 3, 2, 4)

    def step(r, xs):
        qi, ki, vi = xs
        s = jnp.einsum('bhqd,bhkd->bhqk', qi, ki) * inner
        o = jnp.einsum('bhqk,bhkv->bhqv', s, vi) + jnp.einsum('bhqd,bhdv->bhqv', qi * q_decay, r)
        r = chunk_decay * r + jnp.einsum('bhkd,bhkv->bhdv', ki * k_decay, vi)
        return r, o

    r, o = lax.scan(step, r0.astype(dt), (to_chunks(q), to_chunks(k), to_chunks(v)))
    return o.transpose(1, 0, 3, 2, 4).reshape(B, L, H, DV), r


def retention_state(k, v, log_gamma):
    L = k.shape[1]
    w = jnp.exp((L - 1.0 - jnp.arange(L, dtype=jnp.float32))[:, None] * log_gamma[None, :]).astype(k.dtype)
    return jnp.einsum('blhd,blhv->bhdv', k * w[None, :, :, None], v)


def swiglu(h, w1, w3, w2):
    return (jax.nn.silu(h @ w1) * (h @ w3)) @ w2


def token_mixer(h, hc, w_in, conv_w, conv_b, lru_wa, lru_ba, lru_wx, lru_bx, lru_lam,
                g_q, w_uq, g_kv, w_ukv, w_oa, w_ob, w_oc, w_out,
                rope, ret_lat, ret_ctx, log_decays, need_ctx_out):
    B, L, _ = h.shape
    Lc = hc.shape[1]
    dt = h.dtype
    z = h @ w_in
    zc = hc @ (w_in if need_ctx_out else w_in[:, :N_KV])
    src = split_cols(z[..., :N_KV], KV_SIZES)
    qry = split_cols(z[..., N_KV:], Q_SIZES)
    csrc = split_cols(zc[..., :N_KV], KV_SIZES)
    cqry = split_cols(zc[..., N_KV:], Q_SIZES) if need_ctx_out else None
    ident = lambda t: t
    flip = lambda t: t[:, ::-1]

    u = depthwise_conv(src[0], conv_w, conv_b)
    uc = depthwise_conv(csrc[0], conv_w, conv_b)
    rec_lat, rec_ctx = [], []
    for d in range(2):
        rev = d == 1
        p = (lru_wa[d], lru_ba[d], lru_wx[d], lru_bx[d], lru_lam[d])
        ac, bc = rglru_coefficients(uc, *p)
        hcd = linear_scan(ac, bc, jnp.zeros_like(bc[:, 0]), rev)
        a, b = rglru_coefficients(u, *p)
        rec_lat.append(linear_scan(a, b, hcd[:, 0] if rev else hcd[:, -1], rev))
        rec_ctx.append(hcd)
    y_a = (jax.nn.gelu(qry[2]) * (rec_lat[0] + rec_lat[1])).astype(dt)

    cos, sin = rope
    rot_k = lambda t: rotate(t, cos, sin)
    rot_q = lambda t: rotate(t, cos[:, None, :], sin[:, None, :])

    def mla_keys(kvd, kr, rot):
        n = kvd.shape[1]
        kv = (rmsnorm(kvd, g_kv) @ w_ukv).reshape(B, n, MLA_HEADS, MLA_NOPE + MLA_V)
        kr = jnp.broadcast_to(rot(kr)[:, :, None, :], (B, n, MLA_HEADS, MLA_ROPE))
        return jnp.concatenate([kv[..., :MLA_NOPE], kr], axis=-1), kv[..., MLA_NOPE:]

    def mla_queries(qd, rot):
        n = qd.shape[1]
        q = (rmsnorm(qd, g_q) @ w_uq).reshape(B, n, MLA_HEADS, MLA_NOPE + MLA_ROPE)
        return jnp.concatenate([q[..., :MLA_NOPE], rot(q[..., MLA_NOPE:])], axis=-1)

    k_lat, v_lat = mla_keys(src[1], src[2], rot_k)
    k_ctx, v_ctx = mla_keys(csrc[1], csrc[2], ident)
    y_b = blocked_attention(mla_queries(qry[0], rot_q),
                            jnp.concatenate([k_ctx, k_lat], axis=1),
                            jnp.concatenate([v_ctx, v_lat], axis=1)).reshape(B, L, MLA_HEADS * MLA_V)

    rl_cos, rl_sin = ret_lat
    rc_cos, rc_sin = ret_ctx
    kscale = RET_DK ** -0.5
    heads = lambda t, dim: t.reshape(*t.shape[:2], RET_HEADS, dim)
    rq = rotate(heads(qry[1], RET_DK), rl_cos, rl_sin)
    rk = rotate(heads(src[3], RET_DK), rl_cos, rl_sin) * kscale
    rv = heads(src[4], RET_DV)
    crk = rotate(heads(csrc[3], RET_DK), rc_cos, rc_sin) * kscale
    crv = heads(csrc[4], RET_DV)
    crq = rotate(heads(cqry[1], RET_DK), rc_cos, rc_sin) if need_ctx_out else None
    ret_out, ret_out_c = [], []
    for d in range(2):
        f = flip if d == 1 else ident
        lg = log_decays[d]
        if need_ctx_out:
            r0 = jnp.zeros((B, RET_HEADS, RET_DK, RET_DV), crk.dtype)
            occ, r_ctx = retention_chunkwise(f(crq), f(crk), f(crv), lg, r0)
            ret_out_c.append(f(occ))
        else:
            r_ctx = retention_state(f(crk), f(crv), lg)
        ol, _ = retention_chunkwise(f(rq), f(rk), f(rv), lg, r_ctx)
        ret_out.append(f(ol))
    y_c = jax.nn.silu(qry[3]) * head_norm(ret_out[0] + ret_out[1]).reshape(B, L, RET_HEADS * RET_DV)

    def merge(ya, yb, yc, gm):
        ga, gb, gc = jnp.split(gm, 3, axis=-1)
        m = (jax.nn.sigmoid(ga) * (ya @ w_oa) + jax.nn.sigmoid(gb) * (yb @ w_ob)
             + jax.nn.sigmoid(gc) * (yc @ w_oc))
        return m @ w_out

    y = merge(y_a, y_b, y_c, qry[4])
    if not need_ctx_out:
        return y, None
    y_ac = (jax.nn.gelu(cqry[2]) * (rec_ctx[0] + rec_ctx[1])).astype(dt)
    y_bc = attention(mla_queries(cqry[0], ident), k_ctx, v_ctx).reshape(B, Lc, MLA_HEADS * MLA_V)
    y_cc = jax.nn.silu(cqry[3]) * head_norm(ret_out_c[0] + ret_out_c[1]).reshape(B, Lc, RET_HEADS * RET_DV)
    return y, merge(y_ac, y_bc, y_cc, cqry[4])


def setup_inputs(seed: int = 0) -> dict:
    key = jax.random.key(seed)
    ks = iter(jax.random.split(key, 32))
    f32 = jnp.float32
    D = D_MODEL

    def nrm(shape, scale):
        return jax.random.normal(next(ks), shape, f32) * scale

    u = jax.random.uniform(next(ks), (DEPTH, 2, LRU_WIDTH), f32, 0.9, 0.999)
    a = u ** (1.0 / LRU_C)
    lam = jnp.log(a) - jnp.log1p(-a)
    return {
        'x': nrm((BATCH, SEQ, D), 1.0),
        'c': nrm((BATCH, D), 1.0),
        'ctx': nrm((BATCH, CTX_LEN, D), 1.0),
        'c_ctx': nrm((D,), 1.0),
        'w_mod': nrm((DEPTH, D, 6 * D), 0.5 * D ** -0.5),
        'b_mod': nrm((DEPTH, 6 * D), 0.02),
        'g_mix': 1.0 + nrm((DEPTH, D), 0.02),
        'g_ffn': 1.0 + nrm((DEPTH, D), 0.02),
        'w_in': nrm((DEPTH, D, N_IN), D ** -0.5),
        'conv_w': nrm((DEPTH, CONV_W, LRU_WIDTH), CONV_W ** -0.5),
        'conv_b': nrm((DEPTH, LRU_WIDTH), 0.02),
        'lru_wa': nrm((DEPTH, 2, LRU_BLOCKS, LRU_BLOCK, LRU_BLOCK), LRU_BLOCK ** -0.5),
        'lru_ba': nrm((DEPTH, 2, LRU_WIDTH), 0.02),
        'lru_wx': nrm((DEPTH, 2, LRU_BLOCKS, LRU_BLOCK, LRU_BLOCK), LRU_BLOCK ** -0.5),
        'lru_bx': nrm((DEPTH, 2, LRU_WIDTH), 0.02),
        'lru_lam': lam,
        'g_q': 1.0 + nrm((DEPTH, Q_LORA), 0.02),
        'w_uq': nrm((DEPTH, Q_LORA, MLA_HEADS * (MLA_NOPE + MLA_ROPE)), Q_LORA ** -0.5),
        'g_kv': 1.0 + nrm((DEPTH, KV_LORA), 0.02),
        'w_ukv': nrm((DEPTH, KV_LORA, MLA_HEADS * (MLA_NOPE + MLA_V)), KV_LORA ** -0.5),
        'w_oa': nrm((DEPTH, LRU_WIDTH, D), LRU_WIDTH ** -0.5),
        'w_ob': nrm((DEPTH, MLA_HEADS * MLA_V, D), (MLA_HEADS * MLA_V) ** -0.5),
        'w_oc': nrm((DEPTH, RET_HEADS * RET_DV, D), (RET_HEADS * RET_DV) ** -0.5),
        'w_out': nrm((DEPTH, D, D), D ** -0.5),
        'w_ff1': nrm((DEPTH, D, D_FF), D ** -0.5),
        'w_ff3': nrm((DEPTH, D, D_FF), D ** -0.5),
        'w_ff2': nrm((DEPTH, D_FF, D), D_FF ** -0.5),
        'g_final': 1.0 + nrm((D,), 0.02),
    }


def reference(x, c, ctx, c_ctx, w_mod, b_mod, g_mix, g_ffn, w_in, conv_w, conv_b,
              lru_wa, lru_ba, lru_wx, lru_bx, lru_lam, g_q, w_uq, g_kv, w_ukv,
              w_oa, w_ob, w_oc, w_out, w_ff1, w_ff3, w_ff2, g_final):
    B, L, D = x.shape
    Lc = ctx.shape[1]
    dt = x.dtype
    rope = axial_rope_tables(L, dt)
    ret_lat = retention_rope_tables(Lc, L, dt)
    ret_ctx = retention_rope_tables(0, Lc, dt)
    log_decays = retention_log_decays()
    c_act = jax.nn.silu(c)
    cc_act = jax.nn.silu(c_ctx)
    xc = ctx
    for l in range(DEPTH):
        last = l == DEPTH - 1
        mod = c_act @ w_mod[l] + b_mod[l]
        sh_a, sc_a, ga_a, sh_f, sc_f, ga_f = [m[:, None, :] for m in jnp.split(mod, 6, axis=-1)]
        n_cm = 2 * D if last else 6 * D
        mc = jnp.split(cc_act @ w_mod[l][:, :n_cm] + b_mod[l][:n_cm], n_cm // D)
        h = modulate(rmsnorm(x, g_mix[l]), sh_a, sc_a)
        hc = modulate(rmsnorm(xc, g_mix[l]), mc[0], mc[1])
        y, yc = token_mixer(h, hc, w_in[l], conv_w[l], conv_b[l], lru_wa[l], lru_ba[l], lru_wx[l],
                            lru_bx[l], lru_lam[l], g_q[l], w_uq[l], g_kv[l], w_ukv[l],
                            w_oa[l], w_ob[l], w_oc[l], w_out[l],
                            rope, ret_lat, ret_ctx, log_decays, not last)
        x = x + ga_a * y
        x = x + ga_f * swiglu(modulate(rmsnorm(x, g_ffn[l]), sh_f, sc_f), w_ff1[l], w_ff3[l], w_ff2[l])
        if not last:
            xc = xc + mc[2] * yc
            xc = xc + mc[5] * swiglu(modulate(rmsnorm(xc, g_ffn[l]), mc[3], mc[4]),
                                     w_ff1[l], w_ff3[l], w_ff2[l])
    return rmsnorm(x, g_final)
```

```python
import functools
import math

import jax
import jax.numpy as jnp
from jax import lax
from jax.experimental import pallas as pl
from jax.experimental.pallas import tpu as pltpu

F32 = jnp.float32
BF16 = jnp.bfloat16

D_MODEL = 1024
EPS = 1e-6
GRID_W = 64
LRU_W = 512
LRU_BLOCKS = 8
LRU_BLOCK = LRU_W // LRU_BLOCKS
CONV_W = 4
CONV_PAD_L = 2
LRU_C = 8.0
MLA_HEADS = 8
MLA_NOPE = 64
MLA_ROPE = 32
MLA_V = 64
Q_LORA = 384
KV_LORA = 256
ROPE_BASE = 10000.0
RET_HEADS = 4
RET_DK = 64
RET_DV = 128
RET_CHUNK = 128
D_FF = 2816

LANES = 128
SUBLANES = 8
VMEM_LIMIT = 56 * 1024 * 1024

KV_GROUPS = (LRU_W, KV_LORA, LANES, LANES, RET_HEADS * RET_DK, RET_HEADS * RET_DV)
Q_GROUPS = (Q_LORA, RET_HEADS * RET_DK, LRU_W, RET_HEADS * RET_DV, 3 * D_MODEL)
N_KV_ARR = sum(KV_GROUPS)


def _params(sem):
    return pltpu.CompilerParams(dimension_semantics=sem, vmem_limit_bytes=VMEM_LIMIT)


def _resident(shape):
    nd = len(shape)
    return pl.BlockSpec(shape, lambda *_: (0,) * nd, pipeline_mode=pl.Buffered(1))


def _rms(x, g):
    y = x * lax.rsqrt(jnp.mean(x * x, axis=-1, keepdims=True) + EPS)
    return y * g


def _mod_body(c_ref, w_ref, b_ref, o_ref):
    cv = c_ref[...]
    act = cv * jax.nn.sigmoid(cv)
    o_ref[0] = jnp.dot(act, w_ref[0], precision=lax.Precision.HIGHEST,
                       preferred_element_type=F32) + b_ref[0]


def _modulation(cvec, w_mod, b_mod):
    depth, d, n = w_mod.shape
    rows = cvec.shape[0]
    tn = 1536
    return pl.pallas_call(
        _mod_body,
        grid=(depth, n // tn),
        in_specs=[
            pl.BlockSpec((rows, d), lambda l, j: (0, 0)),
            pl.BlockSpec((1, d, tn), lambda l, j: (l, 0, j)),
            pl.BlockSpec((1, 1, tn), lambda l, j: (l, 0, j)),
        ],
        out_specs=pl.BlockSpec((1, rows, tn), lambda l, j: (l, 0, j)),
        out_shape=jax.ShapeDtypeStruct((depth, rows, n), F32),
        compiler_params=_params(("arbitrary", "arbitrary")),
    )(cvec, w_mod, b_mod.reshape(depth, 1, n))


def _inproj_body(x_ref, g_ref, sh_ref, sc_ref, w_ref, *out_refs, widths):
    x = x_ref[0]
    h = _rms(x, g_ref[...]) * (1.0 + sc_ref[0]) + sh_ref[0]
    hb = h.astype(BF16)
    off = 0
    for o_ref, wd in zip(out_refs, widths):
        o_ref[0] = jnp.dot(hb, w_ref[:, off:off + wd],
                           preferred_element_type=F32).astype(o_ref.dtype)
        off += wd


def _inproj(x, g, shift, scale, w, widths, tm):
    b, l, d = x.shape
    n = w.shape[1]
    assert n == sum(widths)
    mod_map = (lambda bi, i: (bi, 0, 0)) if shift.shape[0] > 1 else (lambda bi, i: (0, 0, 0))
    return pl.pallas_call(
        functools.partial(_inproj_body, widths=widths),
        grid=(b, l // tm),
        in_specs=[
            pl.BlockSpec((1, tm, d), lambda bi, i: (bi, i, 0)),
            _resident((1, d)),
            pl.BlockSpec((1, 1, d), mod_map),
            pl.BlockSpec((1, 1, d), mod_map),
            _resident((d, n)),
        ],
        out_specs=[pl.BlockSpec((1, tm, wd), lambda bi, i: (bi, i, 0)) for wd in widths],
        out_shape=[jax.ShapeDtypeStruct((b, l, wd), BF16) for wd in widths],
        compiler_params=_params(("parallel", "parallel")),
    )(x, g.reshape(1, d), shift, scale, w)


LRU_TC = 256
LRU_HALO = SUBLANES


def _lru_body(u_ref, uc_ref, cw_ref, cb_ref, wg_ref, bg_ref, lam_ref, *rest, need_ctx):
    if need_ctx:
        out_ref, outc_ref, upad, upadc, acc, accc, hbuf = rest
    else:
        out_ref, upad, upadc, acc, accc, hbuf = rest
        outc_ref = None
    l = u_ref.shape[1]
    lc = uc_ref.shape[1]
    w = u_ref.shape[2]
    tc = LRU_TC
    nt = tc // SUBLANES

    zeros_halo = jnp.zeros((LRU_HALO, w), F32)
    for pad_ref, src_ref, n in ((upad, u_ref, l), (upadc, uc_ref, lc)):
        pad_ref[0:LRU_HALO, :] = zeros_halo
        pad_ref[LRU_HALO + n:2 * LRU_HALO + n, :] = zeros_halo

        def copy(i, carry, pad_ref=pad_ref, src_ref=src_ref):
            t0 = pl.multiple_of(i * tc, tc)
            pad_ref[pl.ds(LRU_HALO + t0, tc), :] = src_ref[0, pl.ds(t0, tc), :].astype(F32)
            return carry

        lax.fori_loop(0, n // tc, copy, 0)

    row = lax.broadcasted_iota(jnp.int32, (tc, w), 0) % SUBLANES

    def chunk_scan(pad_ref, t0, d, h):
        win = pad_ref[pl.ds(t0, tc + 2 * LRU_HALO), :]
        u = cb_ref[...]
        for k in range(CONV_W):
            sh = (CONV_PAD_L - k) % (tc + 2 * LRU_HALO)
            tap = win if sh == 0 else pltpu.roll(win, sh, 0)
            u = u + cw_ref[k:k + 1, :] * tap[LRU_HALO:LRU_HALO + tc]
        g = jnp.dot(u.astype(BF16), wg_ref[d], preferred_element_type=F32) + bg_ref[d]
        r = jax.nn.sigmoid(g[:, :w])
        gi = jax.nn.sigmoid(g[:, w:])
        z = -lam_ref[d]
        sp = jnp.maximum(z, 0.0) + jnp.log1p(jnp.exp(-jnp.abs(z)))
        log_a = -LRU_C * r * sp
        a = jnp.exp(log_a)
        th = jnp.tanh(log_a)
        bb = jnp.sqrt(-2.0 * th / (1.0 - th)) * (gi * u)
        s = 1
        while s < SUBLANES:
            if d == 0:
                keep = row >= s
                shift = s
            else:
                keep = row < SUBLANES - s
                shift = tc - s
            a_s = jnp.where(keep, pltpu.roll(a, shift, 0), 1.0)
            b_s = jnp.where(keep, pltpu.roll(bb, shift, 0), 0.0)
            bb = a * b_s + bb
            a = a * a_s
            s *= 2
        order = range(nt) if d == 0 else range(nt - 1, -1, -1)
        for k in order:
            sl = slice(k * SUBLANES, (k + 1) * SUBLANES)
            hk = a[sl] * h + bb[sl]
            hbuf[sl, :] = hk
            h = hk[SUBLANES - 1:SUBLANES] if d == 0 else hk[0:1]
        return h

    def run(pad_ref, n, d, h, write):
        nchunks = n // tc

        def body(i, h):
            ci = i if d == 0 else nchunks - 1 - i
            t0 = pl.multiple_of(ci * tc, tc)
            h = chunk_scan(pad_ref, t0, d, h)
            write(t0)
            return h

        return lax.fori_loop(0, nchunks, body, h)

    def write_accc(t0):
        accc[pl.ds(t0, tc), :] = hbuf[...]

    def write_acc(t0):
        acc[pl.ds(t0, tc), :] = hbuf[...]

    def write_outc(t0):
        if outc_ref is not None:
            outc_ref[0, pl.ds(t0, tc), :] = (accc[pl.ds(t0, tc), :] + hbuf[...]).astype(outc_ref.dtype)

    def write_out(t0):
        out_ref[0, pl.ds(t0, tc), :] = (acc[pl.ds(t0, tc), :] + hbuf[...]).astype(out_ref.dtype)

    h0 = jnp.zeros((1, w), F32)
    h = run(upadc, lc, 0, h0, write_accc)
    run(upad, l, 0, h, write_acc)
    h = run(upadc, lc, 1, h0, write_outc)
    run(upad, l, 1, h, write_out)


def _lru(u, uc, conv_w, conv_b, wg, bg, lam, need_ctx):
    b, l, w = u.shape
    lc = uc.shape[1]
    out_shape = [jax.ShapeDtypeStruct((b, l, w), BF16)]
    out_specs = [pl.BlockSpec((1, l, w), lambda bi: (bi, 0, 0))]
    if need_ctx:
        out_shape.append(jax.ShapeDtypeStruct((b, lc, w), BF16))
        out_specs.append(pl.BlockSpec((1, lc, w), lambda bi: (bi, 0, 0)))
    res = pl.pallas_call(
        functools.partial(_lru_body, need_ctx=need_ctx),
        grid=(b,),
        in_specs=[
            pl.BlockSpec((1, l, w), lambda bi: (bi, 0, 0)),
            pl.BlockSpec((1, lc, w), lambda bi: (bi, 0, 0)),
            _resident((CONV_W, w)),
            _resident((1, w)),
            _resident((2, w, 2 * w)),
            _resident((2, 1, 2 * w)),
            _resident((2, 1, w)),
        ],
        out_specs=out_specs,
        out_shape=out_shape,
        scratch_shapes=[
            pltpu.VMEM((l + 2 * LRU_HALO, w), F32),
            pltpu.VMEM((lc + 2 * LRU_HALO, w), F32),
            pltpu.VMEM((l, w), F32),
            pltpu.VMEM((lc, w), F32),
            pltpu.VMEM((LRU_TC, w), F32),
        ],
        compiler_params=_params(("parallel",)),
    )(u, uc, conv_w, conv_b.reshape(1, w), wg, bg, lam.reshape(2, 1, w))
    return (res[0], res[1]) if need_ctx else (res[0], None)


def _mla_prep_body(*refs, rotate, need_q):
    it = iter(refs)
    cq_ref = next(it) if need_q else None
    ckv_ref, ra_ref = next(it), next(it)
    rb_ref = next(it) if rotate else None
    if need_q:
        cosq_ref = next(it)
        sinq_ref = next(it) if rotate else None
    cosk_ref = next(it)
    sink_ref = next(it) if rotate else None
    if need_q:
        gq_ref, wq_ref = next(it), next(it)
        wqs_ref = next(it) if rotate else None
    gkv_ref, wkv_ref = next(it), next(it)
    q_out = next(it) if need_q else None
    k_out, v_out = next(it), next(it)
    nh = MLA_HEADS

    if need_q:
        cqn = _rms(cq_ref[0].astype(F32), gq_ref[...]).astype(BF16)
        qm = jnp.dot(cqn, wq_ref[...], preferred_element_type=F32)
        cosq = cosq_ref[...]
        if rotate:
            qs = jnp.dot(cqn, wqs_ref[...], preferred_element_type=F32)
            sinq = sinq_ref[...]
        for h in range(nh):
            sl = slice(h * LANES, (h + 1) * LANES)
            qh = qm[:, sl] * cosq
            if rotate:
                qh = qh + qs[:, sl] * sinq
            q_out[0, h] = qh.astype(q_out.dtype)

    kvn = _rms(ckv_ref[0].astype(F32), gkv_ref[...]).astype(BF16)
    kv = jnp.dot(kvn, wkv_ref[...], preferred_element_type=F32)
    rope = ra_ref[0].astype(F32) * cosk_ref[...]
    if rotate:
        rope = rope + rb_ref[0].astype(F32) * sink_ref[...]
    for h in range(nh):
        k_out[0, h] = (kv[:, h * LANES:(h + 1) * LANES] + rope).astype(k_out.dtype)
        v_out[0, h] = kv[:, (nh + h) * LANES:(nh + h + 1) * LANES].astype(v_out.dtype)


def _mla_prep(cq, ckv, ra, rb, cosq, sinq, cosk, sink, gq, wq, wqs, gkv, wkv, rotate, need_q, tm):
    b, l, _ = ckv.shape
    nh = MLA_HEADS
    row = lambda wd: pl.BlockSpec((1, tm, wd), lambda bi, i: (bi, i, 0))
    tab = pl.BlockSpec((tm, LANES), lambda bi, i: (i, 0))
    args, specs = [], []

    def add(a, s):
        args.append(a)
        specs.append(s)

    if need_q:
        add(cq, row(Q_LORA))
    add(ckv, row(KV_LORA))
    add(ra, row(LANES))
    if rotate:
        add(rb, row(LANES))
    if need_q:
        add(cosq, tab)
        if rotate:
            add(sinq, tab)
    add(cosk, tab)
    if rotate:
        add(sink, tab)
    if need_q:
        add(gq.reshape(1, Q_LORA), _resident((1, Q_LORA)))
        add(wq, _resident(wq.shape))
        if rotate:
            add(wqs, _resident(wqs.shape))
    add(gkv.reshape(1, KV_LORA), _resident((1, KV_LORA)))
    add(wkv, _resident(wkv.shape))

    head = pl.BlockSpec((1, nh, tm, LANES), lambda bi, i: (bi, 0, i, 0))
    hshape = jax.ShapeDtypeStruct((b, nh, l, LANES), BF16)
    n_out = 3 if need_q else 2
    res = pl.pallas_call(
        functools.partial(_mla_prep_body, rotate=rotate, need_q=need_q),
        grid=(b, l // tm),
        in_specs=specs,
        out_specs=[head] * n_out,
        out_shape=[hshape] * n_out,
        compiler_params=_params(("parallel", "parallel")),
    )(*args)
    return res if need_q else (None, res[0], res[1])


def _attn_body(q_ref, *refs, n_kv):
    kv_refs = refs[:2 * n_kv]
    o_ref = refs[2 * n_kv]
    acc = None
    for e in range(2):
        q = q_ref[0, e]
        ss = [lax.dot_general(q, kv_refs[2 * j][0, e], (((1,), (1,)), ((), ())),
                              preferred_element_type=F32) for j in range(n_kv)]
        m = ss[0].max(axis=-1, keepdims=True)
        for s in ss[1:]:
            m = jnp.maximum(m, s.max(axis=-1, keepdims=True))
        den = None
        o = None
        for j, s in enumerate(ss):
            p = jnp.exp(s - m)
            dj = p.sum(axis=-1, keepdims=True)
            oj = jnp.dot(p.astype(BF16), kv_refs[2 * j + 1][0, e], preferred_element_type=F32)
            den = dj if den is None else den + dj
            o = oj if o is None else o + oj
        o = o * (1.0 / den)
        acc = o if acc is None else acc + o
    o_ref[0] = acc.astype(o_ref.dtype)


def _attention(q, kvs, tq):
    b, nh, lq, _ = q.shape
    specs = [pl.BlockSpec((1, 2, tq, LANES), lambda bi, j, i: (bi, j, i, 0))]
    args = [q]
    for k, v in kvs:
        lk = k.shape[2]
        for a in (k, v):
            specs.append(pl.BlockSpec((1, 2, lk, LANES), lambda bi, j, i: (bi, j, 0, 0)))
            args.append(a)
    return pl.pallas_call(
        functools.partial(_attn_body, n_kv=len(kvs)),
        grid=(b, nh // 2, lq // tq),
        in_specs=specs,
        out_specs=pl.BlockSpec((1, tq, LANES), lambda bi, j, i: (bi, i, j)),
        out_shape=jax.ShapeDtypeStruct((b, lq, nh * MLA_V), BF16),
        compiler_params=_params(("parallel", "parallel", "arbitrary")),
    )(*args)


def _ret_body(*refs, need_ctx):
    it = iter(refs)
    q_ref, k_ref, v_ref = next(it), next(it), next(it)
    qc_ref = next(it) if need_ctx else None
    kc_ref, vc_ref = next(it), next(it)
    cq_ref, sq_ref, ck_ref, sk_ref = next(it), next(it), next(it), next(it)
    ccq_ref, scq_ref, cck_ref, sck_ref = next(it), next(it), next(it), next(it)
    inner_ref, qdec_ref, kdec_ref, cd_ref = next(it), next(it), next(it), next(it)
    o_ref = next(it)
    oc_ref = next(it) if need_ctx else None
    qr, kr, qcr, kcr, ofw, ofwc = (next(it) for _ in range(6))

    c = RET_CHUNK
    l = q_ref.shape[1]
    lc = kc_ref.shape[1]
    lane = lax.broadcasted_iota(jnp.int32, (c, LANES), 1)
    first_half = (lane % RET_DK) < (RET_DK // 2)

    def rotate_into(dst, src_ref, cos_ref, sin_ref, n):
        def body(i, carry):
            t0 = pl.multiple_of(i * c, c)
            x = src_ref[0, pl.ds(t0, c), :].astype(F32)
            sw = jnp.where(first_half, pltpu.roll(x, LANES - RET_DK // 2, 1),
                           pltpu.roll(x, RET_DK // 2, 1))
            y = x * cos_ref[pl.ds(t0, c), :] + sw * sin_ref[pl.ds(t0, c), :]
            dst[pl.ds(t0, c), :] = y
            return carry

        lax.fori_loop(0, n // c, body, 0)

    rotate_into(qr, q_ref, cq_ref, sq_ref, l)
    rotate_into(kr, k_ref, ck_ref, sk_ref, l)
    rotate_into(kcr, kc_ref, cck_ref, sck_ref, lc)
    if need_ctx:
        rotate_into(qcr, qc_ref, ccq_ref, scq_ref, lc)

    def chunk(qs, ks, vs_ref, t0, d, states, with_out):
        kf = ks[pl.ds(t0, c), :]
        kb = kf.astype(BF16)
        if with_out:
            qf = qs[pl.ds(t0, c), :]
        outs = []
        new_states = []
        for e in range(2):
            v = vs_ref[0, pl.ds(t0, c), e * RET_DV:(e + 1) * RET_DV]
            r = states[e]
            if with_out:
                in_head = (lane // RET_DK) == e
                qm = jnp.where(in_head, qf, 0.0)
                s = lax.dot_general(qm.astype(BF16), kb, (((1,), (1,)), ((), ())),
                                    preferred_element_type=F32)
                s = s * inner_ref[0, e, d]
                o = jnp.dot(s.astype(BF16), v, preferred_element_type=F32)
                o = o + jnp.dot((qf * qdec_ref[0, e, d]).astype(BF16), r.astype(BF16),
                                preferred_element_type=F32)
                outs.append(o)
            kd = (kf * kdec_ref[0, e, d]).astype(BF16)
            upd = lax.dot_general(kd, v, (((0,), (0,)), ((), ())), preferred_element_type=F32)
            new_states.append(cd_ref[0, e, d] * r + upd)
        return tuple(new_states), (jnp.concatenate(outs, axis=1) if with_out else None)

    def head_norm_store(dst_ref, t0, tot):
        parts = []
        for e in range(2):
            x = tot[:, e * RET_DV:(e + 1) * RET_DV]
            mu = jnp.mean(x, axis=-1, keepdims=True)
            xc = x - mu
            var = jnp.mean(xc * xc, axis=-1, keepdims=True)
            parts.append(xc * lax.rsqrt(var + EPS))
        dst_ref[0, pl.ds(t0, c), :] = jnp.concatenate(parts, axis=1).astype(dst_ref.dtype)

    def run(qs, ks, vs_ref, n, d, states, with_out, fw_ref, dst_ref):
        nchunks = n // c

        def body(i, states):
            ci = i if d == 0 else nchunks - 1 - i
            t0 = pl.multiple_of(ci * c, c)
            states, o = chunk(qs, ks, vs_ref, t0, d, states, with_out)
            if with_out:
                if d == 0:
                    fw_ref[pl.ds(t0, c), :] = o
                else:
                    head_norm_store(dst_ref, t0, fw_ref[pl.ds(t0, c), :] + o)
            return states

        return lax.fori_loop(0, nchunks, body, states)

    zero = (jnp.zeros((LANES, RET_DV), F32), jnp.zeros((LANES, RET_DV), F32))
    for d in range(2):
        st = run(qcr, kcr, vc_ref, lc, d, zero, need_ctx, ofwc, oc_ref)
        run(qr, kr, v_ref, l, d, st, True, ofw, o_ref)


def _retention(q, k, v, qc, kc, vc, tabs, ctabs, dec, need_ctx):
    b, l, _ = q.shape
    lc = kc.shape[1]
    npair = RET_HEADS // 2
    seq = lambda n, wd: pl.BlockSpec((1, n, wd), lambda bi, p: (bi, 0, p))
    args = [q, k, v]
    specs = [seq(l, LANES), seq(l, LANES), seq(l, 2 * RET_DV)]
    if need_ctx:
        args.append(qc)
        specs.append(seq(lc, LANES))
    args += [kc, vc]
    specs += [seq(lc, LANES), seq(lc, 2 * RET_DV)]
    for t in tabs:
        args.append(t)
        specs.append(_resident((l, LANES)))
    for t in ctabs:
        args.append(t)
        specs.append(_resident((lc, LANES)))
    inner, qdec, kdec, cd = dec
    c = RET_CHUNK
    args += [inner, qdec, kdec, cd]
    specs += [
        pl.BlockSpec((1, 2, 2, c, c), lambda bi, p: (p, 0, 0, 0, 0)),
        pl.BlockSpec((1, 2, 2, c, LANES), lambda bi, p: (p, 0, 0, 0, 0)),
        pl.BlockSpec((1, 2, 2, c, LANES), lambda bi, p: (p, 0, 0, 0, 0)),
        pl.BlockSpec((1, 2, 2, 1, RET_DV), lambda bi, p: (p, 0, 0, 0, 0)),
    ]
    out_shape = [jax.ShapeDtypeStruct((b, l, RET_HEADS * RET_DV), BF16)]
    out_specs = [seq(l, 2 * RET_DV)]
    if need_ctx:
        out_shape.append(jax.ShapeDtypeStruct((b, lc, RET_HEADS * RET_DV), BF16))
        out_specs.append(seq(lc, 2 * RET_DV))
    res = pl.pallas_call(
        functools.partial(_ret_body, need_ctx=need_ctx),
        grid=(b, npair),
        in_specs=specs,
        out_specs=out_specs,
        out_shape=out_shape,
        scratch_shapes=[
            pltpu.VMEM((l, LANES), F32),
            pltpu.VMEM((l, LANES), F32),
            pltpu.VMEM((lc, LANES), F32),
            pltpu.VMEM((lc, LANES), F32),
            pltpu.VMEM((l, 2 * RET_DV), F32),
            pltpu.VMEM((lc, 2 * RET_DV), F32),
        ],
        compiler_params=_params(("parallel", "parallel")),
    )(*args)
    return (res[0], res[1]) if need_ctx else (res[0], None)


def _merge_body(x_ref, rec_ref, gl_ref, yb_ref, rn_ref, gr_ref, gm_ref, ga_ref,
                woa_ref, wob_ref, woc_ref, wout_ref, o_ref):
    d = x_ref.shape[2]
    ya = (jax.nn.gelu(gl_ref[0].astype(F32)) * rec_ref[0].astype(F32)).astype(BF16)
    yc = (jax.nn.silu(gr_ref[0].astype(F32)) * rn_ref[0].astype(F32)).astype(BF16)
    gm = gm_ref[0]
    m = jax.nn.sigmoid(gm[:, :d].astype(F32)) * jnp.dot(ya, woa_ref[...], preferred_element_type=F32)
    m = m + jax.nn.sigmoid(gm[:, d:2 * d].astype(F32)) * jnp.dot(
        yb_ref[0], wob_ref[...], preferred_element_type=F32)
    m = m + jax.nn.sigmoid(gm[:, 2 * d:].astype(F32)) * jnp.dot(
        yc, woc_ref[...], preferred_element_type=F32)
    y = jnp.dot(m.astype(BF16), wout_ref[...], preferred_element_type=F32)
    o_ref[0] = x_ref[0] + ga_ref[0] * y


def _merge(x, rec, gl, yb, rn, gr, gm, gate, woa, wob, woc, wout, tm):
    b, l, d = x.shape
    row = lambda wd: pl.BlockSpec((1, tm, wd), lambda bi, i: (bi, i, 0))
    mod_map = (lambda bi, i: (bi, 0, 0)) if gate.shape[0] > 1 else (lambda bi, i: (0, 0, 0))
    return pl.pallas_call(
        _merge_body,
        grid=(b, l // tm),
        in_specs=[
            row(d), row(LRU_W), row(LRU_W), row(MLA_HEADS * MLA_V),
            row(RET_HEADS * RET_DV), row(RET_HEADS * RET_DV), row(3 * d),
            pl.BlockSpec((1, 1, d), mod_map),
            _resident(woa.shape), _resident(wob.shape), _resident(woc.shape), _resident(wout.shape),
        ],
        out_specs=row(d),
        out_shape=jax.ShapeDtypeStruct((b, l, d), F32),
        compiler_params=_params(("parallel", "parallel")),
    )(x, rec, gl, yb, rn, gr, gm, gate, woa, wob, woc, wout)


FFN_CHUNK = 1408


def _ffn_body(x_ref, g_ref, sh_ref, sc_ref, ga_ref, w1_ref, w3_ref, w2_ref, *rest, final):
    if final:
        gf_ref, o_ref = rest
    else:
        (o_ref,) = rest
    x = x_ref[0]
    hb = (_rms(x, g_ref[...]) * (1.0 + sc_ref[0]) + sh_ref[0]).astype(BF16)
    acc = None
    for c0 in range(0, D_FF, FFN_CHUNK):
        a = jnp.dot(hb, w1_ref[:, c0:c0 + FFN_CHUNK], preferred_element_type=F32)
        bgate = jnp.dot(hb, w3_ref[:, c0:c0 + FFN_CHUNK], preferred_element_type=F32)
        t = (jax.nn.silu(a) * bgate).astype(BF16)
        part = jnp.dot(t, w2_ref[c0:c0 + FFN_CHUNK, :], preferred_element_type=F32)
        acc = part if acc is None else acc + part
    y = x + ga_ref[0] * acc
    if final:
        y = _rms(y, gf_ref[...])
    o_ref[0] = y


def _ffn(x, g, shift, scale, gate, w1, w3, w2, g_final, tm):
    b, l, d = x.shape
    final = g_final is not None
    row = pl.BlockSpec((1, tm, d), lambda bi, i: (bi, i, 0))
    mod_map = (lambda bi, i: (bi, 0, 0)) if shift.shape[0] > 1 else (lambda bi, i: (0, 0, 0))
    mod = pl.BlockSpec((1, 1, d), mod_map)
    args = [x, g.reshape(1, d), shift, scale, gate, w1, w3, w2]
    specs = [row, _resident((1, d)), mod, mod, mod,
             _resident(w1.shape), _resident(w3.shape), _resident(w2.shape)]
    if final:
        args.append(g_final.reshape(1, d))
        specs.append(_resident((1, d)))
    return pl.pallas_call(
        functools.partial(_ffn_body, final=final),
        grid=(b, l // tm),
        in_specs=specs,
        out_specs=row,
        out_shape=jax.ShapeDtypeStruct((b, l, d), F32),
        compiler_params=_params(("parallel", "parallel")),
    )(*args)


def _swap_half(w):
    half = w.shape[-1] // 2
    return jnp.concatenate([-w[..., half:], w[..., :half]], axis=-1)


def _arrange_w_in(w_in):
    d = w_in.shape[0]
    o = 0
    parts = {}
    for name, n in (("lru_x", LRU_W), ("ckv", KV_LORA), ("kr", MLA_ROPE),
                    ("ret_k", RET_HEADS * RET_DK), ("ret_v", RET_HEADS * RET_DV),
                    ("cq", Q_LORA), ("ret_q", RET_HEADS * RET_DK), ("g_lru", LRU_W),
                    ("g_ret", RET_HEADS * RET_DV), ("gm", 3 * D_MODEL)):
        parts[name] = w_in[:, o:o + n]
        o += n
    z = lambda n: jnp.zeros((d, n), w_in.dtype)
    rope_a = jnp.concatenate([z(MLA_NOPE), parts["kr"], z(LANES - MLA_NOPE - MLA_ROPE)], axis=1)
    rope_b = jnp.concatenate([z(MLA_NOPE), _swap_half(parts["kr"]),
                              z(LANES - MLA_NOPE - MLA_ROPE)], axis=1)
    cols = [parts["lru_x"], parts["ckv"], rope_a, rope_b, parts["ret_k"], parts["ret_v"],
            parts["cq"], parts["ret_q"], parts["g_lru"], parts["g_ret"], parts["gm"]]
    return jnp.concatenate(cols, axis=1).astype(BF16)


def _arrange_mla(w_uq, w_ukv):
    nh = MLA_HEADS
    wq = w_uq.reshape(Q_LORA, nh, MLA_NOPE + MLA_ROPE)
    padq = jnp.zeros((Q_LORA, nh, LANES - MLA_NOPE - MLA_ROPE), w_uq.dtype)
    wq_arr = jnp.concatenate([wq, padq], axis=-1).reshape(Q_LORA, nh * LANES)
    wqs = jnp.concatenate([jnp.zeros((Q_LORA, nh, MLA_NOPE), w_uq.dtype),
                           _swap_half(wq[..., MLA_NOPE:]), padq], axis=-1)
    wqs_arr = wqs.reshape(Q_LORA, nh * LANES)
    wkv = w_ukv.reshape(KV_LORA, nh, MLA_NOPE + MLA_V)
    zk = jnp.zeros((KV_LORA, nh, LANES - MLA_NOPE), w_ukv.dtype)
    wk_arr = jnp.concatenate([wkv[..., :MLA_NOPE], zk], axis=-1).reshape(KV_LORA, nh * LANES)
    wv = wkv[..., MLA_NOPE:]
    zv = jnp.zeros_like(wv)
    even = (jnp.arange(nh) % 2 == 0)[None, :, None]
    wv_arr = jnp.concatenate([jnp.where(even, wv, zv), jnp.where(even, zv, wv)], axis=-1)
    wv_arr = wv_arr.reshape(KV_LORA, nh * LANES)
    return (wq_arr.astype(BF16), wqs_arr.astype(BF16),
            jnp.concatenate([wk_arr, wv_arr], axis=1).astype(BF16))


def _arrange_lru(wa, ba, wx, bx):
    def dense(wblk):
        eye = jnp.eye(LRU_BLOCKS, dtype=wblk.dtype)
        full = jnp.einsum("nde,nm->ndme", wblk, eye)
        return full.reshape(LRU_W, LRU_W)

    wg = jnp.stack([jnp.concatenate([dense(wa[d]), dense(wx[d])], axis=1) for d in range(2)])
    bg = jnp.stack([jnp.concatenate([ba[d], bx[d]])[None, :] for d in range(2)])
    return wg.astype(BF16), bg


def _mla_tables(n_tokens, rotate):
    scale = (MLA_NOPE + MLA_ROPE) ** -0.5
    pad = LANES - MLA_NOPE - MLA_ROPE
    ones = jnp.ones((n_tokens, MLA_NOPE), F32)
    zeros = jnp.zeros((n_tokens, MLA_NOPE), F32)
    zpad = jnp.zeros((n_tokens, pad), F32)
    if rotate:
        rows = n_tokens // GRID_W
        rowi = jnp.repeat(jnp.arange(rows, dtype=F32), GRID_W)
        coli = jnp.tile(jnp.arange(GRID_W, dtype=F32), rows)
        n_freq = MLA_ROPE // 4
        inv = jnp.power(ROPE_BASE, -jnp.arange(n_freq, dtype=F32) / n_freq)
        ang = jnp.concatenate([rowi[:, None] * inv, coli[:, None] * inv], axis=-1)
        cos, sin = jnp.cos(ang), jnp.sin(ang)
    else:
        cos = jnp.ones((n_tokens, MLA_ROPE // 2), F32)
        sin = jnp.zeros((n_tokens, MLA_ROPE // 2), F32)
    cosq = jnp.concatenate([ones, cos, cos, zpad], axis=1) * scale
    sinq = jnp.concatenate([zeros, sin, sin, zpad], axis=1) * scale
    cosk = jnp.concatenate([zeros, cos, cos, zpad], axis=1)
    sink = jnp.concatenate([zeros, sin, sin, zpad], axis=1)
    return cosq, sinq, cosk, sink


def _ret_tables(start, n):
    theta = 1.0 / jnp.power(10000.0, jnp.linspace(0.0, 1.0, RET_DK // 2, dtype=F32))
    pos = start + jnp.arange(n, dtype=F32)
    ang = pos[:, None] * theta
    cos, sin = jnp.cos(ang), jnp.sin(ang)
    reps = LANES // RET_DK
    cos_t = jnp.tile(jnp.concatenate([cos, cos], axis=1), (1, reps))
    sin_t = jnp.tile(jnp.concatenate([-sin, sin], axis=1), (1, reps))
    kscale = RET_DK ** -0.5
    return cos_t, sin_t, cos_t * kscale, sin_t * kscale


def _ret_decay_tables():
    c = RET_CHUNK
    h = jnp.arange(RET_HEADS, dtype=F32)
    lgs = jnp.stack([jnp.log1p(-jnp.exp2(-5.0 - h)), jnp.log1p(-jnp.exp2(-5.5 - h))], axis=1)
    pos = jnp.arange(c, dtype=F32)
    diff = pos[:, None] - pos[None, :]
    lg = lgs[:, :, None, None]
    inner_f = jnp.where(diff >= 0, jnp.exp(lg * jnp.maximum(diff, 0.0)), 0.0)
    inner = jnp.stack([inner_f[:, 0], jnp.swapaxes(inner_f[:, 1], -1, -2)], axis=1)
    lg1 = lgs[:, :, None]
    qd_f = jnp.exp(lg1 * (pos + 1.0))
    kd_f = jnp.exp(lg1 * (c - 1.0 - pos))
    qd = jnp.stack([qd_f[:, 0], qd_f[:, 1, ::-1]], axis=1)
    kd = jnp.stack([kd_f[:, 0], kd_f[:, 1, ::-1]], axis=1)
    lane_head = (jnp.arange(LANES) // RET_DK)
    e_of_h = jnp.arange(RET_HEADS) % 2
    mask = (lane_head[None, :] == e_of_h[:, None]).astype(F32)
    qdec = qd[..., None] * mask[:, None, None, :]
    kdec = kd[..., None] * mask[:, None, None, :]
    cd = jnp.broadcast_to(jnp.exp(lgs * c)[:, :, None, None], (RET_HEADS, 2, 1, RET_DV))
    pair = lambda t: t.reshape(RET_HEADS // 2, 2, *t.shape[1:])
    return pair(inner), pair(qdec), pair(kdec), pair(cd)


def _row_tile(n, pref):
    t = min(pref, n)
    while n % t:
        t //= 2
    return t


def kernel(x, c, ctx, c_ctx, w_mod, b_mod, g_mix, g_ffn, w_in, conv_w, conv_b, lru_wa, lru_ba,
           lru_wx, lru_bx, lru_lam, g_q, w_uq, g_kv, w_ukv, w_oa, w_ob, w_oc, w_out,
           w_ff1, w_ff3, w_ff2, g_final):
    b, l, d = x.shape
    lc = ctx.shape[1]
    depth = w_mod.shape[0]
    assert d == D_MODEL and l % LRU_TC == 0 and lc % LRU_TC == 0 and l % GRID_W == 0

    tm = _row_tile(l, 512)
    tmc = _row_tile(lc, 256)
    tq = _row_tile(l, 256)
    tqc = _row_tile(lc, 256)

    lat_tabs = _mla_tables(l, True)
    ctx_tabs = _mla_tables(lc, False)
    ret_lat = _ret_tables(float(lc), l)
    ret_ctx = _ret_tables(0.0, lc)
    dec = _ret_decay_tables()

    rows = -(-(b + 1) // SUBLANES) * SUBLANES
    cvec = jnp.zeros((rows, d), F32).at[:b].set(c).at[b].set(c_ctx)
    mod_all = _modulation(cvec, w_mod, b_mod)

    xc = ctx
    for li in range(depth):
        last = li == depth - 1
        mod = mod_all[li, :b].reshape(b, 1, 6, d)
        sh_a, sc_a, ga_a, sh_f, sc_f, ga_f = (mod[:, :, j] for j in range(6))
        mc = mod_all[li, b].reshape(1, 1, 6, d)
        mcs = [mc[:, :, j] for j in range(6)]

        w_arr = _arrange_w_in(w_in[li])
        wq_arr, wqs_arr, wkv_arr = _arrange_mla(w_uq[li], w_ukv[li])
        wg, bg = _arrange_lru(lru_wa[li], lru_ba[li], lru_wx[li], lru_bx[li])
        woa, wob, woc, wout = (w.astype(BF16) for w in (w_oa[li], w_ob[li], w_oc[li], w_out[li]))
        w1, w3, w2 = (w.astype(BF16) for w in (w_ff1[li], w_ff3[li], w_ff2[li]))

        z = _inproj(x, g_mix[li], sh_a, sc_a, w_arr, KV_GROUPS + Q_GROUPS, tm)
        u, ckv, ra, rb, rk, rv, cq, rq, g_lru, g_ret, gm = z
        if last:
            zc = _inproj(xc, g_mix[li], mcs[0], mcs[1], w_arr[:, :N_KV_ARR], KV_GROUPS, tmc)
            uc, ckvc, rac, _, rkc, rvc = zc
            cqc = rqc = None
        else:
            zc = _inproj(xc, g_mix[li], mcs[0], mcs[1], w_arr, KV_GROUPS + Q_GROUPS, tmc)
            uc, ckvc, rac, _, rkc, rvc, cqc, rqc, g_lru_c, g_ret_c, gm_c = zc
        need_ctx = not last

        rec, rec_c = _lru(u, uc, conv_w[li], conv_b[li], wg, bg, lru_lam[li], need_ctx)

        q, k, v = _mla_prep(cq, ckv, ra, rb, *lat_tabs, g_q[li], wq_arr, wqs_arr, g_kv[li],
                            wkv_arr, True, True, tm)
        qc, kc, vc = _mla_prep(cqc, ckvc, rac, None, *ctx_tabs, g_q[li], wq_arr, None, g_kv[li],
                               wkv_arr, False, need_ctx, tmc)
        yb = _attention(q, [(kc, vc), (k, v)], tq)

        rn, rn_c = _retention(rq, rk, rv, rqc, rkc, rvc, ret_lat, ret_ctx, dec, need_ctx)

        x = _merge(x, rec, g_lru, yb, rn, g_ret, gm, ga_a, woa, wob, woc, wout, tm)
        x = _ffn(x, g_ffn[li], sh_f, sc_f, ga_f, w1, w3, w2, g_final if last else None, tm)
        if need_ctx:
            yb_c = _attention(qc, [(kc, vc)], tqc)
            xc = _merge(xc, rec_c, g_lru_c, yb_c, rn_c, g_ret_c, gm_c, mcs[2], woa, wob, woc, wout, tmc)
            xc = _ffn(xc, g_ffn[li], mcs[3], mcs[4], mcs[5], w1, w3, w2, None, tmc)
    return x
```

```python
import functools
import math

import jax
import jax.numpy as jnp
from jax import lax
from jax.experimental import pallas as pl
from jax.experimental.pallas import tpu as pltpu

F32 = jnp.float32
BF16 = jnp.bfloat16

D_MODEL = 1024
EPS = 1e-6
GRID_W = 64
LRU_W = 512
LRU_BLOCKS = 8
LRU_BLOCK = LRU_W // LRU_BLOCKS
CONV_W = 4
CONV_PAD_L = 2
LRU_C = 8.0
MLA_HEADS = 8
MLA_NOPE = 64
MLA_ROPE = 32
MLA_V = 64
Q_LORA = 384
KV_LORA = 256
ROPE_BASE = 10000.0
RET_HEADS = 4
RET_DK = 64
RET_DV = 128
RET_CHUNK = 128
D_FF = 2816

LANES = 128
SUBLANES = 8
VMEM_LIMIT = 56 * 1024 * 1024

KV_GROUPS = (LRU_W, KV_LORA, LANES, LANES, RET_HEADS * RET_DK, RET_HEADS * RET_DV)
Q_GROUPS = (Q_LORA, RET_HEADS * RET_DK, LRU_W, RET_HEADS * RET_DV, 3 * D_MODEL)
N_KV_ARR = sum(KV_GROUPS)


def _params(sem):
    return pltpu.CompilerParams(dimension_semantics=sem, vmem_limit_bytes=VMEM_LIMIT)


def _resident(shape):
    nd = len(shape)
    return pl.BlockSpec(shape, lambda *_: (0,) * nd, pipeline_mode=pl.Buffered(1))


def _rms(x, g):
    y = x * lax.rsqrt(jnp.mean(x * x, axis=-1, keepdims=True) + EPS)
    return y * g


def _mod_body(c_ref, w_ref, b_ref, o_ref):
    cv = c_ref[...]
    act = cv * jax.nn.sigmoid(cv)
    o_ref[0] = jnp.dot(act, w_ref[0], precision=lax.Precision.HIGHEST,
                       preferred_element_type=F32) + b_ref[0]


def _modulation(cvec, w_mod, b_mod):
    depth, d, n = w_mod.shape
    rows = cvec.shape[0]
    tn = 1536
    return pl.pallas_call(
        _mod_body,
        grid=(depth, n // tn),
        in_specs=[
            pl.BlockSpec((rows, d), lambda l, j: (0, 0)),
            pl.BlockSpec((1, d, tn), lambda l, j: (l, 0, j)),
            pl.BlockSpec((1, 1, tn), lambda l, j: (l, 0, j)),
        ],
        out_specs=pl.BlockSpec((1, rows, tn), lambda l, j: (l, 0, j)),
        out_shape=jax.ShapeDtypeStruct((depth, rows, n), F32),
        compiler_params=_params(("arbitrary", "arbitrary")),
    )(cvec, w_mod, b_mod.reshape(depth, 1, n))


def _inproj_body(x_ref, g_ref, sh_ref, sc_ref, w_ref, *out_refs, widths):
    x = x_ref[0]
    h = _rms(x, g_ref[...]) * (1.0 + sc_ref[0]) + sh_ref[0]
    hb = h.astype(BF16)
    off = 0
    for o_ref, wd in zip(out_refs, widths):
        o_ref[0] = jnp.dot(hb, w_ref[:, off:off + wd],
                           preferred_element_type=F32).astype(o_ref.dtype)
        off += wd


def _inproj(x, g, shift, scale, w, widths, tm):
    b, l, d = x.shape
    n = w.shape[1]
    assert n == sum(widths)
    mod_map = (lambda bi, i: (bi, 0, 0)) if shift.shape[0] > 1 else (lambda bi, i: (0, 0, 0))
    return pl.pallas_call(
        functools.partial(_inproj_body, widths=widths),
        grid=(b, l // tm),
        in_specs=[
            pl.BlockSpec((1, tm, d), lambda bi, i: (bi, i, 0)),
            _resident((1, d)),
            pl.BlockSpec((1, 1, d), mod_map),
            pl.BlockSpec((1, 1, d), mod_map),
            _resident((d, n)),
        ],
        out_specs=[pl.BlockSpec((1, tm, wd), lambda bi, i: (bi, i, 0)) for wd in widths],
        out_shape=[jax.ShapeDtypeStruct((b, l, wd), BF16) for wd in widths],
        compiler_params=_params(("parallel", "parallel")),
    )(x, g.reshape(1, d), shift, scale, w)


LRU_TC = 256
LRU_HALO = SUBLANES


def _lru_body(u_ref, uc_ref, cw_ref, cb_ref, wg_ref, bg_ref, lam_ref, *rest, need_ctx):
    if need_ctx:
        out_ref, outc_ref, upad, upadc, ucv, ucvc, hbuf = rest
    else:
        out_ref, upad, upadc, ucv, ucvc, hbuf = rest
        outc_ref = None
    l = u_ref.shape[1]
    lc = uc_ref.shape[1]
    w = u_ref.shape[2]
    tc = LRU_TC
    nt = tc // SUBLANES

    zeros_halo = jnp.zeros((LRU_HALO, w), F32)
    for pad_ref, cv_ref, src_ref, n in ((upad, ucv, u_ref, l), (upadc, ucvc, uc_ref, lc)):
        pad_ref[0:LRU_HALO, :] = zeros_halo
        pad_ref[LRU_HALO + n:2 * LRU_HALO + n, :] = zeros_halo

        def copy(i, carry, pad_ref=pad_ref, src_ref=src_ref):
            t0 = pl.multiple_of(i * tc, tc)
            pad_ref[pl.ds(LRU_HALO + t0, tc), :] = src_ref[0, pl.ds(t0, tc), :].astype(F32)
            return carry

        lax.fori_loop(0, n // tc, copy, 0)

        def conv(i, carry, pad_ref=pad_ref, cv_ref=cv_ref):
            t0 = pl.multiple_of(i * tc, tc)
            win = pad_ref[pl.ds(t0, tc + 2 * LRU_HALO), :]
            u = cb_ref[...]
            for k in range(CONV_W):
                sh = (CONV_PAD_L - k) % (tc + 2 * LRU_HALO)
                tap = win if sh == 0 else pltpu.roll(win, sh, 0)
                u = u + cw_ref[k:k + 1, :] * tap[LRU_HALO:LRU_HALO + tc]
            cv_ref[pl.ds(t0, tc), :] = u
            return carry

        lax.fori_loop(0, n // tc, conv, 0)

    row = lax.broadcasted_iota(jnp.int32, (nt, SUBLANES, w), 1)

    def chunk_scan(cv_ref, t0, d, h):
        u = cv_ref[pl.ds(t0, tc), :]
        g = jnp.dot(u.astype(BF16), wg_ref[d], preferred_element_type=F32) + bg_ref[d]
        r = jax.nn.sigmoid(g[:, :w])
        gi = jax.nn.sigmoid(g[:, w:])
        z = -lam_ref[d]
        sp = jnp.maximum(z, 0.0) + jnp.log1p(jnp.exp(-jnp.abs(z)))
        log_a = -LRU_C * r * sp
        a = jnp.exp(log_a)
        th = jnp.tanh(log_a)
        om = -2.0 * th / (1.0 - th)
        root = jnp.where(om > 0.0, om * lax.rsqrt(om), 0.0)
        bb = root * (gi * u)
        a = a.reshape(nt, SUBLANES, w)
        bb = bb.reshape(nt, SUBLANES, w)
        s = 1
        while s < SUBLANES:
            if d == 0:
                keep = row >= s
                shift = s
            else:
                keep = row < SUBLANES - s
                shift = SUBLANES - s
            a_s = jnp.where(keep, pltpu.roll(a, shift, 1), 1.0)
            b_s = jnp.where(keep, pltpu.roll(bb, shift, 1), 0.0)
            bb = a * b_s + bb
            a = a * a_s
            s *= 2
        order = range(nt) if d == 0 else range(nt - 1, -1, -1)
        for k in order:
            hk = a[k] * h + bb[k]
            hbuf[k * SUBLANES:(k + 1) * SUBLANES, :] = hk
            h = hk[SUBLANES - 1:SUBLANES] if d == 0 else hk[0:1]
        return h

    def run(cv_ref, n, d, h, write):
        nchunks = n // tc

        def body(i, h):
            ci = i if d == 0 else nchunks - 1 - i
            t0 = pl.multiple_of(ci * tc, tc)
            h = chunk_scan(cv_ref, t0, d, h)
            write(t0)
            return h

        return lax.fori_loop(0, nchunks, body, h)

    def write_accc(t0):
        upadc[pl.ds(t0, tc), :] = hbuf[...]

    def write_acc(t0):
        upad[pl.ds(t0, tc), :] = hbuf[...]

    def write_outc(t0):
        if outc_ref is not None:
            outc_ref[0, pl.ds(t0, tc), :] = (upadc[pl.ds(t0, tc), :] + hbuf[...]).astype(outc_ref.dtype)

    def write_out(t0):
        out_ref[0, pl.ds(t0, tc), :] = (upad[pl.ds(t0, tc), :] + hbuf[...]).astype(out_ref.dtype)

    h0 = jnp.zeros((1, w), F32)
    h = run(ucvc, lc, 0, h0, write_accc)
    run(ucv, l, 0, h, write_acc)
    h = run(ucvc, lc, 1, h0, write_outc)
    run(ucv, l, 1, h, write_out)


def _lru(u, uc, conv_w, conv_b, wg, bg, lam, need_ctx):
    b, l, w = u.shape
    lc = uc.shape[1]
    out_shape = [jax.ShapeDtypeStruct((b, l, w), BF16)]
    out_specs = [pl.BlockSpec((1, l, w), lambda bi: (bi, 0, 0))]
    if need_ctx:
        out_shape.append(jax.ShapeDtypeStruct((b, lc, w), BF16))
        out_specs.append(pl.BlockSpec((1, lc, w), lambda bi: (bi, 0, 0)))
    res = pl.pallas_call(
        functools.partial(_lru_body, need_ctx=need_ctx),
        grid=(b,),
        in_specs=[
            pl.BlockSpec((1, l, w), lambda bi: (bi, 0, 0)),
            pl.BlockSpec((1, lc, w), lambda bi: (bi, 0, 0)),
            _resident((CONV_W, w)),
            _resident((1, w)),
            _resident((2, w, 2 * w)),
            _resident((2, 1, 2 * w)),
            _resident((2, 1, w)),
        ],
        out_specs=out_specs,
        out_shape=out_shape,
        scratch_shapes=[
            pltpu.VMEM((l + 2 * LRU_HALO, w), F32),
            pltpu.VMEM((lc + 2 * LRU_HALO, w), F32),
            pltpu.VMEM((l, w), F32),
            pltpu.VMEM((lc, w), F32),
            pltpu.VMEM((LRU_TC, w), F32),
        ],
        compiler_params=_params(("parallel",)),
    )(u, uc, conv_w, conv_b.reshape(1, w), wg, bg, lam.reshape(2, 1, w))
    return (res[0], res[1]) if need_ctx else (res[0], None)


def _mla_prep_body(*refs, rotate, need_q):
    it = iter(refs)
    cq_ref = next(it) if need_q else None
    ckv_ref, ra_ref = next(it), next(it)
    rb_ref = next(it) if rotate else None
    if need_q:
        cosq_ref = next(it)
        sinq_ref = next(it) if rotate else None
    cosk_ref = next(it)
    sink_ref = next(it) if rotate else None
    if need_q:
        gq_ref, wq_ref = next(it), next(it)
        wqs_ref = next(it) if rotate else None
    gkv_ref, wkv_ref = next(it), next(it)
    q_out = next(it) if need_q else None
    k_out, v_out = next(it), next(it)
    nh = MLA_HEADS
    tm = ckv_ref.shape[1]

    if need_q:
        cqn = _rms(cq_ref[0].astype(F32), gq_ref[...]).astype(BF16)
        qm = jnp.dot(cqn, wq_ref[...], preferred_element_type=F32)
        cosq = cosq_ref[...]
        if rotate:
            qs = jnp.dot(cqn, wqs_ref[...], preferred_element_type=F32)
            sinq = sinq_ref[...]
        for h in range(nh):
            sl = slice(h * LANES, (h + 1) * LANES)
            qh = qm[:, sl] * cosq
            if rotate:
                qh = qh + qs[:, sl] * sinq
            q_out[0, h] = qh.astype(q_out.dtype)

    kvn = _rms(ckv_ref[0].astype(F32), gkv_ref[...]).astype(BF16)
    kv = jnp.dot(kvn, wkv_ref[...], preferred_element_type=F32)
    rope = ra_ref[0].astype(F32) * cosk_ref[...]
    if rotate:
        rope = rope + rb_ref[0].astype(F32) * sink_ref[...]
    lane = lax.broadcasted_iota(jnp.int32, (tm, LANES), 1)
    for h in range(nh):
        k_out[0, h] = (kv[:, h * LANES:(h + 1) * LANES] + rope).astype(k_out.dtype)
        vh = kv[:, (nh + h) * LANES:(nh + h + 1) * LANES]
        v_out[0, h] = jnp.where(lane == _ones_lane(h), 1.0, vh).astype(v_out.dtype)


def _ones_lane(h):
    return MLA_V if h % 2 == 0 else 0


def _mla_prep(cq, ckv, ra, rb, cosq, sinq, cosk, sink, gq, wq, wqs, gkv, wkv, rotate, need_q, tm):
    b, l, _ = ckv.shape
    nh = MLA_HEADS
    row = lambda wd: pl.BlockSpec((1, tm, wd), lambda bi, i: (bi, i, 0))
    tab = pl.BlockSpec((tm, LANES), lambda bi, i: (i, 0))
    args, specs = [], []

    def add(a, s):
        args.append(a)
        specs.append(s)

    if need_q:
        add(cq, row(Q_LORA))
    add(ckv, row(KV_LORA))
    add(ra, row(LANES))
    if rotate:
        add(rb, row(LANES))
    if need_q:
        add(cosq, tab)
        if rotate:
            add(sinq, tab)
    add(cosk, tab)
    if rotate:
        add(sink, tab)
    if need_q:
        add(gq.reshape(1, Q_LORA), _resident((1, Q_LORA)))
        add(wq, _resident(wq.shape))
        if rotate:
            add(wqs, _resident(wqs.shape))
    add(gkv.reshape(1, KV_LORA), _resident((1, KV_LORA)))
    add(wkv, _resident(wkv.shape))

    head = pl.BlockSpec((1, nh, tm, LANES), lambda bi, i: (bi, 0, i, 0))
    hshape = jax.ShapeDtypeStruct((b, nh, l, LANES), BF16)
    n_out = 3 if need_q else 2
    res = pl.pallas_call(
        functools.partial(_mla_prep_body, rotate=rotate, need_q=need_q),
        grid=(b, l // tm),
        in_specs=specs,
        out_specs=[head] * n_out,
        out_shape=[hshape] * n_out,
        compiler_params=_params(("parallel", "parallel")),
    )(*args)
    return res if need_q else (None, res[0], res[1])


ATTN_PAIRS_PER_STEP = 2


def _attn_body(q_ref, *refs, n_kv, tq):
    kv_refs = refs[:2 * n_kv]
    o_ref = refs[2 * n_kv]
    s_bufs = refs[2 * n_kv + 1:2 * n_kv + 3]
    p_bufs = refs[2 * n_kv + 3:2 * n_kv + 5]
    npair = ATTN_PAIRS_PER_STEP
    n_items = (q_ref.shape[2] // tq) * npair
    lks = [kv_refs[2 * j].shape[2] for j in range(n_kv)]
    offs = [sum(lks[:j]) for j in range(n_kv)]
    lane = lax.broadcasted_iota(jnp.int32, (tq, LANES), 1)
    nt_dims = (((1,), (1,)), ((), ()))

    s_bufs[1][...] = jnp.zeros(s_bufs[1].shape, s_bufs[1].dtype)
    p_bufs[0][...] = jnp.zeros(p_bufs[0].shape, p_bufs[0].dtype)

    def item(t):
        tile = lax.shift_right_logical(t, 1)
        pair = jnp.bitwise_and(t, npair - 1)
        return pl.multiple_of(tile * tq, tq), pair

    def stages(t, slot):
        s_a, s_b = s_bufs[slot], s_bufs[1 - slot]
        p_b, p_c = p_bufs[1 - slot], p_bufs[slot]

        row_a, pair_a = item(jnp.minimum(t, n_items - 1))
        for e in range(2):
            h = pair_a * 2 + e
            q = q_ref[0, h, pl.ds(row_a, tq), :]
            for j in range(n_kv):
                s_a[e, :, offs[j]:offs[j] + lks[j]] = lax.dot_general(
                    q, kv_refs[2 * j][0, h], nt_dims, preferred_element_type=F32)

        for e in range(2):
            s = s_b[e]
            m = s.max(axis=-1, keepdims=True)
            p_b[e] = jnp.exp2(s - m).astype(p_b.dtype)

        row_c, pair_c = item(jnp.maximum(t - 2, 0))
        halves = []
        for e in range(2):
            h = pair_c * 2 + e
            o = None
            for j in range(n_kv):
                oj = jnp.dot(p_c[e, :, offs[j]:offs[j] + lks[j]], kv_refs[2 * j + 1][0, h],
                             preferred_element_type=F32)
                o = oj if o is None else o + oj
            one = _ones_lane(e)
            halves.append(o * (1.0 / o[:, one:one + 1]))
        blk = jnp.where(lane < MLA_V, halves[0], halves[1])
        o_ref[0, pair_c, pl.ds(row_c, tq), :] = blk.astype(o_ref.dtype)

    def body(i, carry):
        stages(2 * i, 0)
        stages(2 * i + 1, 1)
        return carry

    lax.fori_loop(0, (n_items + 2) // 2, body, 0)


def _attention(q, kvs, tq):
    b, nh, lq, _ = q.shape
    hs = 2 * ATTN_PAIRS_PER_STEP
    once = lambda n: pl.BlockSpec((1, hs, n, LANES), lambda bi, j: (bi, j, 0, 0),
                                  pipeline_mode=pl.Buffered(1))
    specs = [once(lq)]
    args = [q]
    lkt = 0
    for k, v in kvs:
        lk = k.shape[2]
        lkt += lk
        specs += [once(lk), once(lk)]
        args += [k, v]
    return pl.pallas_call(
        functools.partial(_attn_body, n_kv=len(kvs), tq=tq),
        grid=(b, nh // hs),
        in_specs=specs,
        out_specs=pl.BlockSpec((1, ATTN_PAIRS_PER_STEP, lq, LANES), lambda bi, j: (bi, j, 0, 0)),
        out_shape=jax.ShapeDtypeStruct((b, nh // 2, lq, LANES), BF16),
        scratch_shapes=[
            pltpu.VMEM((2, tq, lkt), F32),
            pltpu.VMEM((2, tq, lkt), F32),
            pltpu.VMEM((2, tq, lkt), BF16),
            pltpu.VMEM((2, tq, lkt), BF16),
        ],
        compiler_params=_params(("parallel", "parallel")),
    )(*args)


def _ret_body(*refs, need_ctx):
    it = iter(refs)
    q_ref, k_ref, v_ref = next(it), next(it), next(it)
    qc_ref = next(it) if need_ctx else None
    kc_ref, vc_ref = next(it), next(it)
    cos_ref, sin_ref = next(it), next(it)
    msum_ref, qdec_ref, kdec_ref, cd_ref = next(it), next(it), next(it), next(it)
    o_ref = next(it)
    oc_ref = next(it) if need_ctx else None
    qa, ka, va, u_s, r_s = (next(it) for _ in range(5))

    c = RET_CHUNK
    l = q_ref.shape[1]
    lc = kc_ref.shape[1]
    ncc = lc // c
    nt = (lc + l) // c
    kscale = RET_DK ** -0.5
    lane = lax.broadcasted_iota(jnp.int32, (c, LANES), 1)
    first_half = (lane % RET_DK) < (RET_DK // 2)

    def rotated(src_ref, t0, g0):
        x = src_ref[0, pl.ds(t0, c), :].astype(F32)
        sw = jnp.where(first_half, pltpu.roll(x, LANES - RET_DK // 2, 1),
                       pltpu.roll(x, RET_DK // 2, 1))
        return x * cos_ref[pl.ds(g0, c), :] + sw * sin_ref[pl.ds(g0, c), :]

    def stage(qsrc, ksrc, vsrc, n, base):
        def body(i, carry):
            t0 = pl.multiple_of(i * c, c)
            g0 = pl.multiple_of(base + i * c, c)
            if qsrc is not None:
                qa[pl.ds(g0, c), :] = rotated(qsrc, t0, g0)
            ka[pl.ds(g0, c), :] = rotated(ksrc, t0, g0) * kscale
            va[pl.ds(g0, c), :] = vsrc[0, pl.ds(t0, c), :]
            return carry

        lax.fori_loop(0, n // c, body, 0)

    stage(qc_ref, kc_ref, vc_ref, lc, 0)
    stage(q_ref, k_ref, v_ref, l, lc)

    tn_dims = (((0,), (0,)), ((), ()))
    nt_dims = (((1,), (1,)), ((), ()))

    def incr(g, carry):
        g0 = pl.multiple_of(g * c, c)
        kf = ka[pl.ds(g0, c), :]
        for e in range(2):
            v = va[pl.ds(g0, c), e * RET_DV:(e + 1) * RET_DV]
            kd = jnp.concatenate([kf * kdec_ref[0, e, 0], kf * kdec_ref[0, e, 1]], axis=1)
            u_s[g, e] = lax.dot_general(kd.astype(BF16), v, tn_dims, preferred_element_type=F32)
        return carry

    lax.fori_loop(0, nt, incr, 0, unroll=2)

    zero = tuple(jnp.zeros((LANES, RET_DV), F32) for _ in range(4))

    def recur(t, st):
        gs = (t, jnp.where(t < ncc, ncc - 1 - t, nt - 1 + ncc - t))
        new = []
        for d in range(2):
            rows = slice(d * LANES, (d + 1) * LANES)
            for e in range(2):
                r = st[2 * d + e]
                r_s[gs[d], e, rows, :] = r.astype(r_s.dtype)
                new.append(cd_ref[0, e, d] * r + u_s[gs[d], e, rows, :])
        return tuple(new)

    lax.fori_loop(0, nt, recur, zero, unroll=2)

    def emit(dst_ref, gbase):
        def body(i, carry):
            t0 = pl.multiple_of(i * c, c)
            g = gbase + i
            g0 = pl.multiple_of(g * c, c)
            qf = qa[pl.ds(g0, c), :]
            kb = ka[pl.ds(g0, c), :].astype(BF16)
            parts = []
            for e in range(2):
                v = va[pl.ds(g0, c), e * RET_DV:(e + 1) * RET_DV]
                qm = jnp.where((lane // RET_DK) == e, qf, 0.0).astype(BF16)
                s = lax.dot_general(qm, kb, nt_dims, preferred_element_type=F32)
                o = jnp.dot((s * msum_ref[0, e]).astype(BF16), v, preferred_element_type=F32)
                qd = jnp.concatenate([qf * qdec_ref[0, e, 0], qf * qdec_ref[0, e, 1]], axis=1)
                o = o + jnp.dot(qd.astype(BF16), r_s[g, e], preferred_element_type=F32)
                mu = jnp.mean(o, axis=-1, keepdims=True)
                oc = o - mu
                var = jnp.mean(oc * oc, axis=-1, keepdims=True)
                parts.append(oc * lax.rsqrt(var + EPS))
            dst_ref[0, pl.ds(t0, c), :] = jnp.concatenate(parts, axis=1).astype(dst_ref.dtype)
            return carry

        return body

    if need_ctx:
        lax.fori_loop(0, ncc, emit(oc_ref, 0), 0, unroll=2)
    lax.fori_loop(0, nt - ncc, emit(o_ref, ncc), 0, unroll=2)


def _retention(q, k, v, qc, kc, vc, tabs, dec, need_ctx):
    b, l, _ = q.shape
    lc = kc.shape[1]
    lt = lc + l
    c = RET_CHUNK
    nt = lt // c
    npair = RET_HEADS // 2
    seq = lambda n, wd: pl.BlockSpec((1, n, wd), lambda bi, p: (bi, 0, p))
    args = [q, k, v]
    specs = [seq(l, LANES), seq(l, LANES), seq(l, 2 * RET_DV)]
    if need_ctx:
        args.append(qc)
        specs.append(seq(lc, LANES))
    args += [kc, vc]
    specs += [seq(lc, LANES), seq(lc, 2 * RET_DV)]
    for t in tabs:
        args.append(t)
        specs.append(_resident((lt, LANES)))
    msum, qdec, kdec, cd = dec
    args += [msum, qdec, kdec, cd]
    specs += [
        pl.BlockSpec((1, 2, c, c), lambda bi, p: (p, 0, 0, 0)),
        pl.BlockSpec((1, 2, 2, c, LANES), lambda bi, p: (p, 0, 0, 0, 0)),
        pl.BlockSpec((1, 2, 2, c, LANES), lambda bi, p: (p, 0, 0, 0, 0)),
        pl.BlockSpec((1, 2, 2, 1, RET_DV), lambda bi, p: (p, 0, 0, 0, 0)),
    ]
    out_shape = [jax.ShapeDtypeStruct((b, l, RET_HEADS * RET_DV), BF16)]
    out_specs = [seq(l, 2 * RET_DV)]
    if need_ctx:
        out_shape.append(jax.ShapeDtypeStruct((b, lc, RET_HEADS * RET_DV), BF16))
        out_specs.append(seq(lc, 2 * RET_DV))
    res = pl.pallas_call(
        functools.partial(_ret_body, need_ctx=need_ctx),
        grid=(b, npair),
        in_specs=specs,
        out_specs=out_specs,
        out_shape=out_shape,
        scratch_shapes=[
            pltpu.VMEM((lt, LANES), F32),
            pltpu.VMEM((lt, LANES), F32),
            pltpu.VMEM((lt, 2 * RET_DV), BF16),
            pltpu.VMEM((nt, 2, 2 * LANES, RET_DV), F32),
            pltpu.VMEM((nt, 2, 2 * LANES, RET_DV), BF16),
        ],
        compiler_params=_params(("parallel", "parallel")),
    )(*args)
    return (res[0], res[1]) if need_ctx else (res[0], None)


def _merge_body(x_ref, rec_ref, gl_ref, yb_ref, rn_ref, gr_ref, gm_ref, ga_ref,
                woa_ref, wob_ref, woc_ref, wout_ref, o_ref):
    d = x_ref.shape[2]
    ya = (jax.nn.gelu(gl_ref[0].astype(F32)) * rec_ref[0].astype(F32)).astype(BF16)
    yc = (jax.nn.silu(gr_ref[0].astype(F32)) * rn_ref[0].astype(F32)).astype(BF16)
    yb = jnp.concatenate([yb_ref[0, p] for p in range(yb_ref.shape[1])], axis=1)
    gm = gm_ref[0]
    m = jax.nn.sigmoid(gm[:, :d].astype(F32)) * jnp.dot(ya, woa_ref[...], preferred_element_type=F32)
    m = m + jax.nn.sigmoid(gm[:, d:2 * d].astype(F32)) * jnp.dot(
        yb, wob_ref[...], preferred_element_type=F32)
    m = m + jax.nn.sigmoid(gm[:, 2 * d:].astype(F32)) * jnp.dot(
        yc, woc_ref[...], preferred_element_type=F32)
    y = jnp.dot(m.astype(BF16), wout_ref[...], preferred_element_type=F32)
    o_ref[0] = x_ref[0] + ga_ref[0] * y


def _merge(x, rec, gl, yb, rn, gr, gm, gate, woa, wob, woc, wout, tm):
    b, l, d = x.shape
    row = lambda wd: pl.BlockSpec((1, tm, wd), lambda bi, i: (bi, i, 0))
    mod_map = (lambda bi, i: (bi, 0, 0)) if gate.shape[0] > 1 else (lambda bi, i: (0, 0, 0))
    return pl.pallas_call(
        _merge_body,
        grid=(b, l // tm),
        in_specs=[
            row(d), row(LRU_W), row(LRU_W),
            pl.BlockSpec((1, MLA_HEADS // 2, tm, LANES), lambda bi, i: (bi, 0, i, 0)),
            row(RET_HEADS * RET_DV), row(RET_HEADS * RET_DV), row(3 * d),
            pl.BlockSpec((1, 1, d), mod_map),
            _resident(woa.shape), _resident(wob.shape), _resident(woc.shape), _resident(wout.shape),
        ],
        out_specs=row(d),
        out_shape=jax.ShapeDtypeStruct((b, l, d), F32),
        compiler_params=_params(("parallel", "parallel")),
    )(x, rec, gl, yb, rn, gr, gm, gate, woa, wob, woc, wout)


FFN_CHUNK = 1408


def _ffn_body(x_ref, g_ref, sh_ref, sc_ref, ga_ref, w1_ref, w3_ref, w2_ref, *rest, final):
    if final:
        gf_ref, o_ref = rest
    else:
        (o_ref,) = rest
    x = x_ref[0]
    hb = (_rms(x, g_ref[...]) * (1.0 + sc_ref[0]) + sh_ref[0]).astype(BF16)
    acc = None
    for c0 in range(0, D_FF, FFN_CHUNK):
        a = jnp.dot(hb, w1_ref[:, c0:c0 + FFN_CHUNK], preferred_element_type=F32)
        bgate = jnp.dot(hb, w3_ref[:, c0:c0 + FFN_CHUNK], preferred_element_type=F32)
        t = (jax.nn.silu(a) * bgate).astype(BF16)
        part = jnp.dot(t, w2_ref[c0:c0 + FFN_CHUNK, :], preferred_element_type=F32)
        acc = part if acc is None else acc + part
    y = x + ga_ref[0] * acc
    if final:
        y = _rms(y, gf_ref[...])
    o_ref[0] = y


def _ffn(x, g, shift, scale, gate, w1, w3, w2, g_final, tm):
    b, l, d = x.shape
    final = g_final is not None
    row = pl.BlockSpec((1, tm, d), lambda bi, i: (bi, i, 0))
    mod_map = (lambda bi, i: (bi, 0, 0)) if shift.shape[0] > 1 else (lambda bi, i: (0, 0, 0))
    mod = pl.BlockSpec((1, 1, d), mod_map)
    args = [x, g.reshape(1, d), shift, scale, gate, w1, w3, w2]
    specs = [row, _resident((1, d)), mod, mod, mod,
             _resident(w1.shape), _resident(w3.shape), _resident(w2.shape)]
    if final:
        args.append(g_final.reshape(1, d))
        specs.append(_resident((1, d)))
    return pl.pallas_call(
        functools.partial(_ffn_body, final=final),
        grid=(b, l // tm),
        in_specs=specs,
        out_specs=row,
        out_shape=jax.ShapeDtypeStruct((b, l, d), F32),
        compiler_params=_params(("parallel", "parallel")),
    )(*args)


def _swap_half(w):
    half = w.shape[-1] // 2
    return jnp.concatenate([-w[..., half:], w[..., :half]], axis=-1)


def _arrange_w_in(w_in):
    d = w_in.shape[0]
    o = 0
    parts = {}
    for name, n in (("lru_x", LRU_W), ("ckv", KV_LORA), ("kr", MLA_ROPE),
                    ("ret_k", RET_HEADS * RET_DK), ("ret_v", RET_HEADS * RET_DV),
                    ("cq", Q_LORA), ("ret_q", RET_HEADS * RET_DK), ("g_lru", LRU_W),
                    ("g_ret", RET_HEADS * RET_DV), ("gm", 3 * D_MODEL)):
        parts[name] = w_in[:, o:o + n]
        o += n
    z = lambda n: jnp.zeros((d, n), w_in.dtype)
    rope_a = jnp.concatenate([z(MLA_NOPE), parts["kr"], z(LANES - MLA_NOPE - MLA_ROPE)], axis=1)
    rope_b = jnp.concatenate([z(MLA_NOPE), _swap_half(parts["kr"]),
                              z(LANES - MLA_NOPE - MLA_ROPE)], axis=1)
    cols = [parts["lru_x"], parts["ckv"], rope_a, rope_b, parts["ret_k"], parts["ret_v"],
            parts["cq"], parts["ret_q"], parts["g_lru"], parts["g_ret"], parts["gm"]]
    return jnp.concatenate(cols, axis=1).astype(BF16)


def _arrange_mla(w_uq, w_ukv):
    nh = MLA_HEADS
    wq = w_uq.reshape(Q_LORA, nh, MLA_NOPE + MLA_ROPE)
    padq = jnp.zeros((Q_LORA, nh, LANES - MLA_NOPE - MLA_ROPE), w_uq.dtype)
    wq_arr = jnp.concatenate([wq, padq], axis=-1).reshape(Q_LORA, nh * LANES)
    wqs = jnp.concatenate([jnp.zeros((Q_LORA, nh, MLA_NOPE), w_uq.dtype),
                           _swap_half(wq[..., MLA_NOPE:]), padq], axis=-1)
    wqs_arr = wqs.reshape(Q_LORA, nh * LANES)
    wkv = w_ukv.reshape(KV_LORA, nh, MLA_NOPE + MLA_V)
    zk = jnp.zeros((KV_LORA, nh, LANES - MLA_NOPE), w_ukv.dtype)
    wk_arr = jnp.concatenate([wkv[..., :MLA_NOPE], zk], axis=-1).reshape(KV_LORA, nh * LANES)
    wv = wkv[..., MLA_NOPE:]
    zv = jnp.zeros_like(wv)
    even = (jnp.arange(nh) % 2 == 0)[None, :, None]
    wv_arr = jnp.concatenate([jnp.where(even, wv, zv), jnp.where(even, zv, wv)], axis=-1)
    wv_arr = wv_arr.reshape(KV_LORA, nh * LANES)
    return (wq_arr.astype(BF16), wqs_arr.astype(BF16),
            jnp.concatenate([wk_arr, wv_arr], axis=1).astype(BF16))


def _arrange_lru(wa, ba, wx, bx):
    def dense(wblk):
        eye = jnp.eye(LRU_BLOCKS, dtype=wblk.dtype)
        full = jnp.einsum("nde,nm->ndme", wblk, eye)
        return full.reshape(LRU_W, LRU_W)

    wg = jnp.stack([jnp.concatenate([dense(wa[d]), dense(wx[d])], axis=1) for d in range(2)])
    bg = jnp.stack([jnp.concatenate([ba[d], bx[d]])[None, :] for d in range(2)])
    return wg.astype(BF16), bg


def _mla_tables(n_tokens, rotate):
    scale = (MLA_NOPE + MLA_ROPE) ** -0.5 * math.log2(math.e)
    pad = LANES - MLA_NOPE - MLA_ROPE
    ones = jnp.ones((n_tokens, MLA_NOPE), F32)
    zeros = jnp.zeros((n_tokens, MLA_NOPE), F32)
    zpad = jnp.zeros((n_tokens, pad), F32)
    if rotate:
        rows = n_tokens // GRID_W
        rowi = jnp.repeat(jnp.arange(rows, dtype=F32), GRID_W)
        coli = jnp.tile(jnp.arange(GRID_W, dtype=F32), rows)
        n_freq = MLA_ROPE // 4
        inv = jnp.power(ROPE_BASE, -jnp.arange(n_freq, dtype=F32) / n_freq)
        ang = jnp.concatenate([rowi[:, None] * inv, coli[:, None] * inv], axis=-1)
        cos, sin = jnp.cos(ang), jnp.sin(ang)
    else:
        cos = jnp.ones((n_tokens, MLA_ROPE // 2), F32)
        sin = jnp.zeros((n_tokens, MLA_ROPE // 2), F32)
    cosq = jnp.concatenate([ones, cos, cos, zpad], axis=1) * scale
    sinq = jnp.concatenate([zeros, sin, sin, zpad], axis=1) * scale
    cosk = jnp.concatenate([zeros, cos, cos, zpad], axis=1)
    sink = jnp.concatenate([zeros, sin, sin, zpad], axis=1)
    return cosq, sinq, cosk, sink


def _ret_tables(n):
    theta = 1.0 / jnp.power(10000.0, jnp.linspace(0.0, 1.0, RET_DK // 2, dtype=F32))
    pos = jnp.arange(n, dtype=F32)
    ang = pos[:, None] * theta
    cos, sin = jnp.cos(ang), jnp.sin(ang)
    reps = LANES // RET_DK
    cos_t = jnp.tile(jnp.concatenate([cos, cos], axis=1), (1, reps))
    sin_t = jnp.tile(jnp.concatenate([-sin, sin], axis=1), (1, reps))
    return cos_t, sin_t


def _ret_decay_tables():
    c = RET_CHUNK
    h = jnp.arange(RET_HEADS, dtype=F32)
    lgs = jnp.stack([jnp.log1p(-jnp.exp2(-5.0 - h)), jnp.log1p(-jnp.exp2(-5.5 - h))], axis=1)
    pos = jnp.arange(c, dtype=F32)
    diff = pos[:, None] - pos[None, :]
    lg = lgs[:, :, None, None]
    inner = jnp.where(diff >= 0, jnp.exp(lg * jnp.maximum(diff, 0.0)), 0.0)
    msum = inner[:, 0] + jnp.swapaxes(inner[:, 1], -1, -2)
    lg1 = lgs[:, :, None]
    qd_f = jnp.exp(lg1 * (pos + 1.0))
    kd_f = jnp.exp(lg1 * (c - 1.0 - pos))
    qd = jnp.stack([qd_f[:, 0], qd_f[:, 1, ::-1]], axis=1)
    kd = jnp.stack([kd_f[:, 0], kd_f[:, 1, ::-1]], axis=1)
    lane_head = (jnp.arange(LANES) // RET_DK)
    e_of_h = jnp.arange(RET_HEADS) % 2
    mask = (lane_head[None, :] == e_of_h[:, None]).astype(F32)
    qdec = qd[..., None] * mask[:, None, None, :]
    kdec = kd[..., None] * mask[:, None, None, :]
    cd = jnp.broadcast_to(jnp.exp(lgs * c)[:, :, None, None], (RET_HEADS, 2, 1, RET_DV))
    pair = lambda t: t.reshape(RET_HEADS // 2, 2, *t.shape[1:])
    return pair(msum), pair(qdec), pair(kdec), pair(cd)


def _row_tile(n, pref):
    t = min(pref, n)
    while n % t:
        t //= 2
    return t


def kernel(x, c, ctx, c_ctx, w_mod, b_mod, g_mix, g_ffn, w_in, conv_w, conv_b, lru_wa, lru_ba,
           lru_wx, lru_bx, lru_lam, g_q, w_uq, g_kv, w_ukv, w_oa, w_ob, w_oc, w_out,
           w_ff1, w_ff3, w_ff2, g_final):
    b, l, d = x.shape
    lc = ctx.shape[1]
    depth = w_mod.shape[0]
    assert d == D_MODEL and l % LRU_TC == 0 and lc % LRU_TC == 0 and l % GRID_W == 0

    tm = _row_tile(l, 512)
    tmc = _row_tile(lc, 256)
    tq = _row_tile(l, 256)
    tqc = _row_tile(lc, 256)

    lat_tabs = _mla_tables(l, True)
    ctx_tabs = _mla_tables(lc, False)
    ret_tabs = _ret_tables(lc + l)
    dec = _ret_decay_tables()

    rows = -(-(b + 1) // SUBLANES) * SUBLANES
    cvec = jnp.zeros((rows, d), F32).at[:b].set(c).at[b].set(c_ctx)
    mod_all = _modulation(cvec, w_mod, b_mod)

    xc = ctx
    for li in range(depth):
        last = li == depth - 1
        mod = mod_all[li, :b].reshape(b, 1, 6, d)
        sh_a, sc_a, ga_a, sh_f, sc_f, ga_f = (mod[:, :, j] for j in range(6))
        mc = mod_all[li, b].reshape(1, 1, 6, d)
        mcs = [mc[:, :, j] for j in range(6)]

        w_arr = _arrange_w_in(w_in[li])
        wq_arr, wqs_arr, wkv_arr = _arrange_mla(w_uq[li], w_ukv[li])
        wg, bg = _arrange_lru(lru_wa[li], lru_ba[li], lru_wx[li], lru_bx[li])
        woa, wob, woc, wout = (w.astype(BF16) for w in (w_oa[li], w_ob[li], w_oc[li], w_out[li]))
        w1, w3, w2 = (w.astype(BF16) for w in (w_ff1[li], w_ff3[li], w_ff2[li]))

        z = _inproj(x, g_mix[li], sh_a, sc_a, w_arr, KV_GROUPS + Q_GROUPS, tm)
        u, ckv, ra, rb, rk, rv, cq, rq, g_lru, g_ret, gm = z
        if last:
            zc = _inproj(xc, g_mix[li], mcs[0], mcs[1], w_arr[:, :N_KV_ARR], KV_GROUPS, tmc)
            uc, ckvc, rac, _, rkc, rvc = zc
            cqc = rqc = None
        else:
            zc = _inproj(xc, g_mix[li], mcs[0], mcs[1], w_arr, KV_GROUPS + Q_GROUPS, tmc)
            uc, ckvc, rac, _, rkc, rvc, cqc, rqc, g_lru_c, g_ret_c, gm_c = zc
        need_ctx = not last

        rec, rec_c = _lru(u, uc, conv_w[li], conv_b[li], wg, bg, lru_lam[li], need_ctx)

        q, k, v = _mla_prep(cq, ckv, ra, rb, *lat_tabs, g_q[li], wq_arr, wqs_arr, g_kv[li],
                            wkv_arr, True, True, tm)
        qc, kc, vc = _mla_prep(cqc, ckvc, rac, None, *ctx_tabs, g_q[li], wq_arr, None, g_kv[li],
                               wkv_arr, False, need_ctx, tmc)
        yb = _attention(q, [(kc, vc), (k, v)], tq)

        rn, rn_c = _retention(rq, rk, rv, rqc, rkc, rvc, ret_tabs, dec, need_ctx)

        x = _merge(x, rec, g_lru, yb, rn, g_ret, gm, ga_a, woa, wob, woc, wout, tm)
        x = _ffn(x, g_ffn[li], sh_f, sc_f, ga_f, w1, w3, w2, g_final if last else None, tm)
        if need_ctx:
            yb_c = _attention(qc, [(kc, vc)], tqc)
            xc = _merge(xc, rec_c, g_lru_c, yb_c, rn_c, g_ret_c, gm_c, mcs[2], woa, wob, woc, wout, tmc)
            xc = _ffn(xc, g_ffn[li], mcs[3], mcs[4], mcs[5], w1, w3, w2, None, tmc)
    return x
```

```python
import functools
import math

import jax
import jax.numpy as jnp
from jax import lax
from jax.experimental import pallas as pl
from jax.experimental.pallas import tpu as pltpu

F32 = jnp.float32
BF16 = jnp.bfloat16

D_MODEL = 1024
EPS = 1e-6
GRID_W = 64
LRU_W = 512
LRU_BLOCKS = 8
LRU_BLOCK = LRU_W // LRU_BLOCKS
CONV_W = 4
CONV_PAD_L = 2
LRU_C = 8.0
MLA_HEADS = 8
MLA_NOPE = 64
MLA_ROPE = 32
MLA_V = 64
Q_LORA = 384
KV_LORA = 256
ROPE_BASE = 10000.0
RET_HEADS = 4
RET_DK = 64
RET_DV = 128
RET_CHUNK = 128
D_FF = 2816

LANES = 128
SUBLANES = 8
VMEM_LIMIT = 56 * 1024 * 1024

KV_GROUPS = (LRU_W, KV_LORA, LANES, LANES, RET_HEADS * RET_DK, RET_HEADS * RET_DV)
Q_GROUPS = (Q_LORA, RET_HEADS * RET_DK, LRU_W, RET_HEADS * RET_DV, 3 * D_MODEL)
N_KV_ARR = sum(KV_GROUPS)


def _params(sem, flags=None):
    return pltpu.CompilerParams(dimension_semantics=sem, vmem_limit_bytes=VMEM_LIMIT, flags=flags)


def _resident(shape):
    nd = len(shape)
    return pl.BlockSpec(shape, lambda *_: (0,) * nd, pipeline_mode=pl.Buffered(1))


def _rms(x, g):
    y = x * lax.rsqrt(jnp.mean(x * x, axis=-1, keepdims=True) + EPS)
    return y * g


def _mod_body(c_ref, w_ref, b_ref, o_ref):
    cv = c_ref[...]
    act = cv * jax.nn.sigmoid(cv)
    o_ref[0] = jnp.dot(act, w_ref[0], precision=lax.Precision.HIGHEST,
                       preferred_element_type=F32) + b_ref[0]


def _modulation(cvec, w_mod, b_mod):
    depth, d, n = w_mod.shape
    rows = cvec.shape[0]
    tn = 1536
    return pl.pallas_call(
        _mod_body,
        grid=(depth, n // tn),
        in_specs=[
            pl.BlockSpec((rows, d), lambda l, j: (0, 0)),
            pl.BlockSpec((1, d, tn), lambda l, j: (l, 0, j)),
            pl.BlockSpec((1, 1, tn), lambda l, j: (l, 0, j)),
        ],
        out_specs=pl.BlockSpec((1, rows, tn), lambda l, j: (l, 0, j)),
        out_shape=jax.ShapeDtypeStruct((depth, rows, n), F32),
        compiler_params=_params(("arbitrary", "arbitrary")),
    )(cvec, w_mod, b_mod.reshape(depth, 1, n))


def _inproj_body(x_ref, g_ref, sh_ref, sc_ref, w_ref, *out_refs, widths):
    x = x_ref[0]
    h = _rms(x, g_ref[...]) * (1.0 + sc_ref[0]) + sh_ref[0]
    hb = h.astype(BF16)
    off = 0
    for o_ref, wd in zip(out_refs, widths):
        o_ref[0] = jnp.dot(hb, w_ref[:, off:off + wd],
                           preferred_element_type=F32).astype(o_ref.dtype)
        off += wd


def _inproj(x, g, shift, scale, w, widths, tm):
    b, l, d = x.shape
    n = w.shape[1]
    assert n == sum(widths)
    mod_map = (lambda bi, i: (bi, 0, 0)) if shift.shape[0] > 1 else (lambda bi, i: (0, 0, 0))
    return pl.pallas_call(
        functools.partial(_inproj_body, widths=widths),
        grid=(b, l // tm),
        in_specs=[
            pl.BlockSpec((1, tm, d), lambda bi, i: (bi, i, 0)),
            _resident((1, d)),
            pl.BlockSpec((1, 1, d), mod_map),
            pl.BlockSpec((1, 1, d), mod_map),
            _resident((d, n)),
        ],
        out_specs=[pl.BlockSpec((1, tm, wd), lambda bi, i: (bi, i, 0)) for wd in widths],
        out_shape=[jax.ShapeDtypeStruct((b, l, wd), BF16) for wd in widths],
        compiler_params=_params(("parallel", "parallel")),
    )(x, g.reshape(1, d), shift, scale, w)


LRU_TC = 256
LRU_HALO = SUBLANES


def _lru_body(u_ref, uc_ref, cw_ref, cb_ref, wg_ref, bg_ref, lam_ref, *rest, need_ctx):
    if need_ctx:
        out_ref, outc_ref, upad, upadc, ucv, ucvc, hbuf = rest
    else:
        out_ref, upad, upadc, ucv, ucvc, hbuf = rest
        outc_ref = None
    l = u_ref.shape[1]
    lc = uc_ref.shape[1]
    w = u_ref.shape[2]
    tc = LRU_TC
    nt = tc // SUBLANES

    zeros_halo = jnp.zeros((LRU_HALO, w), F32)
    for pad_ref, cv_ref, src_ref, n in ((upad, ucv, u_ref, l), (upadc, ucvc, uc_ref, lc)):
        pad_ref[0:LRU_HALO, :] = zeros_halo
        pad_ref[LRU_HALO + n:2 * LRU_HALO + n, :] = zeros_halo

        def copy(i, carry, pad_ref=pad_ref, src_ref=src_ref):
            t0 = pl.multiple_of(i * tc, tc)
            pad_ref[pl.ds(LRU_HALO + t0, tc), :] = src_ref[0, pl.ds(t0, tc), :].astype(F32)
            return carry

        lax.fori_loop(0, n // tc, copy, 0)

        def conv(i, carry, pad_ref=pad_ref, cv_ref=cv_ref):
            t0 = pl.multiple_of(i * tc, tc)
            win = pad_ref[pl.ds(t0, tc + 2 * LRU_HALO), :]
            u = cb_ref[...]
            for k in range(CONV_W):
                sh = (CONV_PAD_L - k) % (tc + 2 * LRU_HALO)
                tap = win if sh == 0 else pltpu.roll(win, sh, 0)
                u = u + cw_ref[k:k + 1, :] * tap[LRU_HALO:LRU_HALO + tc]
            cv_ref[pl.ds(t0, tc), :] = u
            return carry

        lax.fori_loop(0, n // tc, conv, 0)

    row = lax.broadcasted_iota(jnp.int32, (nt, SUBLANES, w), 1)

    def chunk_scan(cv_ref, t0, d, h):
        u = cv_ref[pl.ds(t0, tc), :]
        g = jnp.dot(u.astype(BF16), wg_ref[d], preferred_element_type=F32) + bg_ref[d]
        r = jax.nn.sigmoid(g[:, :w])
        gi = jax.nn.sigmoid(g[:, w:])
        z = -lam_ref[d]
        sp = jnp.maximum(z, 0.0) + jnp.log1p(jnp.exp(-jnp.abs(z)))
        log_a = -LRU_C * r * sp
        a = jnp.exp(log_a)
        th = jnp.tanh(log_a)
        om = -2.0 * th / (1.0 - th)
        root = jnp.where(om > 0.0, om * lax.rsqrt(om), 0.0)
        bb = root * (gi * u)
        a = a.reshape(nt, SUBLANES, w)
        bb = bb.reshape(nt, SUBLANES, w)
        s = 1
        while s < SUBLANES:
            if d == 0:
                keep = row >= s
                shift = s
            else:
                keep = row < SUBLANES - s
                shift = SUBLANES - s
            a_s = jnp.where(keep, pltpu.roll(a, shift, 1), 1.0)
            b_s = jnp.where(keep, pltpu.roll(bb, shift, 1), 0.0)
            bb = a * b_s + bb
            a = a * a_s
            s *= 2
        order = range(nt) if d == 0 else range(nt - 1, -1, -1)
        for k in order:
            hk = a[k] * h + bb[k]
            hbuf[k * SUBLANES:(k + 1) * SUBLANES, :] = hk
            h = hk[SUBLANES - 1:SUBLANES] if d == 0 else hk[0:1]
        return h

    def run(cv_ref, n, d, h, write):
        nchunks = n // tc

        def body(i, h):
            ci = i if d == 0 else nchunks - 1 - i
            t0 = pl.multiple_of(ci * tc, tc)
            h = chunk_scan(cv_ref, t0, d, h)
            write(t0)
            return h

        return lax.fori_loop(0, nchunks, body, h)

    def write_accc(t0):
        upadc[pl.ds(t0, tc), :] = hbuf[...]

    def write_acc(t0):
        upad[pl.ds(t0, tc), :] = hbuf[...]

    def write_outc(t0):
        if outc_ref is not None:
            outc_ref[0, pl.ds(t0, tc), :] = (upadc[pl.ds(t0, tc), :] + hbuf[...]).astype(outc_ref.dtype)

    def write_out(t0):
        out_ref[0, pl.ds(t0, tc), :] = (upad[pl.ds(t0, tc), :] + hbuf[...]).astype(out_ref.dtype)

    h0 = jnp.zeros((1, w), F32)
    h = run(ucvc, lc, 0, h0, write_accc)
    run(ucv, l, 0, h, write_acc)
    h = run(ucvc, lc, 1, h0, write_outc)
    run(ucv, l, 1, h, write_out)


def _lru(u, uc, conv_w, conv_b, wg, bg, lam, need_ctx):
    b, l, w = u.shape
    lc = uc.shape[1]
    out_shape = [jax.ShapeDtypeStruct((b, l, w), BF16)]
    out_specs = [pl.BlockSpec((1, l, w), lambda bi: (bi, 0, 0))]
    if need_ctx:
        out_shape.append(jax.ShapeDtypeStruct((b, lc, w), BF16))
        out_specs.append(pl.BlockSpec((1, lc, w), lambda bi: (bi, 0, 0)))
    res = pl.pallas_call(
        functools.partial(_lru_body, need_ctx=need_ctx),
        grid=(b,),
        in_specs=[
            pl.BlockSpec((1, l, w), lambda bi: (bi, 0, 0)),
            pl.BlockSpec((1, lc, w), lambda bi: (bi, 0, 0)),
            _resident((CONV_W, w)),
            _resident((1, w)),
            _resident((2, w, 2 * w)),
            _resident((2, 1, 2 * w)),
            _resident((2, 1, w)),
        ],
        out_specs=out_specs,
        out_shape=out_shape,
        scratch_shapes=[
            pltpu.VMEM((l + 2 * LRU_HALO, w), F32),
            pltpu.VMEM((lc + 2 * LRU_HALO, w), F32),
            pltpu.VMEM((l, w), F32),
            pltpu.VMEM((lc, w), F32),
            pltpu.VMEM((LRU_TC, w), F32),
        ],
        compiler_params=_params(("parallel",)),
    )(u, uc, conv_w, conv_b.reshape(1, w), wg, bg, lam.reshape(2, 1, w))
    return (res[0], res[1]) if need_ctx else (res[0], None)


def _mla_prep_body(*refs, rotate, need_q):
    it = iter(refs)
    cq_ref = next(it) if need_q else None
    ckv_ref, ra_ref = next(it), next(it)
    rb_ref = next(it) if rotate else None
    if need_q:
        cosq_ref = next(it)
        sinq_ref = next(it) if rotate else None
    cosk_ref = next(it)
    sink_ref = next(it) if rotate else None
    if need_q:
        gq_ref, wq_ref = next(it), next(it)
        wqs_ref = next(it) if rotate else None
    gkv_ref, wkv_ref = next(it), next(it)
    q_out = next(it) if need_q else None
    k_out, v_out = next(it), next(it)
    nh = MLA_HEADS
    tm = ckv_ref.shape[1]

    if need_q:
        cqn = _rms(cq_ref[0].astype(F32), gq_ref[...]).astype(BF16)
        qm = jnp.dot(cqn, wq_ref[...], preferred_element_type=F32)
        cosq = cosq_ref[...]
        if rotate:
            qs = jnp.dot(cqn, wqs_ref[...], preferred_element_type=F32)
            sinq = sinq_ref[...]
        for h in range(nh):
            sl = slice(h * LANES, (h + 1) * LANES)
            qh = qm[:, sl] * cosq
            if rotate:
                qh = qh + qs[:, sl] * sinq
            q_out[0, h] = qh.astype(q_out.dtype)

    kvn = _rms(ckv_ref[0].astype(F32), gkv_ref[...]).astype(BF16)
    kv = jnp.dot(kvn, wkv_ref[...], preferred_element_type=F32)
    rope = ra_ref[0].astype(F32) * cosk_ref[...]
    if rotate:
        rope = rope + rb_ref[0].astype(F32) * sink_ref[...]
    lane = lax.broadcasted_iota(jnp.int32, (tm, LANES), 1)
    for h in range(nh):
        k_out[0, h] = (kv[:, h * LANES:(h + 1) * LANES] + rope).astype(k_out.dtype)
        vh = kv[:, (nh + h) * LANES:(nh + h + 1) * LANES]
        v_out[0, h] = jnp.where(lane == _sum_lane(h), 1.0, vh).astype(v_out.dtype)


def _sum_lane(h):
    return MLA_V if h % 2 == 0 else 0


def _mla_prep(cq, ckv, ra, rb, cosq, sinq, cosk, sink, gq, wq, wqs, gkv, wkv, rotate, need_q, tm):
    b, l, _ = ckv.shape
    nh = MLA_HEADS
    row = lambda wd: pl.BlockSpec((1, tm, wd), lambda bi, i: (bi, i, 0))
    tab = pl.BlockSpec((tm, LANES), lambda bi, i: (i, 0))
    args, specs = [], []

    def add(a, s):
        args.append(a)
        specs.append(s)

    if need_q:
        add(cq, row(Q_LORA))
    add(ckv, row(KV_LORA))
    add(ra, row(LANES))
    if rotate:
        add(rb, row(LANES))
    if need_q:
        add(cosq, tab)
        if rotate:
            add(sinq, tab)
    add(cosk, tab)
    if rotate:
        add(sink, tab)
    if need_q:
        add(gq.reshape(1, Q_LORA), _resident((1, Q_LORA)))
        add(wq, _resident(wq.shape))
        if rotate:
            add(wqs, _resident(wqs.shape))
    add(gkv.reshape(1, KV_LORA), _resident((1, KV_LORA)))
    add(wkv, _resident(wkv.shape))

    head = pl.BlockSpec((1, nh, tm, LANES), lambda bi, i: (bi, 0, i, 0))
    hshape = jax.ShapeDtypeStruct((b, nh, l, LANES), ATTN_DT)
    n_out = 3 if need_q else 2
    res = pl.pallas_call(
        functools.partial(_mla_prep_body, rotate=rotate, need_q=need_q),
        grid=(b, l // tm),
        in_specs=specs,
        out_specs=[head] * n_out,
        out_shape=[hshape] * n_out,
        compiler_params=_params(("parallel", "parallel")),
    )(*args)
    return res if need_q else (None, res[0], res[1])


ATTN_PAIRS_PER_STEP = 2
ATTN_DT = jnp.float8_e4m3fn
ATTN_P_SHIFT = 8.0


def _attn_body(q_ref, *refs, n_kv, tq):
    kv_refs = refs[:2 * n_kv]
    o_ref = refs[2 * n_kv]
    s_bufs = refs[2 * n_kv + 1:2 * n_kv + 3]
    p_bufs = refs[2 * n_kv + 3:2 * n_kv + 5]
    npair = ATTN_PAIRS_PER_STEP
    n_items = (q_ref.shape[2] // tq) * npair
    lks = [kv_refs[2 * j].shape[2] for j in range(n_kv)]
    offs = [sum(lks[:j]) for j in range(n_kv)]
    lane = lax.broadcasted_iota(jnp.int32, (tq, LANES), 1)
    nt_dims = (((1,), (1,)), ((), ()))

    s_bufs[1][...] = jnp.zeros(s_bufs[1].shape, s_bufs[1].dtype)
    p_bufs[0][...] = jnp.zeros(p_bufs[0].shape, p_bufs[0].dtype)

    def item(t):
        tile = lax.shift_right_logical(t, 1)
        pair = jnp.bitwise_and(t, npair - 1)
        return pl.multiple_of(tile * tq, tq), pair

    def stages(t, slot):
        s_a, s_b = s_bufs[slot], s_bufs[1 - slot]
        p_b, p_c = p_bufs[1 - slot], p_bufs[slot]

        row_a, pair_a = item(jnp.minimum(t, n_items - 1))
        for e in range(2):
            h = pair_a * 2 + e
            q = q_ref[0, h, pl.ds(row_a, tq), :]
            for j in range(n_kv):
                s_a[e, :, offs[j]:offs[j] + lks[j]] = lax.dot_general(
                    q, kv_refs[2 * j][0, h], nt_dims, preferred_element_type=F32)

        for e in range(2):
            s = s_b[e]
            m = s.max(axis=-1, keepdims=True)
            p_b[e] = jnp.exp2(s - (m - ATTN_P_SHIFT)).astype(p_b.dtype)

        row_c, pair_c = item(jnp.maximum(t - 2, 0))
        halves = []
        for e in range(2):
            h = pair_c * 2 + e
            o = None
            for j in range(n_kv):
                oj = jnp.dot(p_c[e, :, offs[j]:offs[j] + lks[j]], kv_refs[2 * j + 1][0, h],
                             preferred_element_type=F32)
                o = oj if o is None else o + oj
            one = _sum_lane(e)
            halves.append(o * (1.0 / o[:, one:one + 1]))
        blk = jnp.where(lane < MLA_V, halves[0], halves[1])
        o_ref[0, pair_c, pl.ds(row_c, tq), :] = blk.astype(o_ref.dtype)

    def body(i, carry):
        stages(2 * i, 0)
        stages(2 * i + 1, 1)
        return carry

    lax.fori_loop(0, (n_items + 2) // 2, body, 0)


def _attention(q, kvs, tq):
    b, nh, lq, _ = q.shape
    hs = 2 * ATTN_PAIRS_PER_STEP
    once = lambda n: pl.BlockSpec((1, hs, n, LANES), lambda bi, j: (bi, j, 0, 0),
                                  pipeline_mode=pl.Buffered(1))
    specs = [once(lq)]
    args = [q]
    lkt = 0
    for k, v in kvs:
        lk = k.shape[2]
        lkt += lk
        specs += [once(lk), once(lk)]
        args += [k, v]
    return pl.pallas_call(
        functools.partial(_attn_body, n_kv=len(kvs), tq=tq),
        grid=(b, nh // hs),
        in_specs=specs,
        out_specs=pl.BlockSpec((1, ATTN_PAIRS_PER_STEP, lq, LANES), lambda bi, j: (bi, j, 0, 0)),
        out_shape=jax.ShapeDtypeStruct((b, nh // 2, lq, LANES), BF16),
        scratch_shapes=[
            pltpu.VMEM((2, tq, lkt), F32),
            pltpu.VMEM((2, tq, lkt), F32),
            pltpu.VMEM((2, tq, lkt), ATTN_DT),
            pltpu.VMEM((2, tq, lkt), ATTN_DT),
        ],
        compiler_params=_params(("parallel", "parallel")),
    )(*args)


def _ret_body(*refs, need_ctx):
    it = iter(refs)
    q_ref, k_ref, v_ref = next(it), next(it), next(it)
    qc_ref = next(it) if need_ctx else None
    kc_ref, vc_ref = next(it), next(it)
    cos_ref, sin_ref = next(it), next(it)
    msum_ref, qdec_ref, kdec_ref, cd_ref = next(it), next(it), next(it), next(it)
    o_ref = next(it)
    oc_ref = next(it) if need_ctx else None
    qa, ka, va, u_s, r_s = (next(it) for _ in range(5))

    c = RET_CHUNK
    l = q_ref.shape[1]
    lc = kc_ref.shape[1]
    ncc = lc // c
    nt = (lc + l) // c
    kscale = RET_DK ** -0.5
    lane = lax.broadcasted_iota(jnp.int32, (c, LANES), 1)
    first_half = (lane % RET_DK) < (RET_DK // 2)

    def rotated(src_ref, t0, g0):
        x = src_ref[0, pl.ds(t0, c), :].astype(F32)
        sw = jnp.where(first_half, pltpu.roll(x, LANES - RET_DK // 2, 1),
                       pltpu.roll(x, RET_DK // 2, 1))
        return x * cos_ref[pl.ds(g0, c), :] + sw * sin_ref[pl.ds(g0, c), :]

    def stage(qsrc, ksrc, vsrc, n, base):
        def body(i, carry):
            t0 = pl.multiple_of(i * c, c)
            g0 = pl.multiple_of(base + i * c, c)
            if qsrc is not None:
                qa[pl.ds(g0, c), :] = rotated(qsrc, t0, g0)
            ka[pl.ds(g0, c), :] = rotated(ksrc, t0, g0) * kscale
            va[pl.ds(g0, c), :] = vsrc[0, pl.ds(t0, c), :]
            return carry

        lax.fori_loop(0, n // c, body, 0, unroll=2)

    stage(qc_ref, kc_ref, vc_ref, lc, 0)
    stage(q_ref, k_ref, v_ref, l, lc)

    tn_dims = (((0,), (0,)), ((), ()))
    nt_dims = (((1,), (1,)), ((), ()))

    def incr(g, carry):
        g0 = pl.multiple_of(g * c, c)
        kf = ka[pl.ds(g0, c), :]
        for e in range(2):
            v = va[pl.ds(g0, c), e * RET_DV:(e + 1) * RET_DV]
            kd = jnp.concatenate([kf * kdec_ref[0, e, 0], kf * kdec_ref[0, e, 1]], axis=1)
            u_s[g, e] = lax.dot_general(kd.astype(BF16), v, tn_dims, preferred_element_type=F32)
        return carry

    lax.fori_loop(0, nt, incr, 0, unroll=2)

    zero = tuple(jnp.zeros((LANES, RET_DV), F32) for _ in range(4))

    def recur(t, st):
        gs = (t, jnp.where(t < ncc, ncc - 1 - t, nt - 1 + ncc - t))
        new = []
        for d in range(2):
            rows = slice(d * LANES, (d + 1) * LANES)
            for e in range(2):
                r = st[2 * d + e]
                r_s[gs[d], e, rows, :] = r.astype(r_s.dtype)
                new.append(cd_ref[0, e, d] * r + u_s[gs[d], e, rows, :])
        return tuple(new)

    lax.fori_loop(0, nt, recur, zero, unroll=2)

    def emit(dst_ref, gbase):
        def body(i, carry):
            t0 = pl.multiple_of(i * c, c)
            g = gbase + i
            g0 = pl.multiple_of(g * c, c)
            qf = qa[pl.ds(g0, c), :]
            kb = ka[pl.ds(g0, c), :].astype(BF16)
            parts = []
            for e in range(2):
                v = va[pl.ds(g0, c), e * RET_DV:(e + 1) * RET_DV]
                qm = jnp.where((lane // RET_DK) == e, qf, 0.0).astype(BF16)
                s = lax.dot_general(qm, kb, nt_dims, preferred_element_type=F32)
                o = jnp.dot((s * msum_ref[0, e]).astype(BF16), v, preferred_element_type=F32)
                qd = jnp.concatenate([qf * qdec_ref[0, e, 0], qf * qdec_ref[0, e, 1]], axis=1)
                o = o + jnp.dot(qd.astype(BF16), r_s[g, e], preferred_element_type=F32)
                mu = jnp.mean(o, axis=-1, keepdims=True)
                oc = o - mu
                var = jnp.mean(oc * oc, axis=-1, keepdims=True)
                parts.append(oc * lax.rsqrt(var + EPS))
            dst_ref[0, pl.ds(t0, c), :] = jnp.concatenate(parts, axis=1).astype(dst_ref.dtype)
            return carry

        return body

    if need_ctx:
        lax.fori_loop(0, ncc, emit(oc_ref, 0), 0, unroll=2)
    lax.fori_loop(0, nt - ncc, emit(o_ref, ncc), 0, unroll=4)


def _retention(q, k, v, qc, kc, vc, tabs, dec, need_ctx):
    b, l, _ = q.shape
    lc = kc.shape[1]
    lt = lc + l
    c = RET_CHUNK
    nt = lt // c
    npair = RET_HEADS // 2
    seq = lambda n, wd: pl.BlockSpec((1, n, wd), lambda bi, p: (bi, 0, p))
    args = [q, k, v]
    specs = [seq(l, LANES), seq(l, LANES), seq(l, 2 * RET_DV)]
    if need_ctx:
        args.append(qc)
        specs.append(seq(lc, LANES))
    args += [kc, vc]
    specs += [seq(lc, LANES), seq(lc, 2 * RET_DV)]
    for t in tabs:
        args.append(t)
        specs.append(_resident((lt, LANES)))
    msum, qdec, kdec, cd = dec
    args += [msum, qdec, kdec, cd]
    specs += [
        pl.BlockSpec((1, 2, c, c), lambda bi, p: (p, 0, 0, 0)),
        pl.BlockSpec((1, 2, 2, c, LANES), lambda bi, p: (p, 0, 0, 0, 0)),
        pl.BlockSpec((1, 2, 2, c, LANES), lambda bi, p: (p, 0, 0, 0, 0)),
        pl.BlockSpec((1, 2, 2, 1, RET_DV), lambda bi, p: (p, 0, 0, 0, 0)),
    ]
    out_shape = [jax.ShapeDtypeStruct((b, l, RET_HEADS * RET_DV), BF16)]
    out_specs = [seq(l, 2 * RET_DV)]
    if need_ctx:
        out_shape.append(jax.ShapeDtypeStruct((b, lc, RET_HEADS * RET_DV), BF16))
        out_specs.append(seq(lc, 2 * RET_DV))
    res = pl.pallas_call(
        functools.partial(_ret_body, need_ctx=need_ctx),
        grid=(b, npair),
        in_specs=specs,
        out_specs=out_specs,
        out_shape=out_shape,
        scratch_shapes=[
            pltpu.VMEM((lt, LANES), F32),
            pltpu.VMEM((lt, LANES), F32),
            pltpu.VMEM((lt, 2 * RET_DV), BF16),
            pltpu.VMEM((nt, 2, 2 * LANES, RET_DV), F32),
            pltpu.VMEM((nt, 2, 2 * LANES, RET_DV), BF16),
        ],
        compiler_params=_params(("parallel", "parallel")),
    )(*args)
    return (res[0], res[1]) if need_ctx else (res[0], None)


def _merge_body(x_ref, rec_ref, gl_ref, yb_ref, rn_ref, gr_ref, gm_ref, ga_ref,
                woa_ref, wob_ref, woc_ref, wout_ref, o_ref):
    d = x_ref.shape[2]
    ya = (jax.nn.gelu(gl_ref[0].astype(F32)) * rec_ref[0].astype(F32)).astype(BF16)
    yc = (jax.nn.silu(gr_ref[0].astype(F32)) * rn_ref[0].astype(F32)).astype(BF16)
    yb = jnp.concatenate([yb_ref[0, p] for p in range(yb_ref.shape[1])], axis=1)
    gm = gm_ref[0]
    m = jax.nn.sigmoid(gm[:, :d].astype(F32)) * jnp.dot(ya, woa_ref[...], preferred_element_type=F32)
    m = m + jax.nn.sigmoid(gm[:, d:2 * d].astype(F32)) * jnp.dot(
        yb, wob_ref[...], preferred_element_type=F32)
    m = m + jax.nn.sigmoid(gm[:, 2 * d:].astype(F32)) * jnp.dot(
        yc, woc_ref[...], preferred_element_type=F32)
    y = jnp.dot(m.astype(BF16), wout_ref[...], preferred_element_type=F32)
    o_ref[0] = x_ref[0] + ga_ref[0] * y


def _merge(x, rec, gl, yb, rn, gr, gm, gate, woa, wob, woc, wout, tm):
    b, l, d = x.shape
    row = lambda wd: pl.BlockSpec((1, tm, wd), lambda bi, i: (bi, i, 0))
    mod_map = (lambda bi, i: (bi, 0, 0)) if gate.shape[0] > 1 else (lambda bi, i: (0, 0, 0))
    return pl.pallas_call(
        _merge_body,
        grid=(b, l // tm),
        in_specs=[
            row(d), row(LRU_W), row(LRU_W),
            pl.BlockSpec((1, MLA_HEADS // 2, tm, LANES), lambda bi, i: (bi, 0, i, 0)),
            row(RET_HEADS * RET_DV), row(RET_HEADS * RET_DV), row(3 * d),
            pl.BlockSpec((1, 1, d), mod_map),
            _resident(woa.shape), _resident(wob.shape), _resident(woc.shape), _resident(wout.shape),
        ],
        out_specs=row(d),
        out_shape=jax.ShapeDtypeStruct((b, l, d), F32),
        compiler_params=_params(("parallel", "parallel")),
    )(x, rec, gl, yb, rn, gr, gm, gate, woa, wob, woc, wout)


FFN_CHUNK = 1408


def _ffn_body(x_ref, g_ref, sh_ref, sc_ref, ga_ref, w1_ref, w3_ref, w2_ref, *rest, final):
    if final:
        gf_ref, o_ref = rest
    else:
        (o_ref,) = rest
    x = x_ref[0]
    hb = (_rms(x, g_ref[...]) * (1.0 + sc_ref[0]) + sh_ref[0]).astype(BF16)
    acc = None
    for c0 in range(0, D_FF, FFN_CHUNK):
        a = jnp.dot(hb, w1_ref[:, c0:c0 + FFN_CHUNK], preferred_element_type=F32)
        bgate = jnp.dot(hb, w3_ref[:, c0:c0 + FFN_CHUNK], preferred_element_type=F32)
        t = (jax.nn.silu(a) * bgate).astype(BF16)
        part = jnp.dot(t, w2_ref[c0:c0 + FFN_CHUNK, :], preferred_element_type=F32)
        acc = part if acc is None else acc + part
    y = x + ga_ref[0] * acc
    if final:
        y = _rms(y, gf_ref[...])
    o_ref[0] = y


def _ffn(x, g, shift, scale, gate, w1, w3, w2, g_final, tm):
    b, l, d = x.shape
    final = g_final is not None
    row = pl.BlockSpec((1, tm, d), lambda bi, i: (bi, i, 0))
    mod_map = (lambda bi, i: (bi, 0, 0)) if shift.shape[0] > 1 else (lambda bi, i: (0, 0, 0))
    mod = pl.BlockSpec((1, 1, d), mod_map)
    args = [x, g.reshape(1, d), shift, scale, gate, w1, w3, w2]
    specs = [row, _resident((1, d)), mod, mod, mod,
             _resident(w1.shape), _resident(w3.shape), _resident(w2.shape)]
    if final:
        args.append(g_final.reshape(1, d))
        specs.append(_resident((1, d)))
    return pl.pallas_call(
        functools.partial(_ffn_body, final=final),
        grid=(b, l // tm),
        in_specs=specs,
        out_specs=row,
        out_shape=jax.ShapeDtypeStruct((b, l, d), F32),
        compiler_params=_params(("parallel", "parallel")),
    )(*args)


def _swap_half(w):
    half = w.shape[-1] // 2
    return jnp.concatenate([-w[..., half:], w[..., :half]], axis=-1)


def _arrange_w_in(w_in):
    d = w_in.shape[0]
    o = 0
    parts = {}
    for name, n in (("lru_x", LRU_W), ("ckv", KV_LORA), ("kr", MLA_ROPE),
                    ("ret_k", RET_HEADS * RET_DK), ("ret_v", RET_HEADS * RET_DV),
                    ("cq", Q_LORA), ("ret_q", RET_HEADS * RET_DK), ("g_lru", LRU_W),
                    ("g_ret", RET_HEADS * RET_DV), ("gm", 3 * D_MODEL)):
        parts[name] = w_in[:, o:o + n]
        o += n
    z = lambda n: jnp.zeros((d, n), w_in.dtype)
    rope_a = jnp.concatenate([z(MLA_NOPE), parts["kr"], z(LANES - MLA_NOPE - MLA_ROPE)], axis=1)
    rope_b = jnp.concatenate([z(MLA_NOPE), _swap_half(parts["kr"]),
                              z(LANES - MLA_NOPE - MLA_ROPE)], axis=1)
    cols = [parts["lru_x"], parts["ckv"], rope_a, rope_b, parts["ret_k"], parts["ret_v"],
            parts["cq"], parts["ret_q"], parts["g_lru"], parts["g_ret"], parts["gm"]]
    return jnp.concatenate(cols, axis=1).astype(BF16)


def _arrange_mla(w_uq, w_ukv):
    nh = MLA_HEADS
    wq = w_uq.reshape(Q_LORA, nh, MLA_NOPE + MLA_ROPE)
    padq = jnp.zeros((Q_LORA, nh, LANES - MLA_NOPE - MLA_ROPE), w_uq.dtype)
    wq_arr = jnp.concatenate([wq, padq], axis=-1).reshape(Q_LORA, nh * LANES)
    wqs = jnp.concatenate([jnp.zeros((Q_LORA, nh, MLA_NOPE), w_uq.dtype),
                           _swap_half(wq[..., MLA_NOPE:]), padq], axis=-1)
    wqs_arr = wqs.reshape(Q_LORA, nh * LANES)
    wkv = w_ukv.reshape(KV_LORA, nh, MLA_NOPE + MLA_V)
    zk = jnp.zeros((KV_LORA, nh, LANES - MLA_NOPE), w_ukv.dtype)
    wk_arr = jnp.concatenate([wkv[..., :MLA_NOPE], zk], axis=-1).reshape(KV_LORA, nh * LANES)
    wv = wkv[..., MLA_NOPE:]
    zv = jnp.zeros_like(wv)
    even = (jnp.arange(nh) % 2 == 0)[None, :, None]
    wv_arr = jnp.concatenate([jnp.where(even, wv, zv), jnp.where(even, zv, wv)], axis=-1)
    wv_arr = wv_arr.reshape(KV_LORA, nh * LANES)
    return (wq_arr.astype(BF16), wqs_arr.astype(BF16),
            jnp.concatenate([wk_arr, wv_arr], axis=1).astype(BF16))


def _arrange_lru(wa, ba, wx, bx):
    def dense(wblk):
        eye = jnp.eye(LRU_BLOCKS, dtype=wblk.dtype)
        full = jnp.einsum("nde,nm->ndme", wblk, eye)
        return full.reshape(LRU_W, LRU_W)

    wg = jnp.stack([jnp.concatenate([dense(wa[d]), dense(wx[d])], axis=1) for d in range(2)])
    bg = jnp.stack([jnp.concatenate([ba[d], bx[d]])[None, :] for d in range(2)])
    return wg.astype(BF16), bg


def _mla_tables(n_tokens, rotate):
    scale = (MLA_NOPE + MLA_ROPE) ** -0.5 * math.log2(math.e)
    pad = LANES - MLA_NOPE - MLA_ROPE
    ones = jnp.ones((n_tokens, MLA_NOPE), F32)
    zeros = jnp.zeros((n_tokens, MLA_NOPE), F32)
    zpad = jnp.zeros((n_tokens, pad), F32)
    if rotate:
        rows = n_tokens // GRID_W
        rowi = jnp.repeat(jnp.arange(rows, dtype=F32), GRID_W)
        coli = jnp.tile(jnp.arange(GRID_W, dtype=F32), rows)
        n_freq = MLA_ROPE // 4
        inv = jnp.power(ROPE_BASE, -jnp.arange(n_freq, dtype=F32) / n_freq)
        ang = jnp.concatenate([rowi[:, None] * inv, coli[:, None] * inv], axis=-1)
        cos, sin = jnp.cos(ang), jnp.sin(ang)
    else:
        cos = jnp.ones((n_tokens, MLA_ROPE // 2), F32)
        sin = jnp.zeros((n_tokens, MLA_ROPE // 2), F32)
    cosq = jnp.concatenate([ones, cos, cos, zpad], axis=1) * scale
    sinq = jnp.concatenate([zeros, sin, sin, zpad], axis=1) * scale
    cosk = jnp.concatenate([zeros, cos, cos, zpad], axis=1)
    sink = jnp.concatenate([zeros, sin, sin, zpad], axis=1)
    return cosq, sinq, cosk, sink


def _ret_tables(n):
    theta = 1.0 / jnp.power(10000.0, jnp.linspace(0.0, 1.0, RET_DK // 2, dtype=F32))
    pos = jnp.arange(n, dtype=F32)
    ang = pos[:, None] * theta
    cos, sin = jnp.cos(ang), jnp.sin(ang)
    reps = LANES // RET_DK
    cos_t = jnp.tile(jnp.concatenate([cos, cos], axis=1), (1, reps))
    sin_t = jnp.tile(jnp.concatenate([-sin, sin], axis=1), (1, reps))
    return cos_t, sin_t


def _ret_decay_tables():
    c = RET_CHUNK
    h = jnp.arange(RET_HEADS, dtype=F32)
    lgs = jnp.stack([jnp.log1p(-jnp.exp2(-5.0 - h)), jnp.log1p(-jnp.exp2(-5.5 - h))], axis=1)
    pos = jnp.arange(c, dtype=F32)
    diff = pos[:, None] - pos[None, :]
    lg = lgs[:, :, None, None]
    inner = jnp.where(diff >= 0, jnp.exp(lg * jnp.maximum(diff, 0.0)), 0.0)
    msum = inner[:, 0] + jnp.swapaxes(inner[:, 1], -1, -2)
    lg1 = lgs[:, :, None]
    qd_f = jnp.exp(lg1 * (pos + 1.0))
    kd_f = jnp.exp(lg1 * (c - 1.0 - pos))
    qd = jnp.stack([qd_f[:, 0], qd_f[:, 1, ::-1]], axis=1)
    kd = jnp.stack([kd_f[:, 0], kd_f[:, 1, ::-1]], axis=1)
    lane_head = (jnp.arange(LANES) // RET_DK)
    e_of_h = jnp.arange(RET_HEADS) % 2
    mask = (lane_head[None, :] == e_of_h[:, None]).astype(F32)
    qdec = qd[..., None] * mask[:, None, None, :]
    kdec = kd[..., None] * mask[:, None, None, :]
    cd = jnp.broadcast_to(jnp.exp(lgs * c)[:, :, None, None], (RET_HEADS, 2, 1, RET_DV))
    pair = lambda t: t.reshape(RET_HEADS // 2, 2, *t.shape[1:])
    return pair(msum), pair(qdec), pair(kdec), pair(cd)


def _row_tile(n, pref):
    t = min(pref, n)
    while n % t:
        t //= 2
    return t


def kernel(x, c, ctx, c_ctx, w_mod, b_mod, g_mix, g_ffn, w_in, conv_w, conv_b, lru_wa, lru_ba,
           lru_wx, lru_bx, lru_lam, g_q, w_uq, g_kv, w_ukv, w_oa, w_ob, w_oc, w_out,
           w_ff1, w_ff3, w_ff2, g_final):
    b, l, d = x.shape
    lc = ctx.shape[1]
    depth = w_mod.shape[0]
    assert d == D_MODEL and l % LRU_TC == 0 and lc % LRU_TC == 0 and l % GRID_W == 0

    tm = _row_tile(l, 512)
    tmc = _row_tile(lc, 256)
    tq = _row_tile(l, 256)
    tqc = _row_tile(lc, 256)

    lat_tabs = _mla_tables(l, True)
    ctx_tabs = _mla_tables(lc, False)
    ret_tabs = _ret_tables(lc + l)
    dec = _ret_decay_tables()

    rows = -(-(b + 1) // SUBLANES) * SUBLANES
    cvec = jnp.zeros((rows, d), F32).at[:b].set(c).at[b].set(c_ctx)
    mod_all = _modulation(cvec, w_mod, b_mod)

    xc = ctx
    for li in range(depth):
        last = li == depth - 1
        mod = mod_all[li, :b].reshape(b, 1, 6, d)
        sh_a, sc_a, ga_a, sh_f, sc_f, ga_f = (mod[:, :, j] for j in range(6))
        mc = mod_all[li, b].reshape(1, 1, 6, d)
        mcs = [mc[:, :, j] for j in range(6)]

        w_arr = _arrange_w_in(w_in[li])
        wq_arr, wqs_arr, wkv_arr = _arrange_mla(w_uq[li], w_ukv[li])
        wg, bg = _arrange_lru(lru_wa[li], lru_ba[li], lru_wx[li], lru_bx[li])
        woa, wob, woc, wout = (w.astype(BF16) for w in (w_oa[li], w_ob[li], w_oc[li], w_out[li]))
        w1, w3, w2 = (w.astype(BF16) for w in (w_ff1[li], w_ff3[li], w_ff2[li]))

        z = _inproj(x, g_mix[li], sh_a, sc_a, w_arr, KV_GROUPS + Q_GROUPS, tm)
        u, ckv, ra, rb, rk, rv, cq, rq, g_lru, g_ret, gm = z
        if last:
            zc = _inproj(xc, g_mix[li], mcs[0], mcs[1], w_arr[:, :N_KV_ARR], KV_GROUPS, tmc)
            uc, ckvc, rac, _, rkc, rvc = zc
            cqc = rqc = None
        else:
            zc = _inproj(xc, g_mix[li], mcs[0], mcs[1], w_arr, KV_GROUPS + Q_GROUPS, tmc)
            uc, ckvc, rac, _, rkc, rvc, cqc, rqc, g_lru_c, g_ret_c, gm_c = zc
        need_ctx = not last

        rec, rec_c = _lru(u, uc, conv_w[li], conv_b[li], wg, bg, lru_lam[li], need_ctx)

        q, k, v = _mla_prep(cq, ckv, ra, rb, *lat_tabs, g_q[li], wq_arr, wqs_arr, g_kv[li],
                            wkv_arr, True, True, tm)
        qc, kc, vc = _mla_prep(cqc, ckvc, rac, None, *ctx_tabs, g_q[li], wq_arr, None, g_kv[li],
                               wkv_arr, False, need_ctx, tmc)
        yb = _attention(q, [(kc, vc), (k, v)], tq)

        rn, rn_c = _retention(rq, rk, rv, rqc, rkc, rvc, ret_tabs, dec, need_ctx)

        x = _merge(x, rec, g_lru, yb, rn, g_ret, gm, ga_a, woa, wob, woc, wout, tm)
        x = _ffn(x, g_ffn[li], sh_f, sc_f, ga_f, w1, w3, w2, g_final if last else None, tm)
        if need_ctx:
            yb_c = _attention(qc, [(kc, vc)], tqc)
            xc = _merge(xc, rec_c, g_lru_c, yb_c, rn_c, g_ret_c, gm_c, mcs[2], woa, wob, woc, wout, tmc)
            xc = _ffn(xc, g_ffn[li], mcs[3], mcs[4], mcs[5], w1, w3, w2, None, tmc)
    return x
```

```python
import functools
import math

import jax
import jax.numpy as jnp
from jax import lax
from jax.experimental import pallas as pl
from jax.experimental.pallas import tpu as pltpu

F32 = jnp.float32
BF16 = jnp.bfloat16

D_MODEL = 1024
EPS = 1e-6
GRID_W = 64
LRU_W = 512
LRU_BLOCKS = 8
LRU_BLOCK = LRU_W // LRU_BLOCKS
CONV_W = 4
CONV_PAD_L = 2
LRU_C = 8.0
MLA_HEADS = 8
MLA_NOPE = 64
MLA_ROPE = 32
MLA_V = 64
Q_LORA = 384
KV_LORA = 256
ROPE_BASE = 10000.0
RET_HEADS = 4
RET_DK = 64
RET_DV = 128
RET_CHUNK = 128
D_FF = 2816

LANES = 128
SUBLANES = 8
VMEM_LIMIT = 56 * 1024 * 1024

KV_GROUPS = (LRU_W, KV_LORA, LANES, LANES, RET_HEADS * RET_DK, RET_HEADS * RET_DV)
Q_GROUPS = (Q_LORA, RET_HEADS * RET_DK, LRU_W, RET_HEADS * RET_DV, 3 * D_MODEL)
N_KV_ARR = sum(KV_GROUPS)


def _params(sem, flags=None):
    return pltpu.CompilerParams(dimension_semantics=sem, vmem_limit_bytes=VMEM_LIMIT, flags=flags)


def _resident(shape):
    nd = len(shape)
    return pl.BlockSpec(shape, lambda *_: (0,) * nd, pipeline_mode=pl.Buffered(1))


def _rms(x, g):
    y = x * lax.rsqrt(jnp.mean(x * x, axis=-1, keepdims=True) + EPS)
    return y * g


def _mod_body(c_ref, w_ref, b_ref, o_ref):
    cv = c_ref[...]
    act = cv * jax.nn.sigmoid(cv)
    o_ref[0] = jnp.dot(act, w_ref[0], precision=lax.Precision.HIGHEST,
                       preferred_element_type=F32) + b_ref[0]


def _modulation(cvec, w_mod, b_mod):
    depth, d, n = w_mod.shape
    rows = cvec.shape[0]
    tn = 1536
    return pl.pallas_call(
        _mod_body,
        grid=(depth, n // tn),
        in_specs=[
            pl.BlockSpec((rows, d), lambda l, j: (0, 0)),
            pl.BlockSpec((1, d, tn), lambda l, j: (l, 0, j)),
            pl.BlockSpec((1, 1, tn), lambda l, j: (l, 0, j)),
        ],
        out_specs=pl.BlockSpec((1, rows, tn), lambda l, j: (l, 0, j)),
        out_shape=jax.ShapeDtypeStruct((depth, rows, n), F32),
        compiler_params=_params(("arbitrary", "arbitrary")),
    )(cvec, w_mod, b_mod.reshape(depth, 1, n))


def _inproj_body(x_ref, g_ref, sh_ref, sc_ref, w_ref, *out_refs, widths):
    x = x_ref[0]
    h = _rms(x, g_ref[...]) * (1.0 + sc_ref[0]) + sh_ref[0]
    hb = h.astype(BF16)
    off = 0
    for o_ref, wd in zip(out_refs, widths):
        o_ref[0] = jnp.dot(hb, w_ref[:, off:off + wd],
                           preferred_element_type=F32).astype(o_ref.dtype)
        off += wd


def _inproj(x, g, shift, scale, w, widths, tm):
    b, l, d = x.shape
    n = w.shape[1]
    assert n == sum(widths)
    mod_map = (lambda bi, i: (bi, 0, 0)) if shift.shape[0] > 1 else (lambda bi, i: (0, 0, 0))
    return pl.pallas_call(
        functools.partial(_inproj_body, widths=widths),
        grid=(b, l // tm),
        in_specs=[
            pl.BlockSpec((1, tm, d), lambda bi, i: (bi, i, 0)),
            _resident((1, d)),
            pl.BlockSpec((1, 1, d), mod_map),
            pl.BlockSpec((1, 1, d), mod_map),
            _resident((d, n)),
        ],
        out_specs=[pl.BlockSpec((1, tm, wd), lambda bi, i: (bi, i, 0)) for wd in widths],
        out_shape=[jax.ShapeDtypeStruct((b, l, wd), BF16) for wd in widths],
        compiler_params=_params(("parallel", "parallel")),
    )(x, g.reshape(1, d), shift, scale, w)


LRU_TC = 256
LRU_HALO = SUBLANES


def _lru_body(u_ref, uc_ref, cw_ref, cb_ref, wg_ref, bg_ref, lam_ref, *rest, need_ctx):
    if need_ctx:
        out_ref, outc_ref, upad, upadc, ucv, ucvc, hbuf = rest
    else:
        out_ref, upad, upadc, ucv, ucvc, hbuf = rest
        outc_ref = None
    l = u_ref.shape[1]
    lc = uc_ref.shape[1]
    w = u_ref.shape[2]
    tc = LRU_TC
    nt = tc // SUBLANES

    zeros_halo = jnp.zeros((LRU_HALO, w), F32)
    for pad_ref, cv_ref, src_ref, n in ((upad, ucv, u_ref, l), (upadc, ucvc, uc_ref, lc)):
        pad_ref[0:LRU_HALO, :] = zeros_halo
        pad_ref[LRU_HALO + n:2 * LRU_HALO + n, :] = zeros_halo

        def copy(i, carry, pad_ref=pad_ref, src_ref=src_ref):
            t0 = pl.multiple_of(i * tc, tc)
            pad_ref[pl.ds(LRU_HALO + t0, tc), :] = src_ref[0, pl.ds(t0, tc), :].astype(F32)
            return carry

        lax.fori_loop(0, n // tc, copy, 0)

        def conv(i, carry, pad_ref=pad_ref, cv_ref=cv_ref):
            t0 = pl.multiple_of(i * tc, tc)
            win = pad_ref[pl.ds(t0, tc + 2 * LRU_HALO), :]
            u = cb_ref[...]
            for k in range(CONV_W):
                sh = (CONV_PAD_L - k) % (tc + 2 * LRU_HALO)
                tap = win if sh == 0 else pltpu.roll(win, sh, 0)
                u = u + cw_ref[k:k + 1, :] * tap[LRU_HALO:LRU_HALO + tc]
            cv_ref[pl.ds(t0, tc), :] = u
            return carry

        lax.fori_loop(0, n // tc, conv, 0)

    row = lax.broadcasted_iota(jnp.int32, (nt, SUBLANES, w), 1)

    def chunk_scan(cv_ref, t0, d, h):
        u = cv_ref[pl.ds(t0, tc), :]
        g = jnp.dot(u.astype(BF16), wg_ref[d], preferred_element_type=F32) + bg_ref[d]
        r = jax.nn.sigmoid(g[:, :w])
        gi = jax.nn.sigmoid(g[:, w:])
        z = -lam_ref[d]
        sp = jnp.maximum(z, 0.0) + jnp.log1p(jnp.exp(-jnp.abs(z)))
        log_a = r * (-LRU_C * sp)
        a = jnp.exp2(r * (-LRU_C * math.log2(math.e) * sp))
        th = jnp.tanh(log_a)
        om = -2.0 * th / (1.0 - th)
        root = jnp.where(om > 0.0, om * lax.rsqrt(om), 0.0)
        bb = root * (gi * u)
        a = a.reshape(nt, SUBLANES, w)
        bb = bb.reshape(nt, SUBLANES, w)
        s = 1
        while s < SUBLANES:
            if d == 0:
                keep = row >= s
                shift = s
            else:
                keep = row < SUBLANES - s
                shift = SUBLANES - s
            a_s = jnp.where(keep, pltpu.roll(a, shift, 1), 1.0)
            b_s = jnp.where(keep, pltpu.roll(bb, shift, 1), 0.0)
            bb = a * b_s + bb
            a = a * a_s
            s *= 2
        order = range(nt) if d == 0 else range(nt - 1, -1, -1)
        for k in order:
            hk = a[k] * h + bb[k]
            hbuf[k * SUBLANES:(k + 1) * SUBLANES, :] = hk
            h = hk[SUBLANES - 1:SUBLANES] if d == 0 else hk[0:1]
        return h

    def run(cv_ref, n, d, h, write):
        nchunks = n // tc

        def body(i, h):
            ci = i if d == 0 else nchunks - 1 - i
            t0 = pl.multiple_of(ci * tc, tc)
            h = chunk_scan(cv_ref, t0, d, h)
            write(t0)
            return h

        return lax.fori_loop(0, nchunks, body, h)

    def write_accc(t0):
        upadc[pl.ds(t0, tc), :] = hbuf[...]

    def write_acc(t0):
        upad[pl.ds(t0, tc), :] = hbuf[...]

    def write_outc(t0):
        if outc_ref is not None:
            outc_ref[0, pl.ds(t0, tc), :] = (upadc[pl.ds(t0, tc), :] + hbuf[...]).astype(outc_ref.dtype)

    def write_out(t0):
        out_ref[0, pl.ds(t0, tc), :] = (upad[pl.ds(t0, tc), :] + hbuf[...]).astype(out_ref.dtype)

    h0 = jnp.zeros((1, w), F32)
    h = run(ucvc, lc, 0, h0, write_accc)
    run(ucv, l, 0, h, write_acc)
    h = run(ucvc, lc, 1, h0, write_outc)
    run(ucv, l, 1, h, write_out)


def _lru(u, uc, conv_w, conv_b, wg, bg, lam, need_ctx):
    b, l, w = u.shape
    lc = uc.shape[1]
    out_shape = [jax.ShapeDtypeStruct((b, l, w), BF16)]
    out_specs = [pl.BlockSpec((1, l, w), lambda bi: (bi, 0, 0))]
    if need_ctx:
        out_shape.append(jax.ShapeDtypeStruct((b, lc, w), BF16))
        out_specs.append(pl.BlockSpec((1, lc, w), lambda bi: (bi, 0, 0)))
    res = pl.pallas_call(
        functools.partial(_lru_body, need_ctx=need_ctx),
        grid=(b,),
        in_specs=[
            pl.BlockSpec((1, l, w), lambda bi: (bi, 0, 0)),
            pl.BlockSpec((1, lc, w), lambda bi: (bi, 0, 0)),
            _resident((CONV_W, w)),
            _resident((1, w)),
            _resident((2, w, 2 * w)),
            _resident((2, 1, 2 * w)),
            _resident((2, 1, w)),
        ],
        out_specs=out_specs,
        out_shape=out_shape,
        scratch_shapes=[
            pltpu.VMEM((l + 2 * LRU_HALO, w), F32),
            pltpu.VMEM((lc + 2 * LRU_HALO, w), F32),
            pltpu.VMEM((l, w), F32),
            pltpu.VMEM((lc, w), F32),
            pltpu.VMEM((LRU_TC, w), F32),
        ],
        compiler_params=_params(("parallel",)),
    )(u, uc, conv_w, conv_b.reshape(1, w), wg, bg, lam.reshape(2, 1, w))
    return (res[0], res[1]) if need_ctx else (res[0], None)


def _mla_prep_body(*refs, rotate, need_q):
    it = iter(refs)
    cq_ref = next(it) if need_q else None
    ckv_ref, ra_ref = next(it), next(it)
    rb_ref = next(it) if rotate else None
    if need_q:
        cosq_ref = next(it)
        sinq_ref = next(it) if rotate else None
    cosk_ref = next(it)
    sink_ref = next(it) if rotate else None
    if need_q:
        gq_ref, wq_ref = next(it), next(it)
        wqs_ref = next(it) if rotate else None
    gkv_ref, wkv_ref = next(it), next(it)
    q_out = next(it) if need_q else None
    k_out, v_out = next(it), next(it)
    nh = MLA_HEADS
    tm = ckv_ref.shape[1]

    if need_q:
        cqn = _rms(cq_ref[0].astype(F32), gq_ref[...]).astype(BF16)
        qm = jnp.dot(cqn, wq_ref[...], preferred_element_type=F32)
        cosq = cosq_ref[...]
        if rotate:
            qs = jnp.dot(cqn, wqs_ref[...], preferred_element_type=F32)
            sinq = sinq_ref[...]
        for h in range(nh):
            sl = slice(h * LANES, (h + 1) * LANES)
            qh = qm[:, sl] * cosq
            if rotate:
                qh = qh + qs[:, sl] * sinq
            q_out[0, h] = qh.astype(q_out.dtype)

    kvn = _rms(ckv_ref[0].astype(F32), gkv_ref[...]).astype(BF16)
    kv = jnp.dot(kvn, wkv_ref[...], preferred_element_type=F32)
    rope = ra_ref[0].astype(F32) * cosk_ref[...]
    if rotate:
        rope = rope + rb_ref[0].astype(F32) * sink_ref[...]
    lane = lax.broadcasted_iota(jnp.int32, (tm, LANES), 1)
    for h in range(nh):
        k_out[0, h] = (kv[:, h * LANES:(h + 1) * LANES] + rope).astype(k_out.dtype)
        vh = kv[:, (nh + h) * LANES:(nh + h + 1) * LANES]
        v_out[0, h] = jnp.where(lane == _sum_lane(h), 1.0, vh).astype(v_out.dtype)


def _sum_lane(h):
    return MLA_V if h % 2 == 0 else 0


def _mla_prep(cq, ckv, ra, rb, cosq, sinq, cosk, sink, gq, wq, wqs, gkv, wkv, rotate, need_q, tm):
    b, l, _ = ckv.shape
    nh = MLA_HEADS
    row = lambda wd: pl.BlockSpec((1, tm, wd), lambda bi, i: (bi, i, 0))
    tab = pl.BlockSpec((tm, LANES), lambda bi, i: (i, 0))
    args, specs = [], []

    def add(a, s):
        args.append(a)
        specs.append(s)

    if need_q:
        add(cq, row(Q_LORA))
    add(ckv, row(KV_LORA))
    add(ra, row(LANES))
    if rotate:
        add(rb, row(LANES))
    if need_q:
        add(cosq, tab)
        if rotate:
            add(sinq, tab)
    add(cosk, tab)
    if rotate:
        add(sink, tab)
    if need_q:
        add(gq.reshape(1, Q_LORA), _resident((1, Q_LORA)))
        add(wq, _resident(wq.shape))
        if rotate:
            add(wqs, _resident(wqs.shape))
    add(gkv.reshape(1, KV_LORA), _resident((1, KV_LORA)))
    add(wkv, _resident(wkv.shape))

    head = pl.BlockSpec((1, nh, tm, LANES), lambda bi, i: (bi, 0, i, 0))
    hshape = jax.ShapeDtypeStruct((b, nh, l, LANES), ATTN_DT)
    n_out = 3 if need_q else 2
    res = pl.pallas_call(
        functools.partial(_mla_prep_body, rotate=rotate, need_q=need_q),
        grid=(b, l // tm),
        in_specs=specs,
        out_specs=[head] * n_out,
        out_shape=[hshape] * n_out,
        compiler_params=_params(("parallel", "parallel")),
    )(*args)
    return res if need_q else (None, res[0], res[1])


ATTN_PAIRS_PER_STEP = 4
ATTN_DT = jnp.float8_e4m3fn
ATTN_P_SHIFT = 8.0


def _attn_body(q_ref, *refs, n_kv, tq):
    kv_refs = refs[:2 * n_kv]
    o_ref = refs[2 * n_kv]
    s_bufs = refs[2 * n_kv + 1:2 * n_kv + 3]
    p_bufs = refs[2 * n_kv + 3:2 * n_kv + 5]
    npair = ATTN_PAIRS_PER_STEP
    n_items = (q_ref.shape[2] // tq) * npair
    lks = [kv_refs[2 * j].shape[2] for j in range(n_kv)]
    offs = [sum(lks[:j]) for j in range(n_kv)]
    lane = lax.broadcasted_iota(jnp.int32, (tq, LANES), 1)
    nt_dims = (((1,), (1,)), ((), ()))

    s_bufs[1][...] = jnp.zeros(s_bufs[1].shape, s_bufs[1].dtype)
    p_bufs[0][...] = jnp.zeros(p_bufs[0].shape, p_bufs[0].dtype)

    def item(t):
        tile = lax.shift_right_logical(t, npair.bit_length() - 1)
        pair = jnp.bitwise_and(t, npair - 1)
        return pl.multiple_of(tile * tq, tq), pair

    def stages(t, slot):
        s_a, s_b = s_bufs[slot], s_bufs[1 - slot]
        p_b, p_c = p_bufs[1 - slot], p_bufs[slot]

        row_a, pair_a = item(jnp.minimum(t, n_items - 1))
        for e in range(2):
            h = pair_a * 2 + e
            q = q_ref[0, h, pl.ds(row_a, tq), :]
            for j in range(n_kv):
                s_a[e, :, offs[j]:offs[j] + lks[j]] = lax.dot_general(
                    q, kv_refs[2 * j][0, h], nt_dims, preferred_element_type=F32).astype(s_a.dtype)

        for e in range(2):
            s = s_b[e]
            bm = s[:, 0:LANES]
            for c0 in range(LANES, s.shape[1], LANES):
                bm = jnp.maximum(bm, s[:, c0:c0 + LANES])
            m = bm.astype(F32).max(axis=-1, keepdims=True)
            shift = (m - ATTN_P_SHIFT).astype(s.dtype)
            p_b[e] = jnp.exp2(s - shift).astype(p_b.dtype)

        row_c, pair_c = item(jnp.maximum(t - 2, 0))
        halves = []
        for e in range(2):
            h = pair_c * 2 + e
            o = None
            for j in range(n_kv):
                oj = jnp.dot(p_c[e, :, offs[j]:offs[j] + lks[j]], kv_refs[2 * j + 1][0, h],
                             preferred_element_type=F32)
                o = oj if o is None else o + oj
            one = _sum_lane(e)
            halves.append(o * (1.0 / o[:, one:one + 1]))
        blk = jnp.where(lane < MLA_V, halves[0], halves[1])
        o_ref[0, pair_c, pl.ds(row_c, tq), :] = blk.astype(o_ref.dtype)

    def body(i, carry):
        stages(2 * i, 0)
        stages(2 * i + 1, 1)
        return carry

    lax.fori_loop(0, (n_items + 2) // 2, body, 0)


def _attention(q, kvs, tq):
    b, nh, lq, _ = q.shape
    hs = 2 * ATTN_PAIRS_PER_STEP
    once = lambda n: pl.BlockSpec((1, hs, n, LANES), lambda bi, j: (bi, j, 0, 0))
    specs = [once(lq)]
    args = [q]
    lkt = 0
    for k, v in kvs:
        lk = k.shape[2]
        lkt += lk
        specs += [once(lk), once(lk)]
        args += [k, v]
    return pl.pallas_call(
        functools.partial(_attn_body, n_kv=len(kvs), tq=tq),
        grid=(b, nh // hs),
        in_specs=specs,
        out_specs=pl.BlockSpec((1, ATTN_PAIRS_PER_STEP, lq, LANES), lambda bi, j: (bi, j, 0, 0)),
        out_shape=jax.ShapeDtypeStruct((b, nh // 2, lq, LANES), BF16),
        scratch_shapes=[
            pltpu.VMEM((2, tq, lkt), BF16),
            pltpu.VMEM((2, tq, lkt), BF16),
            pltpu.VMEM((2, tq, lkt), ATTN_DT),
            pltpu.VMEM((2, tq, lkt), ATTN_DT),
        ],
        compiler_params=_params(("parallel", "parallel")),
    )(*args)


def _ret_body(*refs, need_ctx):
    it = iter(refs)
    q_ref, k_ref, v_ref = next(it), next(it), next(it)
    qc_ref = next(it) if need_ctx else None
    kc_ref, vc_ref = next(it), next(it)
    cos_ref, sin_ref = next(it), next(it)
    msum_ref, qdec_ref, kdec_ref, cd_ref = next(it), next(it), next(it), next(it)
    o_ref = next(it)
    oc_ref = next(it) if need_ctx else None
    qa, ka, va, u_s, r_s = (next(it) for _ in range(5))

    c = RET_CHUNK
    l = q_ref.shape[1]
    lc = kc_ref.shape[1]
    ncc = lc // c
    nt = (lc + l) // c
    kscale = RET_DK ** -0.5
    lane = lax.broadcasted_iota(jnp.int32, (c, LANES), 1)
    first_half = (lane % RET_DK) < (RET_DK // 2)

    def rotated(src_ref, t0, g0):
        x = src_ref[0, pl.ds(t0, c), :].astype(F32)
        sw = jnp.where(first_half, pltpu.roll(x, LANES - RET_DK // 2, 1),
                       pltpu.roll(x, RET_DK // 2, 1))
        return x * cos_ref[pl.ds(g0, c), :] + sw * sin_ref[pl.ds(g0, c), :]

    def stage(qsrc, ksrc, vsrc, n, base):
        def body(i, carry):
            t0 = pl.multiple_of(i * c, c)
            g0 = pl.multiple_of(base + i * c, c)
            if qsrc is not None:
                qa[pl.ds(g0, c), :] = rotated(qsrc, t0, g0)
            ka[pl.ds(g0, c), :] = rotated(ksrc, t0, g0) * kscale
            va[pl.ds(g0, c), :] = vsrc[0, pl.ds(t0, c), :]
            return carry

        lax.fori_loop(0, n // c, body, 0, unroll=2)

    stage(qc_ref, kc_ref, vc_ref, lc, 0)
    stage(q_ref, k_ref, v_ref, l, lc)

    tn_dims = (((0,), (0,)), ((), ()))
    nt_dims = (((1,), (1,)), ((), ()))

    def incr(g, carry):
        g0 = pl.multiple_of(g * c, c)
        kf = ka[pl.ds(g0, c), :]
        for e in range(2):
            v = va[pl.ds(g0, c), e * RET_DV:(e + 1) * RET_DV]
            kd = jnp.concatenate([kf * kdec_ref[0, e, 0], kf * kdec_ref[0, e, 1]], axis=1)
            u_s[g, e] = lax.dot_general(kd.astype(BF16), v, tn_dims, preferred_element_type=F32)
        return carry

    lax.fori_loop(0, nt, incr, 0, unroll=2)

    zero = tuple(jnp.zeros((LANES, RET_DV), F32) for _ in range(4))

    def recur(t, st):
        gs = (t, jnp.where(t < ncc, ncc - 1 - t, nt - 1 + ncc - t))
        new = []
        for d in range(2):
            rows = slice(d * LANES, (d + 1) * LANES)
            for e in range(2):
                r = st[2 * d + e]
                r_s[gs[d], e, rows, :] = r.astype(r_s.dtype)
                new.append(cd_ref[0, e, d] * r + u_s[gs[d], e, rows, :])
        return tuple(new)

    lax.fori_loop(0, nt, recur, zero, unroll=2)

    def emit(dst_ref, gbase):
        def body(i, carry):
            t0 = pl.multiple_of(i * c, c)
            g = gbase + i
            g0 = pl.multiple_of(g * c, c)
            qf = qa[pl.ds(g0, c), :]
            kb = ka[pl.ds(g0, c), :].astype(BF16)
            parts = []
            for e in range(2):
                v = va[pl.ds(g0, c), e * RET_DV:(e + 1) * RET_DV]
                qm = jnp.where((lane // RET_DK) == e, qf, 0.0).astype(BF16)
                s = lax.dot_general(qm, kb, nt_dims, preferred_element_type=F32)
                o = jnp.dot((s * msum_ref[0, e]).astype(BF16), v, preferred_element_type=F32)
                qd = jnp.concatenate([qf * qdec_ref[0, e, 0], qf * qdec_ref[0, e, 1]], axis=1)
                o = o + jnp.dot(qd.astype(BF16), r_s[g, e], preferred_element_type=F32)
                mu = jnp.mean(o, axis=-1, keepdims=True)
                oc = o - mu
                var = jnp.mean(oc * oc, axis=-1, keepdims=True)
                parts.append(oc * lax.rsqrt(var + EPS))
            dst_ref[0, pl.ds(t0, c), :] = jnp.concatenate(parts, axis=1).astype(dst_ref.dtype)
            return carry

        return body

    if need_ctx:
        lax.fori_loop(0, ncc, emit(oc_ref, 0), 0, unroll=2)
    lax.fori_loop(0, nt - ncc, emit(o_ref, ncc), 0, unroll=4)


def _retention(q, k, v, qc, kc, vc, tabs, dec, need_ctx):
    b, l, _ = q.shape
    lc = kc.shape[1]
    lt = lc + l
    c = RET_CHUNK
    nt = lt // c
    npair = RET_HEADS // 2
    seq = lambda n, wd: pl.BlockSpec((1, n, wd), lambda bi, p: (bi, 0, p))
    args = [q, k, v]
    specs = [seq(l, LANES), seq(l, LANES), seq(l, 2 * RET_DV)]
    if need_ctx:
        args.append(qc)
        specs.append(seq(lc, LANES))
    args += [kc, vc]
    specs += [seq(lc, LANES), seq(lc, 2 * RET_DV)]
    for t in tabs:
        args.append(t)
        specs.append(_resident((lt, LANES)))
    msum, qdec, kdec, cd = dec
    args += [msum, qdec, kdec, cd]
    specs += [
        pl.BlockSpec((1, 2, c, c), lambda bi, p: (p, 0, 0, 0)),
        pl.BlockSpec((1, 2, 2, c, LANES), lambda bi, p: (p, 0, 0, 0, 0)),
        pl.BlockSpec((1, 2, 2, c, LANES), lambda bi, p: (p, 0, 0, 0, 0)),
        pl.BlockSpec((1, 2, 2, 1, RET_DV), lambda bi, p: (p, 0, 0, 0, 0)),
    ]
    out_shape = [jax.ShapeDtypeStruct((b, l, RET_HEADS * RET_DV), BF16)]
    out_specs = [seq(l, 2 * RET_DV)]
    if need_ctx:
        out_shape.append(jax.ShapeDtypeStruct((b, lc, RET_HEADS * RET_DV), BF16))
        out_specs.append(seq(lc, 2 * RET_DV))
    res = pl.pallas_call(
        functools.partial(_ret_body, need_ctx=need_ctx),
        grid=(b, npair),
        in_specs=specs,
        out_specs=out_specs,
        out_shape=out_shape,
        scratch_shapes=[
            pltpu.VMEM((lt, LANES), F32),
            pltpu.VMEM((lt, LANES), F32),
            pltpu.VMEM((lt, 2 * RET_DV), BF16),
            pltpu.VMEM((nt, 2, 2 * LANES, RET_DV), F32),
            pltpu.VMEM((nt, 2, 2 * LANES, RET_DV), BF16),
        ],
        compiler_params=_params(("parallel", "parallel")),
    )(*args)
    return (res[0], res[1]) if need_ctx else (res[0], None)


def _merge_body(x_ref, rec_ref, gl_ref, yb_ref, rn_ref, gr_ref, gm_ref, ga_ref,
                woa_ref, wob_ref, woc_ref, wout_ref, o_ref):
    d = x_ref.shape[2]
    ya = (jax.nn.gelu(gl_ref[0].astype(F32)) * rec_ref[0].astype(F32)).astype(BF16)
    yc = (jax.nn.silu(gr_ref[0].astype(F32)) * rn_ref[0].astype(F32)).astype(BF16)
    yb = jnp.concatenate([yb_ref[0, p] for p in range(yb_ref.shape[1])], axis=1)
    gm = gm_ref[0]
    m = jax.nn.sigmoid(gm[:, :d].astype(F32)) * jnp.dot(ya, woa_ref[...], preferred_element_type=F32)
    m = m + jax.nn.sigmoid(gm[:, d:2 * d].astype(F32)) * jnp.dot(
        yb, wob_ref[...], preferred_element_type=F32)
    m = m + jax.nn.sigmoid(gm[:, 2 * d:].astype(F32)) * jnp.dot(
        yc, woc_ref[...], preferred_element_type=F32)
    y = jnp.dot(m.astype(BF16), wout_ref[...], preferred_element_type=F32)
    o_ref[0] = x_ref[0] + ga_ref[0] * y


def _merge(x, rec, gl, yb, rn, gr, gm, gate, woa, wob, woc, wout, tm):
    b, l, d = x.shape
    row = lambda wd: pl.BlockSpec((1, tm, wd), lambda bi, i: (bi, i, 0))
    mod_map = (lambda bi, i: (bi, 0, 0)) if gate.shape[0] > 1 else (lambda bi, i: (0, 0, 0))
    return pl.pallas_call(
        _merge_body,
        grid=(b, l // tm),
        in_specs=[
            row(d), row(LRU_W), row(LRU_W),
            pl.BlockSpec((1, MLA_HEADS // 2, tm, LANES), lambda bi, i: (bi, 0, i, 0)),
            row(RET_HEADS * RET_DV), row(RET_HEADS * RET_DV), row(3 * d),
            pl.BlockSpec((1, 1, d), mod_map),
            _resident(woa.shape), _resident(wob.shape), _resident(woc.shape), _resident(wout.shape),
        ],
        out_specs=row(d),
        out_shape=jax.ShapeDtypeStruct((b, l, d), F32),
        compiler_params=_params(("parallel", "parallel")),
    )(x, rec, gl, yb, rn, gr, gm, gate, woa, wob, woc, wout)


FFN_CHUNK = 1408


def _ffn_body(x_ref, g_ref, sh_ref, sc_ref, ga_ref, w1_ref, w3_ref, w2_ref, *rest, final):
    if final:
        gf_ref, o_ref = rest
    else:
        (o_ref,) = rest
    x = x_ref[0]
    hb = (_rms(x, g_ref[...]) * (1.0 + sc_ref[0]) + sh_ref[0]).astype(BF16)
    acc = None
    for c0 in range(0, D_FF, FFN_CHUNK):
        a = jnp.dot(hb, w1_ref[:, c0:c0 + FFN_CHUNK], preferred_element_type=F32)
        bgate = jnp.dot(hb, w3_ref[:, c0:c0 + FFN_CHUNK], preferred_element_type=F32)
        t = (jax.nn.silu(a) * bgate).astype(BF16)
        part = jnp.dot(t, w2_ref[c0:c0 + FFN_CHUNK, :], preferred_element_type=F32)
        acc = part if acc is None else acc + part
    y = x + ga_ref[0] * acc
    if final:
        y = _rms(y, gf_ref[...])
    o_ref[0] = y


def _ffn(x, g, shift, scale, gate, w1, w3, w2, g_final, tm):
    b, l, d = x.shape
    final = g_final is not None
    row = pl.BlockSpec((1, tm, d), lambda bi, i: (bi, i, 0))
    mod_map = (lambda bi, i: (bi, 0, 0)) if shift.shape[0] > 1 else (lambda bi, i: (0, 0, 0))
    mod = pl.BlockSpec((1, 1, d), mod_map)
    args = [x, g.reshape(1, d), shift, scale, gate, w1, w3, w2]
    specs = [row, _resident((1, d)), mod, mod, mod,
             _resident(w1.shape), _resident(w3.shape), _resident(w2.shape)]
    if final:
        args.append(g_final.reshape(1, d))
        specs.append(_resident((1, d)))
    return pl.pallas_call(
        functools.partial(_ffn_body, final=final),
        grid=(b, l // tm),
        in_specs=specs,
        out_specs=row,
        out_shape=jax.ShapeDtypeStruct((b, l, d), F32),
        compiler_params=_params(("parallel", "parallel")),
    )(*args)


def _swap_half(w):
    half = w.shape[-1] // 2
    return jnp.concatenate([-w[..., half:], w[..., :half]], axis=-1)


def _arrange_w_in(w_in):
    d = w_in.shape[0]
    o = 0
    parts = {}
    for name, n in (("lru_x", LRU_W), ("ckv", KV_LORA), ("kr", MLA_ROPE),
                    ("ret_k", RET_HEADS * RET_DK), ("ret_v", RET_HEADS * RET_DV),
                    ("cq", Q_LORA), ("ret_q", RET_HEADS * RET_DK), ("g_lru", LRU_W),
                    ("g_ret", RET_HEADS * RET_DV), ("gm", 3 * D_MODEL)):
        parts[name] = w_in[:, o:o + n]
        o += n
    z = lambda n: jnp.zeros((d, n), w_in.dtype)
    rope_a = jnp.concatenate([z(MLA_NOPE), parts["kr"], z(LANES - MLA_NOPE - MLA_ROPE)], axis=1)
    rope_b = jnp.concatenate([z(MLA_NOPE), _swap_half(parts["kr"]),
                              z(LANES - MLA_NOPE - MLA_ROPE)], axis=1)
    cols = [parts["lru_x"], parts["ckv"], rope_a, rope_b, parts["ret_k"], parts["ret_v"],
            parts["cq"], parts["ret_q"], parts["g_lru"], parts["g_ret"], parts["gm"]]
    return jnp.concatenate(cols, axis=1).astype(BF16)


def _arrange_mla(w_uq, w_ukv):
    nh = MLA_HEADS
    wq = w_uq.reshape(Q_LORA, nh, MLA_NOPE + MLA_ROPE)
    padq = jnp.zeros((Q_LORA, nh, LANES - MLA_NOPE - MLA_ROPE), w_uq.dtype)
    wq_arr = jnp.concatenate([wq, padq], axis=-1).reshape(Q_LORA, nh * LANES)
    wqs = jnp.concatenate([jnp.zeros((Q_LORA, nh, MLA_NOPE), w_uq.dtype),
                           _swap_half(wq[..., MLA_NOPE:]), padq], axis=-1)
    wqs_arr = wqs.reshape(Q_LORA, nh * LANES)
    wkv = w_ukv.reshape(KV_LORA, nh, MLA_NOPE + MLA_V)
    zk = jnp.zeros((KV_LORA, nh, LANES - MLA_NOPE), w_ukv.dtype)
    wk_arr = jnp.concatenate([wkv[..., :MLA_NOPE], zk], axis=-1).reshape(KV_LORA, nh * LANES)
    wv = wkv[..., MLA_NOPE:]
    zv = jnp.zeros_like(wv)
    even = (jnp.arange(nh) % 2 == 0)[None, :, None]
    wv_arr = jnp.concatenate([jnp.where(even, wv, zv), jnp.where(even, zv, wv)], axis=-1)
    wv_arr = wv_arr.reshape(KV_LORA, nh * LANES)
    return (wq_arr.astype(BF16), wqs_arr.astype(BF16),
            jnp.concatenate([wk_arr, wv_arr], axis=1).astype(BF16))


def _arrange_lru(wa, ba, wx, bx):
    def dense(wblk):
        eye = jnp.eye(LRU_BLOCKS, dtype=wblk.dtype)
        full = jnp.einsum("nde,nm->ndme", wblk, eye)
        return full.reshape(LRU_W, LRU_W)

    wg = jnp.stack([jnp.concatenate([dense(wa[d]), dense(wx[d])], axis=1) for d in range(2)])
    bg = jnp.stack([jnp.concatenate([ba[d], bx[d]])[None, :] for d in range(2)])
    return wg.astype(BF16), bg


def _mla_tables(n_tokens, rotate):
    scale = (MLA_NOPE + MLA_ROPE) ** -0.5 * math.log2(math.e)
    pad = LANES - MLA_NOPE - MLA_ROPE
    ones = jnp.ones((n_tokens, MLA_NOPE), F32)
    zeros = jnp.zeros((n_tokens, MLA_NOPE), F32)
    zpad = jnp.zeros((n_tokens, pad), F32)
    if rotate:
        rows = n_tokens // GRID_W
        rowi = jnp.repeat(jnp.arange(rows, dtype=F32), GRID_W)
        coli = jnp.tile(jnp.arange(GRID_W, dtype=F32), rows)
        n_freq = MLA_ROPE // 4
        inv = jnp.power(ROPE_BASE, -jnp.arange(n_freq, dtype=F32) / n_freq)
        ang = jnp.concatenate([rowi[:, None] * inv, coli[:, None] * inv], axis=-1)
        cos, sin = jnp.cos(ang), jnp.sin(ang)
    else:
        cos = jnp.ones((n_tokens, MLA_ROPE // 2), F32)
        sin = jnp.zeros((n_tokens, MLA_ROPE // 2), F32)
    cosq = jnp.concatenate([ones, cos, cos, zpad], axis=1) * scale
    sinq = jnp.concatenate([zeros, sin, sin, zpad], axis=1) * scale
    cosk = jnp.concatenate([zeros, cos, cos, zpad], axis=1)
    sink = jnp.concatenate([zeros, sin, sin, zpad], axis=1)
    return cosq, sinq, cosk, sink


def _ret_tables(n):
    theta = 1.0 / jnp.power(10000.0, jnp.linspace(0.0, 1.0, RET_DK // 2, dtype=F32))
    pos = jnp.arange(n, dtype=F32)
    ang = pos[:, None] * theta
    cos, sin = jnp.cos(ang), jnp.sin(ang)
    reps = LANES // RET_DK
    cos_t = jnp.tile(jnp.concatenate([cos, cos], axis=1), (1, reps))
    sin_t = jnp.tile(jnp.concatenate([-sin, sin], axis=1), (1, reps))
    return cos_t, sin_t


def _ret_decay_tables():
    c = RET_CHUNK
    h = jnp.arange(RET_HEADS, dtype=F32)
    lgs = jnp.stack([jnp.log1p(-jnp.exp2(-5.0 - h)), jnp.log1p(-jnp.exp2(-5.5 - h))], axis=1)
    pos = jnp.arange(c, dtype=F32)
    diff = pos[:, None] - pos[None, :]
    lg = lgs[:, :, None, None]
    inner = jnp.where(diff >= 0, jnp.exp(lg * jnp.maximum(diff, 0.0)), 0.0)
    msum = inner[:, 0] + jnp.swapaxes(inner[:, 1], -1, -2)
    lg1 = lgs[:, :, None]
    qd_f = jnp.exp(lg1 * (pos + 1.0))
    kd_f = jnp.exp(lg1 * (c - 1.0 - pos))
    qd = jnp.stack([qd_f[:, 0], qd_f[:, 1, ::-1]], axis=1)
    kd = jnp.stack([kd_f[:, 0], kd_f[:, 1, ::-1]], axis=1)
    lane_head = (jnp.arange(LANES) // RET_DK)
    e_of_h = jnp.arange(RET_HEADS) % 2
    mask = (lane_head[None, :] == e_of_h[:, None]).astype(F32)
    qdec = qd[..., None] * mask[:, None, None, :]
    kdec = kd[..., None] * mask[:, None, None, :]
    cd = jnp.broadcast_to(jnp.exp(lgs * c)[:, :, None, None], (RET_HEADS, 2, 1, RET_DV))
    pair = lambda t: t.reshape(RET_HEADS // 2, 2, *t.shape[1:])
    return pair(msum), pair(qdec), pair(kdec), pair(cd)


def _row_tile(n, pref):
    t = min(pref, n)
    while n % t:
        t //= 2
    return t


def kernel(x, c, ctx, c_ctx, w_mod, b_mod, g_mix, g_ffn, w_in, conv_w, conv_b, lru_wa, lru_ba,
           lru_wx, lru_bx, lru_lam, g_q, w_uq, g_kv, w_ukv, w_oa, w_ob, w_oc, w_out,
           w_ff1, w_ff3, w_ff2, g_final):
    b, l, d = x.shape
    lc = ctx.shape[1]
    depth = w_mod.shape[0]
    assert d == D_MODEL and l % LRU_TC == 0 and lc % LRU_TC == 0 and l % GRID_W == 0

    tm = _row_tile(l, 512)
    tmc = _row_tile(lc, 256)
    tq = _row_tile(l, 256)
    tqc = _row_tile(lc, 256)

    lat_tabs = _mla_tables(l, True)
    ctx_tabs = _mla_tables(lc, False)
    ret_tabs = _ret_tables(lc + l)
    dec = _ret_decay_tables()

    rows = -(-(b + 1) // SUBLANES) * SUBLANES
    cvec = jnp.zeros((rows, d), F32).at[:b].set(c).at[b].set(c_ctx)
    mod_all = _modulation(cvec, w_mod, b_mod)

    xc = ctx
    for li in range(depth):
        last = li == depth - 1
        mod = mod_all[li, :b].reshape(b, 1, 6, d)
        sh_a, sc_a, ga_a, sh_f, sc_f, ga_f = (mod[:, :, j] for j in range(6))
        mc = mod_all[li, b].reshape(1, 1, 6, d)
        mcs = [mc[:, :, j] for j in range(6)]

        w_arr = _arrange_w_in(w_in[li])
        wq_arr, wqs_arr, wkv_arr = _arrange_mla(w_uq[li], w_ukv[li])
        wg, bg = _arrange_lru(lru_wa[li], lru_ba[li], lru_wx[li], lru_bx[li])
        woa, wob, woc, wout = (w.astype(BF16) for w in (w_oa[li], w_ob[li], w_oc[li], w_out[li]))
        w1, w3, w2 = (w.astype(BF16) for w in (w_ff1[li], w_ff3[li], w_ff2[li]))

        z = _inproj(x, g_mix[li], sh_a, sc_a, w_arr, KV_GROUPS + Q_GROUPS, tm)
        u, ckv, ra, rb, rk, rv, cq, rq, g_lru, g_ret, gm = z
        if last:
            zc = _inproj(xc, g_mix[li], mcs[0], mcs[1], w_arr[:, :N_KV_ARR], KV_GROUPS, tmc)
            uc, ckvc, rac, _, rkc, rvc = zc
            cqc = rqc = None
        else:
            zc = _inproj(xc, g_mix[li], mcs[0], mcs[1], w_arr, KV_GROUPS + Q_GROUPS, tmc)
            uc, ckvc, rac, _, rkc, rvc, cqc, rqc, g_lru_c, g_ret_c, gm_c = zc
        need_ctx = not last

        rec, rec_c = _lru(u, uc, conv_w[li], conv_b[li], wg, bg, lru_lam[li], need_ctx)

        q, k, v = _mla_prep(cq, ckv, ra, rb, *lat_tabs, g_q[li], wq_arr, wqs_arr, g_kv[li],
                            wkv_arr, True, True, tm)
        qc, kc, vc = _mla_prep(cqc, ckvc, rac, None, *ctx_tabs, g_q[li], wq_arr, None, g_kv[li],
                               wkv_arr, False, need_ctx, tmc)
        yb = _attention(q, [(kc, vc), (k, v)], tq)

        rn, rn_c = _retention(rq, rk, rv, rqc, rkc, rvc, ret_tabs, dec, need_ctx)

        x = _merge(x, rec, g_lru, yb, rn, g_ret, gm, ga_a, woa, wob, woc, wout, tm)
        x = _ffn(x, g_ffn[li], sh_f, sc_f, ga_f, w1, w3, w2, g_final if last else None, tm)
        if need_ctx:
            yb_c = _attention(qc, [(kc, vc)], tqc)
            xc = _merge(xc, rec_c, g_lru_c, yb_c, rn_c, g_ret_c, gm_c, mcs[2], woa, wob, woc, wout, tmc)
            xc = _ffn(xc, g_ffn[li], mcs[3], mcs[4], mcs[5], w1, w3, w2, None, tmc)
    return x
```

```python
import functools
import math

import jax
import jax.numpy as jnp
from jax import lax
from jax.experimental import pallas as pl
from jax.experimental.pallas import tpu as pltpu

F32 = jnp.float32
BF16 = jnp.bfloat16

D_MODEL = 1024
EPS = 1e-6
GRID_W = 64
LRU_W = 512
LRU_BLOCKS = 8
LRU_BLOCK = LRU_W // LRU_BLOCKS
CONV_W = 4
CONV_PAD_L = 2
LRU_C = 8.0
MLA_HEADS = 8
MLA_NOPE = 64
MLA_ROPE = 32
MLA_V = 64
Q_LORA = 384
KV_LORA = 256
ROPE_BASE = 10000.0
RET_HEADS = 4
RET_DK = 64
RET_DV = 128
RET_CHUNK = 128
D_FF = 2816

LANES = 128
SUBLANES = 8
MXU_TILE = 256
VMEM_LIMIT = 56 * 1024 * 1024

KV_GROUPS = (LRU_W, KV_LORA, LANES, LANES, RET_HEADS * RET_DK, RET_HEADS * RET_DV)
Q_GROUPS = (Q_LORA, RET_HEADS * RET_DK, LRU_W, RET_HEADS * RET_DV, 3 * D_MODEL)
N_KV_ARR = sum(KV_GROUPS)


def _params(sem, flags=None):
    return pltpu.CompilerParams(dimension_semantics=sem, vmem_limit_bytes=VMEM_LIMIT, flags=flags)


def _resident(shape):
    nd = len(shape)
    return pl.BlockSpec(shape, lambda *_: (0,) * nd, pipeline_mode=pl.Buffered(1))


def _layer_spec(stacked, li, cols=None):
    shape = stacked.shape[1:] if cols is None else stacked.shape[1:-1] + (cols,)
    nd = len(shape)
    return pl.BlockSpec((None,) + shape, lambda *_: (li,) + (0,) * nd, pipeline_mode=pl.Buffered(1))


def _rms(x, g):
    y = x * lax.rsqrt(jnp.mean(x * x, axis=-1, keepdims=True) + EPS)
    return y * g


def _mod_body(c_ref, w_ref, b_ref, o_ref):
    cv = c_ref[...]
    act = cv * jax.nn.sigmoid(cv)
    o_ref[0] = jnp.dot(act, w_ref[0], precision=lax.Precision.HIGHEST,
                       preferred_element_type=F32) + b_ref[0]


def _modulation(cvec, w_mod, b_mod):
    depth, d, n = w_mod.shape
    rows = cvec.shape[0]
    tn = 1536
    return pl.pallas_call(
        _mod_body,
        grid=(depth, n // tn),
        in_specs=[
            pl.BlockSpec((rows, d), lambda l, j: (0, 0)),
            pl.BlockSpec((1, d, tn), lambda l, j: (l, 0, j)),
            pl.BlockSpec((1, 1, tn), lambda l, j: (l, 0, j)),
        ],
        out_specs=pl.BlockSpec((1, rows, tn), lambda l, j: (l, 0, j)),
        out_shape=jax.ShapeDtypeStruct((depth, rows, n), F32),
        compiler_params=_params(("arbitrary", "arbitrary")),
    )(cvec, w_mod, b_mod.reshape(depth, 1, n))


def _inproj_body(x_ref, g_ref, sh_ref, sc_ref, w_ref, *out_refs, widths):
    x = x_ref[0]
    h = _rms(x, g_ref[...]) * (1.0 + sc_ref[0]) + sh_ref[0]
    hb = h.astype(BF16)
    start, off = 0, 0
    for i, wd in enumerate(widths):
        off += wd
        if off % MXU_TILE == 0 or i == len(widths) - 1:
            base = sum(widths[:start])
            z = jnp.dot(hb, w_ref[:, base:off], preferred_element_type=F32)
            col = 0
            for j in range(start, i + 1):
                out_refs[j][0] = z[:, col:col + widths[j]].astype(out_refs[j].dtype)
                col += widths[j]
            start = i + 1


def _inproj(x, g, shift, scale, w_all, li, widths, tm):
    b, l, d = x.shape
    n = sum(widths)
    mod_map = (lambda bi, i: (bi, 0, 0)) if shift.shape[0] > 1 else (lambda bi, i: (0, 0, 0))
    return pl.pallas_call(
        functools.partial(_inproj_body, widths=widths),
        grid=(b, l // tm),
        in_specs=[
            pl.BlockSpec((1, tm, d), lambda bi, i: (bi, i, 0)),
            _resident((1, d)),
            pl.BlockSpec((1, 1, d), mod_map),
            pl.BlockSpec((1, 1, d), mod_map),
            _layer_spec(w_all, li, n),
        ],
        out_specs=[pl.BlockSpec((1, tm, wd), lambda bi, i: (bi, i, 0)) for wd in widths],
        out_shape=[jax.ShapeDtypeStruct((b, l, wd), BF16) for wd in widths],
        compiler_params=_params(("parallel", "parallel")),
    )(x, g.reshape(1, d), shift, scale, w_all)


LRU_TC = 256
LRU_HALO = SUBLANES


def _lru_body(u_ref, uc_ref, cw_ref, cb_ref, wg_ref, bg_ref, lam_ref, *rest, need_ctx):
    if need_ctx:
        out_ref, outc_ref, upad, upadc, ucv, ucvc, hbuf = rest
    else:
        out_ref, upad, upadc, ucv, ucvc, hbuf = rest
        outc_ref = None
    l = u_ref.shape[1]
    lc = uc_ref.shape[1]
    w = u_ref.shape[2]
    tc = LRU_TC
    nt = tc // SUBLANES

    zeros_halo = jnp.zeros((LRU_HALO, w), F32)
    for pad_ref, cv_ref, src_ref, n in ((upad, ucv, u_ref, l), (upadc, ucvc, uc_ref, lc)):
        pad_ref[0:LRU_HALO, :] = zeros_halo
        pad_ref[LRU_HALO + n:2 * LRU_HALO + n, :] = zeros_halo

        def copy(i, carry, pad_ref=pad_ref, src_ref=src_ref):
            t0 = pl.multiple_of(i * tc, tc)
            pad_ref[pl.ds(LRU_HALO + t0, tc), :] = src_ref[0, pl.ds(t0, tc), :].astype(F32)
            return carry

        lax.fori_loop(0, n // tc, copy, 0)

        def conv(i, carry, pad_ref=pad_ref, cv_ref=cv_ref):
            t0 = pl.multiple_of(i * tc, tc)
            win = pad_ref[pl.ds(t0, tc + 2 * LRU_HALO), :]
            u = cb_ref[...]
            for k in range(CONV_W):
                sh = (CONV_PAD_L - k) % (tc + 2 * LRU_HALO)
                tap = win if sh == 0 else pltpu.roll(win, sh, 0)
                u = u + cw_ref[k:k + 1, :] * tap[LRU_HALO:LRU_HALO + tc]
            cv_ref[pl.ds(t0, tc), :] = u
            return carry

        lax.fori_loop(0, n // tc, conv, 0)

    row = lax.broadcasted_iota(jnp.int32, (nt, SUBLANES, w), 1)

    def chunk_scan(cv_ref, t0, d, h):
        u = cv_ref[pl.ds(t0, tc), :]
        g = jnp.dot(u.astype(BF16), wg_ref[d], preferred_element_type=F32) + bg_ref[d]
        r = jax.nn.sigmoid(g[:, :w])
        gi = jax.nn.sigmoid(g[:, w:])
        z = -lam_ref[d]
        sp = jnp.maximum(z, 0.0) + jnp.log1p(jnp.exp(-jnp.abs(z)))
        log_a = r * (-LRU_C * sp)
        a = jnp.exp2(r * (-LRU_C * math.log2(math.e) * sp))
        th = jnp.tanh(log_a)
        om = -2.0 * th / (1.0 - th)
        root = jnp.where(om > 0.0, om * lax.rsqrt(om), 0.0)
        bb = root * (gi * u)
        a = a.reshape(nt, SUBLANES, w)
        bb = bb.reshape(nt, SUBLANES, w)
        s = 1
        while s < SUBLANES:
            if d == 0:
                keep = row >= s
                shift = s
            else:
                keep = row < SUBLANES - s
                shift = SUBLANES - s
            a_s = jnp.where(keep, pltpu.roll(a, shift, 1), 1.0)
            b_s = jnp.where(keep, pltpu.roll(bb, shift, 1), 0.0)
            bb = a * b_s + bb
            a = a * a_s
            s *= 2
        order = range(nt) if d == 0 else range(nt - 1, -1, -1)
        for k in order:
            hk = a[k] * h + bb[k]
            hbuf[k * SUBLANES:(k + 1) * SUBLANES, :] = hk
            h = hk[SUBLANES - 1:SUBLANES] if d == 0 else hk[0:1]
        return h

    def run(cv_ref, n, d, h, write):
        nchunks = n // tc

        def body(i, h):
            ci = i if d == 0 else nchunks - 1 - i
            t0 = pl.multiple_of(ci * tc, tc)
            h = chunk_scan(cv_ref, t0, d, h)
            write(t0)
            return h

        return lax.fori_loop(0, nchunks, body, h)

    def write_accc(t0):
        upadc[pl.ds(t0, tc), :] = hbuf[...]

    def write_acc(t0):
        upad[pl.ds(t0, tc), :] = hbuf[...]

    def write_outc(t0):
        if outc_ref is not None:
            outc_ref[0, pl.ds(t0, tc), :] = (upadc[pl.ds(t0, tc), :] + hbuf[...]).astype(outc_ref.dtype)

    def write_out(t0):
        out_ref[0, pl.ds(t0, tc), :] = (upad[pl.ds(t0, tc), :] + hbuf[...]).astype(out_ref.dtype)

    h0 = jnp.zeros((1, w), F32)
    h = run(ucvc, lc, 0, h0, write_accc)
    run(ucv, l, 0, h, write_acc)
    h = run(ucvc, lc, 1, h0, write_outc)
    run(ucv, l, 1, h, write_out)


def _lru(u, uc, conv_w, conv_b, wg, bg, lam, need_ctx):
    b, l, w = u.shape
    lc = uc.shape[1]
    out_shape = [jax.ShapeDtypeStruct((b, l, w), BF16)]
    out_specs = [pl.BlockSpec((1, l, w), lambda bi: (bi, 0, 0))]
    if need_ctx:
        out_shape.append(jax.ShapeDtypeStruct((b, lc, w), BF16))
        out_specs.append(pl.BlockSpec((1, lc, w), lambda bi: (bi, 0, 0)))
    res = pl.pallas_call(
        functools.partial(_lru_body, need_ctx=need_ctx),
        grid=(b,),
        in_specs=[
            pl.BlockSpec((1, l, w), lambda bi: (bi, 0, 0)),
            pl.BlockSpec((1, lc, w), lambda bi: (bi, 0, 0)),
            _resident((CONV_W, w)),
            _resident((1, w)),
            _resident((2, w, 2 * w)),
            _resident((2, 1, 2 * w)),
            _resident((2, 1, w)),
        ],
        out_specs=out_specs,
        out_shape=out_shape,
        scratch_shapes=[
            pltpu.VMEM((l + 2 * LRU_HALO, w), F32),
            pltpu.VMEM((lc + 2 * LRU_HALO, w), F32),
            pltpu.VMEM((l, w), F32),
            pltpu.VMEM((lc, w), F32),
            pltpu.VMEM((LRU_TC, w), F32),
        ],
        compiler_params=_params(("parallel",)),
    )(u, uc, conv_w, conv_b.reshape(1, w), wg, bg, lam.reshape(2, 1, w))
    return (res[0], res[1]) if need_ctx else (res[0], None)


def _mla_prep_body(*refs, rotate, need_q):
    it = iter(refs)
    cq_ref = next(it) if need_q else None
    ckv_ref, ra_ref = next(it), next(it)
    rb_ref = next(it) if rotate else None
    if need_q:
        cosq_ref = next(it)
        sinq_ref = next(it) if rotate else None
    cosk_ref = next(it)
    sink_ref = next(it) if rotate else None
    if need_q:
        gq_ref, wq_ref = next(it), next(it)
        wqs_ref = next(it) if rotate else None
    gkv_ref, wkv_ref = next(it), next(it)
    q_out = next(it) if need_q else None
    k_out, v_out = next(it), next(it)
    nh = MLA_HEADS
    tm = ckv_ref.shape[1]

    if need_q:
        cqn = _rms(cq_ref[0].astype(F32), gq_ref[...]).astype(BF16)
        qm = jnp.dot(cqn, wq_ref[...], preferred_element_type=F32)
        cosq = cosq_ref[...]
        if rotate:
            qs = jnp.dot(cqn, wqs_ref[...], preferred_element_type=F32)
            sinq = sinq_ref[...]
        for h in range(nh):
            sl = slice(h * LANES, (h + 1) * LANES)
            qh = qm[:, sl] * cosq
            if rotate:
                qh = qh + qs[:, sl] * sinq
            q_out[0, h] = qh.astype(q_out.dtype)

    kvn = _rms(ckv_ref[0].astype(F32), gkv_ref[...]).astype(BF16)
    kv = jnp.dot(kvn, wkv_ref[...], preferred_element_type=F32)
    rope = ra_ref[0].astype(F32) * cosk_ref[...]
    if rotate:
        rope = rope + rb_ref[0].astype(F32) * sink_ref[...]
    lane = lax.broadcasted_iota(jnp.int32, (tm, LANES), 1)
    for h in range(nh):
        k_out[0, h] = (kv[:, h * LANES:(h + 1) * LANES] + rope).astype(k_out.dtype)
        vh = kv[:, (nh + h) * LANES:(nh + h + 1) * LANES]
        v_out[0, h] = jnp.where(lane == _sum_lane(h), 1.0, vh).astype(v_out.dtype)


def _sum_lane(h):
    return MLA_V if h % 2 == 0 else 0


def _mla_prep(cq, ckv, ra, rb, cosq, sinq, cosk, sink, gq, wq, wqs, gkv, wkv, rotate, need_q, tm):
    b, l, _ = ckv.shape
    nh = MLA_HEADS
    row = lambda wd: pl.BlockSpec((1, tm, wd), lambda bi, i: (bi, i, 0))
    tab = pl.BlockSpec((tm, LANES), lambda bi, i: (i, 0))
    args, specs = [], []

    def add(a, s):
        args.append(a)
        specs.append(s)

    if need_q:
        add(cq, row(Q_LORA))
    add(ckv, row(KV_LORA))
    add(ra, row(LANES))
    if rotate:
        add(rb, row(LANES))
    if need_q:
        add(cosq, tab)
        if rotate:
            add(sinq, tab)
    add(cosk, tab)
    if rotate:
        add(sink, tab)
    if need_q:
        add(gq.reshape(1, Q_LORA), _resident((1, Q_LORA)))
        add(wq, _resident(wq.shape))
        if rotate:
            add(wqs, _resident(wqs.shape))
    add(gkv.reshape(1, KV_LORA), _resident((1, KV_LORA)))
    add(wkv, _resident(wkv.shape))

    head = pl.BlockSpec((1, nh, tm, LANES), lambda bi, i: (bi, 0, i, 0))
    hshape = jax.ShapeDtypeStruct((b, nh, l, LANES), ATTN_DT)
    n_out = 3 if need_q else 2
    res = pl.pallas_call(
        functools.partial(_mla_prep_body, rotate=rotate, need_q=need_q),
        grid=(b, l // tm),
        in_specs=specs,
        out_specs=[head] * n_out,
        out_shape=[hshape] * n_out,
        compiler_params=_params(("parallel", "parallel")),
    )(*args)
    return res if need_q else (None, res[0], res[1])


ATTN_PAIRS_PER_STEP = 4
ATTN_DT = jnp.float8_e4m3fn
ATTN_P_SHIFT = 8.0


def _attn_body(q_ref, *refs, n_kv, tq):
    kv_refs = refs[:2 * n_kv]
    o_ref = refs[2 * n_kv]
    s_bufs = refs[2 * n_kv + 1:2 * n_kv + 3]
    p_bufs = refs[2 * n_kv + 3:2 * n_kv + 5]
    npair = ATTN_PAIRS_PER_STEP
    n_items = (q_ref.shape[2] // tq) * npair
    lks = [kv_refs[2 * j].shape[2] for j in range(n_kv)]
    offs = [sum(lks[:j]) for j in range(n_kv)]
    lane = lax.broadcasted_iota(jnp.int32, (tq, LANES), 1)
    nt_dims = (((1,), (1,)), ((), ()))

    s_bufs[1][...] = jnp.zeros(s_bufs[1].shape, s_bufs[1].dtype)
    p_bufs[0][...] = jnp.zeros(p_bufs[0].shape, p_bufs[0].dtype)

    def item(t):
        tile = lax.shift_right_logical(t, npair.bit_length() - 1)
        pair = jnp.bitwise_and(t, npair - 1)
        return pl.multiple_of(tile * tq, tq), pair

    def stages(t, slot):
        s_a, s_b = s_bufs[slot], s_bufs[1 - slot]
        p_b, p_c = p_bufs[1 - slot], p_bufs[slot]

        row_a, pair_a = item(jnp.minimum(t, n_items - 1))
        for e in range(2):
            h = pair_a * 2 + e
            q = q_ref[0, h, pl.ds(row_a, tq), :]
            for j in range(n_kv):
                s_a[e, :, offs[j]:offs[j] + lks[j]] = lax.dot_general(
                    q, kv_refs[2 * j][0, h], nt_dims, preferred_element_type=F32).astype(s_a.dtype)

        for e in range(2):
            s = s_b[e]
            bm = s[:, 0:LANES]
            for c0 in range(LANES, s.shape[1], LANES):
                bm = jnp.maximum(bm, s[:, c0:c0 + LANES])
            m = bm.astype(F32).max(axis=-1, keepdims=True)
            shift = (m - ATTN_P_SHIFT).astype(s.dtype)
            p_b[e] = jnp.exp2(s - shift).astype(p_b.dtype)

        row_c, pair_c = item(jnp.maximum(t - 2, 0))
        halves = []
        for e in range(2):
            h = pair_c * 2 + e
            o = None
            for j in range(n_kv):
                oj = jnp.dot(p_c[e, :, offs[j]:offs[j] + lks[j]], kv_refs[2 * j + 1][0, h],
                             preferred_element_type=F32)
                o = oj if o is None else o + oj
            one = _sum_lane(e)
            halves.append(o * (1.0 / o[:, one:one + 1]))
        blk = jnp.where(lane < MLA_V, halves[0], halves[1])
        o_ref[0, pair_c, pl.ds(row_c, tq), :] = blk.astype(o_ref.dtype)

    def body(i, carry):
        stages(2 * i, 0)
        stages(2 * i + 1, 1)
        return carry

    lax.fori_loop(0, (n_items + 2) // 2, body, 0)


def _attention(q, kvs, tq):
    b, nh, lq, _ = q.shape
    hs = 2 * ATTN_PAIRS_PER_STEP
    once = lambda n: pl.BlockSpec((1, hs, n, LANES), lambda bi, j: (bi, j, 0, 0))
    specs = [once(lq)]
    args = [q]
    lkt = 0
    for k, v in kvs:
        lk = k.shape[2]
        lkt += lk
        specs += [once(lk), once(lk)]
        args += [k, v]
    return pl.pallas_call(
        functools.partial(_attn_body, n_kv=len(kvs), tq=tq),
        grid=(b, nh // hs),
        in_specs=specs,
        out_specs=pl.BlockSpec((1, ATTN_PAIRS_PER_STEP, lq, LANES), lambda bi, j: (bi, j, 0, 0)),
        out_shape=jax.ShapeDtypeStruct((b, nh // 2, lq, LANES), BF16),
        scratch_shapes=[
            pltpu.VMEM((2, tq, lkt), BF16),
            pltpu.VMEM((2, tq, lkt), BF16),
            pltpu.VMEM((2, tq, lkt), ATTN_DT),
            pltpu.VMEM((2, tq, lkt), ATTN_DT),
        ],
        compiler_params=_params(("parallel", "parallel")),
    )(*args)


def _ret_body(*refs, need_ctx):
    it = iter(refs)
    q_ref, k_ref, v_ref = next(it), next(it), next(it)
    qc_ref = next(it) if need_ctx else None
    kc_ref, vc_ref = next(it), next(it)
    cos_ref, sin_ref = next(it), next(it)
    msum_ref, qdec_ref, kdec_ref, cd_ref = next(it), next(it), next(it), next(it)
    o_ref = next(it)
    oc_ref = next(it) if need_ctx else None
    qa, ka, va, u_s, r_s = (next(it) for _ in range(5))

    c = RET_CHUNK
    l = q_ref.shape[1]
    lc = kc_ref.shape[1]
    ncc = lc // c
    nt = (lc + l) // c
    kscale = RET_DK ** -0.5
    lane = lax.broadcasted_iota(jnp.int32, (c, LANES), 1)
    first_half = (lane % RET_DK) < (RET_DK // 2)

    def rotated(src_ref, t0, g0):
        x = src_ref[0, pl.ds(t0, c), :].astype(F32)
        sw = jnp.where(first_half, pltpu.roll(x, LANES - RET_DK // 2, 1),
                       pltpu.roll(x, RET_DK // 2, 1))
        return x * cos_ref[pl.ds(g0, c), :] + sw * sin_ref[pl.ds(g0, c), :]

    def stage(qsrc, ksrc, vsrc, n, base):
        def body(i, carry):
            t0 = pl.multiple_of(i * c, c)
            g0 = pl.multiple_of(base + i * c, c)
            if qsrc is not None:
                qa[pl.ds(g0, c), :] = rotated(qsrc, t0, g0)
            ka[pl.ds(g0, c), :] = rotated(ksrc, t0, g0) * kscale
            va[pl.ds(g0, c), :] = vsrc[0, pl.ds(t0, c), :]
            return carry

        lax.fori_loop(0, n // c, body, 0, unroll=2)

    stage(qc_ref, kc_ref, vc_ref, lc, 0)
    stage(q_ref, k_ref, v_ref, l, lc)

    tn_dims = (((0,), (0,)), ((), ()))
    nt_dims = (((1,), (1,)), ((), ()))

    def incr(g, carry):
        g0 = pl.multiple_of(g * c, c)
        kf = ka[pl.ds(g0, c), :]
        for e in range(2):
            v = va[pl.ds(g0, c), e * RET_DV:(e + 1) * RET_DV]
            kd = jnp.concatenate([kf * kdec_ref[0, e, 0], kf * kdec_ref[0, e, 1]], axis=1)
            u_s[g, e] = lax.dot_general(kd.astype(BF16), v, tn_dims, preferred_element_type=F32)
        return carry

    lax.fori_loop(0, nt, incr, 0, unroll=4)

    zero = tuple(jnp.zeros((LANES, RET_DV), F32) for _ in range(4))

    def recur(t, st):
        gs = (t, jnp.where(t < ncc, ncc - 1 - t, nt - 1 + ncc - t))
        new = []
        for d in range(2):
            rows = slice(d * LANES, (d + 1) * LANES)
            for e in range(2):
                r = st[2 * d + e]
                r_s[gs[d], e, rows, :] = r.astype(r_s.dtype)
                new.append(cd_ref[0, e, d] * r + u_s[gs[d], e, rows, :])
        return tuple(new)

    lax.fori_loop(0, nt, recur, zero, unroll=2)

    def emit(dst_ref, gbase):
        def body(i, carry):
            t0 = pl.multiple_of(i * c, c)
            g = gbase + i
            g0 = pl.multiple_of(g * c, c)
            qf = qa[pl.ds(g0, c), :]
            kb = ka[pl.ds(g0, c), :].astype(BF16)
            parts = []
            for e in range(2):
                v = va[pl.ds(g0, c), e * RET_DV:(e + 1) * RET_DV]
                qm = jnp.where((lane // RET_DK) == e, qf, 0.0).astype(BF16)
                s = lax.dot_general(qm, kb, nt_dims, preferred_element_type=F32)
                o = jnp.dot((s * msum_ref[0, e]).astype(BF16), v, preferred_element_type=F32)
                qd = jnp.concatenate([qf * qdec_ref[0, e, 0], qf * qdec_ref[0, e, 1]], axis=1)
                o = o + jnp.dot(qd.astype(BF16), r_s[g, e], preferred_element_type=F32)
                mu = jnp.mean(o, axis=-1, keepdims=True)
                oc = o - mu
                var = jnp.mean(oc * oc, axis=-1, keepdims=True)
                parts.append(oc * lax.rsqrt(var + EPS))
            dst_ref[0, pl.ds(t0, c), :] = jnp.concatenate(parts, axis=1).astype(dst_ref.dtype)
            return carry

        return body

    if need_ctx:
        lax.fori_loop(0, ncc, emit(oc_ref, 0), 0, unroll=2)
    lax.fori_loop(0, nt - ncc, emit(o_ref, ncc), 0, unroll=8)


def _retention(q, k, v, qc, kc, vc, tabs, dec, need_ctx):
    b, l, _ = q.shape
    lc = kc.shape[1]
    lt = lc + l
    c = RET_CHUNK
    nt = lt // c
    npair = RET_HEADS // 2
    seq = lambda n, wd: pl.BlockSpec((1, n, wd), lambda bi, p: (bi, 0, p))
    args = [q, k, v]
    specs = [seq(l, LANES), seq(l, LANES), seq(l, 2 * RET_DV)]
    if need_ctx:
        args.append(qc)
        specs.append(seq(lc, LANES))
    args += [kc, vc]
    specs += [seq(lc, LANES), seq(lc, 2 * RET_DV)]
    for t in tabs:
        args.append(t)
        specs.append(_resident((lt, LANES)))
    msum, qdec, kdec, cd = dec
    args += [msum, qdec, kdec, cd]
    specs += [
        pl.BlockSpec((1, 2, c, c), lambda bi, p: (p, 0, 0, 0)),
        pl.BlockSpec((1, 2, 2, c, LANES), lambda bi, p: (p, 0, 0, 0, 0)),
        pl.BlockSpec((1, 2, 2, c, LANES), lambda bi, p: (p, 0, 0, 0, 0)),
        pl.BlockSpec((1, 2, 2, 1, RET_DV), lambda bi, p: (p, 0, 0, 0, 0)),
    ]
    out_shape = [jax.ShapeDtypeStruct((b, l, RET_HEADS * RET_DV), BF16)]
    out_specs = [seq(l, 2 * RET_DV)]
    if need_ctx:
        out_shape.append(jax.ShapeDtypeStruct((b, lc, RET_HEADS * RET_DV), BF16))
        out_specs.append(seq(lc, 2 * RET_DV))
    res = pl.pallas_call(
        functools.partial(_ret_body, need_ctx=need_ctx),
        grid=(b, npair),
        in_specs=specs,
        out_specs=out_specs,
        out_shape=out_shape,
        scratch_shapes=[
            pltpu.VMEM((lt, LANES), F32),
            pltpu.VMEM((lt, LANES), F32),
            pltpu.VMEM((lt, 2 * RET_DV), BF16),
            pltpu.VMEM((nt, 2, 2 * LANES, RET_DV), F32),
            pltpu.VMEM((nt, 2, 2 * LANES, RET_DV), BF16),
        ],
        compiler_params=_params(("parallel", "parallel")),
    )(*args)
    return (res[0], res[1]) if need_ctx else (res[0], None)


def _merge_body(x_ref, rec_ref, gl_ref, yb_ref, rn_ref, gr_ref, gm_ref, ga_ref,
                woa_ref, wob_ref, woc_ref, wout_ref, o_ref):
    d = x_ref.shape[2]
    ya = (jax.nn.gelu(gl_ref[0].astype(F32)) * rec_ref[0].astype(F32)).astype(BF16)
    yc = (jax.nn.silu(gr_ref[0].astype(F32)) * rn_ref[0].astype(F32)).astype(BF16)
    yb = jnp.concatenate([yb_ref[0, p] for p in range(yb_ref.shape[1])], axis=1)
    gm = gm_ref[0]
    m = jax.nn.sigmoid(gm[:, :d].astype(F32)) * jnp.dot(ya, woa_ref[...], preferred_element_type=F32)
    m = m + jax.nn.sigmoid(gm[:, d:2 * d].astype(F32)) * jnp.dot(
        yb, wob_ref[...], preferred_element_type=F32)
    m = m + jax.nn.sigmoid(gm[:, 2 * d:].astype(F32)) * jnp.dot(
        yc, woc_ref[...], preferred_element_type=F32)
    y = jnp.dot(m.astype(BF16), wout_ref[...], preferred_element_type=F32)
    o_ref[0] = x_ref[0] + ga_ref[0] * y


def _merge(x, rec, gl, yb, rn, gr, gm, gate, woa, wob, woc, wout, li, tm):
    b, l, d = x.shape
    row = lambda wd: pl.BlockSpec((1, tm, wd), lambda bi, i: (bi, i, 0))
    mod_map = (lambda bi, i: (bi, 0, 0)) if gate.shape[0] > 1 else (lambda bi, i: (0, 0, 0))
    return pl.pallas_call(
        _merge_body,
        grid=(b, l // tm),
        in_specs=[
            row(d), row(LRU_W), row(LRU_W),
            pl.BlockSpec((1, MLA_HEADS // 2, tm, LANES), lambda bi, i: (bi, 0, i, 0)),
            row(RET_HEADS * RET_DV), row(RET_HEADS * RET_DV), row(3 * d),
            pl.BlockSpec((1, 1, d), mod_map),
            _layer_spec(woa, li), _layer_spec(wob, li), _layer_spec(woc, li), _layer_spec(wout, li),
        ],
        out_specs=row(d),
        out_shape=jax.ShapeDtypeStruct((b, l, d), F32),
        compiler_params=_params(("parallel", "parallel")),
    )(x, rec, gl, yb, rn, gr, gm, gate, woa, wob, woc, wout)


FFN_CHUNKS = ((0, 1536), (1536, D_FF))
assert all(lo % MXU_TILE == 0 and hi % MXU_TILE == 0 for lo, hi in FFN_CHUNKS)


def _ffn_body(x_ref, g_ref, sh_ref, sc_ref, ga_ref, w1_ref, w3_ref, w2_ref, *rest, final):
    if final:
        gf_ref, o_ref = rest
    else:
        (o_ref,) = rest
    x = x_ref[0]
    hb = (_rms(x, g_ref[...]) * (1.0 + sc_ref[0]) + sh_ref[0]).astype(BF16)
    acc = None
    for c0, c1 in FFN_CHUNKS:
        a = jnp.dot(hb, w1_ref[:, c0:c1], preferred_element_type=F32)
        bgate = jnp.dot(hb, w3_ref[:, c0:c1], preferred_element_type=F32)
        t = (jax.nn.silu(a) * bgate).astype(BF16)
        part = jnp.dot(t, w2_ref[c0:c1, :], preferred_element_type=F32)
        acc = part if acc is None else acc + part
    y = x + ga_ref[0] * acc
    if final:
        y = _rms(y, gf_ref[...])
    o_ref[0] = y


def _ffn(x, g, shift, scale, gate, w1, w3, w2, li, g_final, tm):
    b, l, d = x.shape
    final = g_final is not None
    row = pl.BlockSpec((1, tm, d), lambda bi, i: (bi, i, 0))
    mod_map = (lambda bi, i: (bi, 0, 0)) if shift.shape[0] > 1 else (lambda bi, i: (0, 0, 0))
    mod = pl.BlockSpec((1, 1, d), mod_map)
    args = [x, g.reshape(1, d), shift, scale, gate, w1, w3, w2]
    specs = [row, _resident((1, d)), mod, mod, mod,
             _layer_spec(w1, li), _layer_spec(w3, li), _layer_spec(w2, li)]
    if final:
        args.append(g_final.reshape(1, d))
        specs.append(_resident((1, d)))
    return pl.pallas_call(
        functools.partial(_ffn_body, final=final),
        grid=(b, l // tm),
        in_specs=specs,
        out_specs=row,
        out_shape=jax.ShapeDtypeStruct((b, l, d), F32),
        compiler_params=_params(("parallel", "parallel")),
    )(*args)


def _swap_half(w):
    half = w.shape[-1] // 2
    return jnp.concatenate([-w[..., half:], w[..., :half]], axis=-1)


def _arrange_w_in(w_in):
    w_in = w_in.astype(BF16)
    lead = w_in.shape[:-1]
    o = 0
    parts = {}
    for name, n in (("lru_x", LRU_W), ("ckv", KV_LORA), ("kr", MLA_ROPE),
                    ("ret_k", RET_HEADS * RET_DK), ("ret_v", RET_HEADS * RET_DV),
                    ("cq", Q_LORA), ("ret_q", RET_HEADS * RET_DK), ("g_lru", LRU_W),
                    ("g_ret", RET_HEADS * RET_DV), ("gm", 3 * D_MODEL)):
        parts[name] = w_in[..., o:o + n]
        o += n
    z = lambda n: jnp.zeros(lead + (n,), w_in.dtype)
    rope_a = jnp.concatenate([z(MLA_NOPE), parts["kr"], z(LANES - MLA_NOPE - MLA_ROPE)], axis=-1)
    rope_b = jnp.concatenate([z(MLA_NOPE), _swap_half(parts["kr"]),
                              z(LANES - MLA_NOPE - MLA_ROPE)], axis=-1)
    cols = [parts["lru_x"], parts["ckv"], rope_a, rope_b, parts["ret_k"], parts["ret_v"],
            parts["cq"], parts["ret_q"], parts["g_lru"], parts["g_ret"], parts["gm"]]
    return jnp.concatenate(cols, axis=-1)


def _arrange_mla(w_uq, w_ukv):
    nh = MLA_HEADS
    wq = w_uq.reshape(Q_LORA, nh, MLA_NOPE + MLA_ROPE)
    padq = jnp.zeros((Q_LORA, nh, LANES - MLA_NOPE - MLA_ROPE), w_uq.dtype)
    wq_arr = jnp.concatenate([wq, padq], axis=-1).reshape(Q_LORA, nh * LANES)
    wqs = jnp.concatenate([jnp.zeros((Q_LORA, nh, MLA_NOPE), w_uq.dtype),
                           _swap_half(wq[..., MLA_NOPE:]), padq], axis=-1)
    wqs_arr = wqs.reshape(Q_LORA, nh * LANES)
    wkv = w_ukv.reshape(KV_LORA, nh, MLA_NOPE + MLA_V)
    zk = jnp.zeros((KV_LORA, nh, LANES - MLA_NOPE), w_ukv.dtype)
    wk_arr = jnp.concatenate([wkv[..., :MLA_NOPE], zk], axis=-1).reshape(KV_LORA, nh * LANES)
    wv = wkv[..., MLA_NOPE:]
    zv = jnp.zeros_like(wv)
    even = (jnp.arange(nh) % 2 == 0)[None, :, None]
    wv_arr = jnp.concatenate([jnp.where(even, wv, zv), jnp.where(even, zv, wv)], axis=-1)
    wv_arr = wv_arr.reshape(KV_LORA, nh * LANES)
    return (wq_arr.astype(BF16), wqs_arr.astype(BF16),
            jnp.concatenate([wk_arr, wv_arr], axis=1).astype(BF16))


def _arrange_lru(wa, ba, wx, bx):
    def dense(wblk):
        eye = jnp.eye(LRU_BLOCKS, dtype=wblk.dtype)
        full = jnp.einsum("nde,nm->ndme", wblk, eye)
        return full.reshape(LRU_W, LRU_W)

    wg = jnp.stack([jnp.concatenate([dense(wa[d]), dense(wx[d])], axis=1) for d in range(2)])
    bg = jnp.stack([jnp.concatenate([ba[d], bx[d]])[None, :] for d in range(2)])
    return wg.astype(BF16), bg


def _mla_tables(n_tokens, rotate):
    scale = (MLA_NOPE + MLA_ROPE) ** -0.5 * math.log2(math.e)
    pad = LANES - MLA_NOPE - MLA_ROPE
    ones = jnp.ones((n_tokens, MLA_NOPE), F32)
    zeros = jnp.zeros((n_tokens, MLA_NOPE), F32)
    zpad = jnp.zeros((n_tokens, pad), F32)
    if rotate:
        rows = n_tokens // GRID_W
        rowi = jnp.repeat(jnp.arange(rows, dtype=F32), GRID_W)
        coli = jnp.tile(jnp.arange(GRID_W, dtype=F32), rows)
        n_freq = MLA_ROPE // 4
        inv = jnp.power(ROPE_BASE, -jnp.arange(n_freq, dtype=F32) / n_freq)
        ang = jnp.concatenate([rowi[:, None] * inv, coli[:, None] * inv], axis=-1)
        cos, sin = jnp.cos(ang), jnp.sin(ang)
    else:
        cos = jnp.ones((n_tokens, MLA_ROPE // 2), F32)
        sin = jnp.zeros((n_tokens, MLA_ROPE // 2), F32)
    cosq = jnp.concatenate([ones, cos, cos, zpad], axis=1) * scale
    sinq = jnp.concatenate([zeros, sin, sin, zpad], axis=1) * scale
    cosk = jnp.concatenate([zeros, cos, cos, zpad], axis=1)
    sink = jnp.concatenate([zeros, sin, sin, zpad], axis=1)
    return cosq, sinq, cosk, sink


def _ret_tables(n):
    theta = 1.0 / jnp.power(10000.0, jnp.linspace(0.0, 1.0, RET_DK // 2, dtype=F32))
    pos = jnp.arange(n, dtype=F32)
    ang = pos[:, None] * theta
    cos, sin = jnp.cos(ang), jnp.sin(ang)
    reps = LANES // RET_DK
    cos_t = jnp.tile(jnp.concatenate([cos, cos], axis=1), (1, reps))
    sin_t = jnp.tile(jnp.concatenate([-sin, sin], axis=1), (1, reps))
    return cos_t, sin_t


def _ret_decay_tables():
    c = RET_CHUNK
    h = jnp.arange(RET_HEADS, dtype=F32)
    lgs = jnp.stack([jnp.log1p(-jnp.exp2(-5.0 - h)), jnp.log1p(-jnp.exp2(-5.5 - h))], axis=1)
    pos = jnp.arange(c, dtype=F32)
    diff = pos[:, None] - pos[None, :]
    lg = lgs[:, :, None, None]
    inner = jnp.where(diff >= 0, jnp.exp(lg * jnp.maximum(diff, 0.0)), 0.0)
    msum = inner[:, 0] + jnp.swapaxes(inner[:, 1], -1, -2)
    lg1 = lgs[:, :, None]
    qd_f = jnp.exp(lg1 * (pos + 1.0))
    kd_f = jnp.exp(lg1 * (c - 1.0 - pos))
    qd = jnp.stack([qd_f[:, 0], qd_f[:, 1, ::-1]], axis=1)
    kd = jnp.stack([kd_f[:, 0], kd_f[:, 1, ::-1]], axis=1)
    lane_head = (jnp.arange(LANES) // RET_DK)
    e_of_h = jnp.arange(RET_HEADS) % 2
    mask = (lane_head[None, :] == e_of_h[:, None]).astype(F32)
    qdec = qd[..., None] * mask[:, None, None, :]
    kdec = kd[..., None] * mask[:, None, None, :]
    cd = jnp.broadcast_to(jnp.exp(lgs * c)[:, :, None, None], (RET_HEADS, 2, 1, RET_DV))
    pair = lambda t: t.reshape(RET_HEADS // 2, 2, *t.shape[1:])
    return pair(msum), pair(qdec), pair(kdec), pair(cd)


def _row_tile(n, pref):
    t = min(pref, n)
    while n % t:
        t //= 2
    return t


def kernel(x, c, ctx, c_ctx, w_mod, b_mod, g_mix, g_ffn, w_in, conv_w, conv_b, lru_wa, lru_ba,
           lru_wx, lru_bx, lru_lam, g_q, w_uq, g_kv, w_ukv, w_oa, w_ob, w_oc, w_out,
           w_ff1, w_ff3, w_ff2, g_final):
    b, l, d = x.shape
    lc = ctx.shape[1]
    depth = w_mod.shape[0]
    assert d == D_MODEL and l % LRU_TC == 0 and lc % LRU_TC == 0 and l % GRID_W == 0

    tm = _row_tile(l, 512)
    tmc = _row_tile(lc, 256)
    tq = _row_tile(l, 256)
    tqc = _row_tile(lc, 256)

    lat_tabs = _mla_tables(l, True)
    ctx_tabs = _mla_tables(lc, False)
    ret_tabs = _ret_tables(lc + l)
    dec = _ret_decay_tables()

    rows = -(-(b + 1) // SUBLANES) * SUBLANES
    cvec = jnp.zeros((rows, d), F32).at[:b].set(c).at[b].set(c_ctx)
    mod_all = _modulation(cvec, w_mod, b_mod)

    w_arr = _arrange_w_in(w_in)
    woa, wob, woc, wout = (w.astype(BF16) for w in (w_oa, w_ob, w_oc, w_out))
    w1, w3, w2 = (w.astype(BF16) for w in (w_ff1, w_ff3, w_ff2))

    xc = ctx
    for li in range(depth):
        last = li == depth - 1
        mod = mod_all[li, :b].reshape(b, 1, 6, d)
        sh_a, sc_a, ga_a, sh_f, sc_f, ga_f = (mod[:, :, j] for j in range(6))
        mc = mod_all[li, b].reshape(1, 1, 6, d)
        mcs = [mc[:, :, j] for j in range(6)]

        wq_arr, wqs_arr, wkv_arr = _arrange_mla(w_uq[li], w_ukv[li])
        wg, bg = _arrange_lru(lru_wa[li], lru_ba[li], lru_wx[li], lru_bx[li])

        z = _inproj(x, g_mix[li], sh_a, sc_a, w_arr, li, KV_GROUPS + Q_GROUPS, tm)
        u, ckv, ra, rb, rk, rv, cq, rq, g_lru, g_ret, gm = z
        if last:
            zc = _inproj(xc, g_mix[li], mcs[0], mcs[1], w_arr, li, KV_GROUPS, tmc)
            uc, ckvc, rac, _, rkc, rvc = zc
            cqc = rqc = None
        else:
            zc = _inproj(xc, g_mix[li], mcs[0], mcs[1], w_arr, li, KV_GROUPS + Q_GROUPS, tmc)
            uc, ckvc, rac, _, rkc, rvc, cqc, rqc, g_lru_c, g_ret_c, gm_c = zc
        need_ctx = not last

        rec, rec_c = _lru(u, uc, conv_w[li], conv_b[li], wg, bg, lru_lam[li], need_ctx)

        q, k, v = _mla_prep(cq, ckv, ra, rb, *lat_tabs, g_q[li], wq_arr, wqs_arr, g_kv[li],
                            wkv_arr, True, True, tm)
        qc, kc, vc = _mla_prep(cqc, ckvc, rac, None, *ctx_tabs, g_q[li], wq_arr, None, g_kv[li],
                               wkv_arr, False, need_ctx, tmc)
        yb = _attention(q, [(kc, vc), (k, v)], tq)

        rn, rn_c = _retention(rq, rk, rv, rqc, rkc, rvc, ret_tabs, dec, need_ctx)

        x = _merge(x, rec, g_lru, yb, rn, g_ret, gm, ga_a, woa, wob, woc, wout, li, tm)
        x = _ffn(x, g_ffn[li], sh_f, sc_f, ga_f, w1, w3, w2, li, g_final if last else None, tm)
        if need_ctx:
            yb_c = _attention(qc, [(kc, vc)], tqc)
            xc = _merge(xc, rec_c, g_lru_c, yb_c, rn_c, g_ret_c, gm_c, mcs[2], woa, wob, woc, wout, li, tmc)
            xc = _ffn(xc, g_ffn[li], mcs[3], mcs[4], mcs[5], w1, w3, w2, li, None, tmc)
    return x
```

```python
import functools
import math

import jax
import jax.numpy as jnp
from jax import lax
from jax.experimental import pallas as pl
from jax.experimental.pallas import tpu as pltpu

F32 = jnp.float32
BF16 = jnp.bfloat16

D_MODEL = 1024
EPS = 1e-6
GRID_W = 64
LRU_W = 512
LRU_BLOCKS = 8
LRU_BLOCK = LRU_W // LRU_BLOCKS
CONV_W = 4
CONV_PAD_L = 2
LRU_C = 8.0
MLA_HEADS = 8
MLA_NOPE = 64
MLA_ROPE = 32
MLA_V = 64
Q_LORA = 384
KV_LORA = 256
ROPE_BASE = 10000.0
RET_HEADS = 4
RET_DK = 64
RET_DV = 128
RET_CHUNK = 128
D_FF = 2816

LANES = 128
SUBLANES = 8
MXU_TILE = 256
VMEM_LIMIT = 56 * 1024 * 1024

KV_GROUPS = (LRU_W, KV_LORA, LANES, LANES, RET_HEADS * RET_DK, RET_HEADS * RET_DV)
Q_GROUPS = (Q_LORA, RET_HEADS * RET_DK, LRU_W, RET_HEADS * RET_DV, 3 * D_MODEL)
N_KV_ARR = sum(KV_GROUPS)


def _params(sem, flags=None):
    return pltpu.CompilerParams(dimension_semantics=sem, vmem_limit_bytes=VMEM_LIMIT, flags=flags)


def _resident(shape):
    nd = len(shape)
    return pl.BlockSpec(shape, lambda *_: (0,) * nd, pipeline_mode=pl.Buffered(1))


def _layer_spec(stacked, li, cols=None):
    shape = stacked.shape[1:] if cols is None else stacked.shape[1:-1] + (cols,)
    nd = len(shape)
    return pl.BlockSpec((None,) + shape, lambda *_: (li,) + (0,) * nd, pipeline_mode=pl.Buffered(1))


def _rms(x, g):
    y = x * lax.rsqrt(jnp.mean(x * x, axis=-1, keepdims=True) + EPS)
    return y * g


def _mod_body(c_ref, w_ref, b_ref, o_ref):
    cv = c_ref[...]
    act = cv * jax.nn.sigmoid(cv)
    o_ref[0] = jnp.dot(act, w_ref[0], precision=lax.Precision.HIGHEST,
                       preferred_element_type=F32) + b_ref[0]


def _modulation(cvec, w_mod, b_mod):
    depth, d, n = w_mod.shape
    rows = cvec.shape[0]
    tn = 1536
    return pl.pallas_call(
        _mod_body,
        grid=(depth, n // tn),
        in_specs=[
            pl.BlockSpec((rows, d), lambda l, j: (0, 0)),
            pl.BlockSpec((1, d, tn), lambda l, j: (l, 0, j)),
            pl.BlockSpec((1, 1, tn), lambda l, j: (l, 0, j)),
        ],
        out_specs=pl.BlockSpec((1, rows, tn), lambda l, j: (l, 0, j)),
        out_shape=jax.ShapeDtypeStruct((depth, rows, n), F32),
        compiler_params=_params(("arbitrary", "arbitrary")),
    )(cvec, w_mod, b_mod.reshape(depth, 1, n))


def _inproj_body(x_ref, g_ref, sh_ref, sc_ref, w_ref, *out_refs, widths):
    x = x_ref[0]
    h = _rms(x, g_ref[...]) * (1.0 + sc_ref[0]) + sh_ref[0]
    hb = h.astype(BF16)
    start, off = 0, 0
    for i, wd in enumerate(widths):
        off += wd
        if off % MXU_TILE == 0 or i == len(widths) - 1:
            base = sum(widths[:start])
            z = jnp.dot(hb, w_ref[:, base:off], preferred_element_type=F32)
            col = 0
            for j in range(start, i + 1):
                out_refs[j][0] = z[:, col:col + widths[j]].astype(out_refs[j].dtype)
                col += widths[j]
            start = i + 1


def _inproj(x, g, shift, scale, w_all, li, widths, tm):
    b, l, d = x.shape
    n = sum(widths)
    mod_map = (lambda bi, i: (bi, 0, 0)) if shift.shape[0] > 1 else (lambda bi, i: (0, 0, 0))
    return pl.pallas_call(
        functools.partial(_inproj_body, widths=widths),
        grid=(b, l // tm),
        in_specs=[
            pl.BlockSpec((1, tm, d), lambda bi, i: (bi, i, 0)),
            _resident((1, d)),
            pl.BlockSpec((1, 1, d), mod_map),
            pl.BlockSpec((1, 1, d), mod_map),
            _layer_spec(w_all, li, n),
        ],
        out_specs=[pl.BlockSpec((1, tm, wd), lambda bi, i: (bi, i, 0)) for wd in widths],
        out_shape=[jax.ShapeDtypeStruct((b, l, wd), BF16) for wd in widths],
        compiler_params=_params(("parallel", "parallel")),
    )(x, g.reshape(1, d), shift, scale, w_all)


LRU_TC = 256
LRU_HALO = SUBLANES


def _lru_body(u_ref, uc_ref, cw_ref, cb_ref, wg_ref, bg_ref, lam_ref, *rest, need_ctx):
    if need_ctx:
        out_ref, outc_ref, upad, upadc, ucv, ucvc, hbuf = rest
    else:
        out_ref, upad, upadc, ucv, ucvc, hbuf = rest
        outc_ref = None
    l = u_ref.shape[1]
    lc = uc_ref.shape[1]
    w = u_ref.shape[2]
    tc = LRU_TC
    nt = tc // SUBLANES

    zeros_halo = jnp.zeros((LRU_HALO, w), F32)
    for pad_ref, cv_ref, src_ref, n in ((upad, ucv, u_ref, l), (upadc, ucvc, uc_ref, lc)):
        pad_ref[0:LRU_HALO, :] = zeros_halo
        pad_ref[LRU_HALO + n:2 * LRU_HALO + n, :] = zeros_halo

        def copy(i, carry, pad_ref=pad_ref, src_ref=src_ref):
            t0 = pl.multiple_of(i * tc, tc)
            pad_ref[pl.ds(LRU_HALO + t0, tc), :] = src_ref[0, pl.ds(t0, tc), :].astype(F32)
            return carry

        lax.fori_loop(0, n // tc, copy, 0)

        def conv(i, carry, pad_ref=pad_ref, cv_ref=cv_ref):
            t0 = pl.multiple_of(i * tc, tc)
            win = pad_ref[pl.ds(t0, tc + 2 * LRU_HALO), :]
            u = cb_ref[...]
            for k in range(CONV_W):
                sh = (CONV_PAD_L - k) % (tc + 2 * LRU_HALO)
                tap = win if sh == 0 else pltpu.roll(win, sh, 0)
                u = u + cw_ref[k:k + 1, :] * tap[LRU_HALO:LRU_HALO + tc]
            cv_ref[pl.ds(t0, tc), :] = u
            return carry

        lax.fori_loop(0, n // tc, conv, 0)

    row = lax.broadcasted_iota(jnp.int32, (nt, SUBLANES, w), 1)

    def chunk_scan(cv_ref, t0, d, h):
        u = cv_ref[pl.ds(t0, tc), :]
        g = jnp.dot(u.astype(BF16), wg_ref[d], preferred_element_type=F32) + bg_ref[d]
        r = jax.nn.sigmoid(g[:, :w])
        gi = jax.nn.sigmoid(g[:, w:])
        z = -lam_ref[d]
        sp = jnp.maximum(z, 0.0) + jnp.log1p(jnp.exp(-jnp.abs(z)))
        log_a = r * (-LRU_C * sp)
        a = jnp.exp2(r * (-LRU_C * math.log2(math.e) * sp))
        th = jnp.tanh(log_a)
        om = -2.0 * th / (1.0 - th)
        root = jnp.where(om > 0.0, om * lax.rsqrt(om), 0.0)
        bb = root * (gi * u)
        a = a.reshape(nt, SUBLANES, w)
        bb = bb.reshape(nt, SUBLANES, w)
        s = 1
        while s < SUBLANES:
            if d == 0:
                keep = row >= s
                shift = s
            else:
                keep = row < SUBLANES - s
                shift = SUBLANES - s
            a_s = jnp.where(keep, pltpu.roll(a, shift, 1), 1.0)
            b_s = jnp.where(keep, pltpu.roll(bb, shift, 1), 0.0)
            bb = a * b_s + bb
            a = a * a_s
            s *= 2
        order = range(nt) if d == 0 else range(nt - 1, -1, -1)
        for k in order:
            hk = a[k] * h + bb[k]
            hbuf[k * SUBLANES:(k + 1) * SUBLANES, :] = hk
            h = hk[SUBLANES - 1:SUBLANES] if d == 0 else hk[0:1]
        return h

    def run(cv_ref, n, d, h, write):
        nchunks = n // tc

        def body(i, h):
            ci = i if d == 0 else nchunks - 1 - i
            t0 = pl.multiple_of(ci * tc, tc)
            h = chunk_scan(cv_ref, t0, d, h)
            write(t0)
            return h

        return lax.fori_loop(0, nchunks, body, h, unroll=2)

    def write_accc(t0):
        upadc[pl.ds(t0, tc), :] = hbuf[...]

    def write_acc(t0):
        upad[pl.ds(t0, tc), :] = hbuf[...]

    def write_outc(t0):
        if outc_ref is not None:
            outc_ref[0, pl.ds(t0, tc), :] = (upadc[pl.ds(t0, tc), :] + hbuf[...]).astype(outc_ref.dtype)

    def write_out(t0):
        out_ref[0, pl.ds(t0, tc), :] = (upad[pl.ds(t0, tc), :] + hbuf[...]).astype(out_ref.dtype)

    h0 = jnp.zeros((1, w), F32)
    h = run(ucvc, lc, 0, h0, write_accc)
    run(ucv, l, 0, h, write_acc)
    h = run(ucvc, lc, 1, h0, write_outc)
    run(ucv, l, 1, h, write_out)


def _lru(u, uc, conv_w, conv_b, wg, bg, lam, need_ctx):
    b, l, w = u.shape
    lc = uc.shape[1]
    out_shape = [jax.ShapeDtypeStruct((b, l, w), BF16)]
    out_specs = [pl.BlockSpec((1, l, w), lambda bi: (bi, 0, 0))]
    if need_ctx:
        out_shape.append(jax.ShapeDtypeStruct((b, lc, w), BF16))
        out_specs.append(pl.BlockSpec((1, lc, w), lambda bi: (bi, 0, 0)))
    res = pl.pallas_call(
        functools.partial(_lru_body, need_ctx=need_ctx),
        grid=(b,),
        in_specs=[
            pl.BlockSpec((1, l, w), lambda bi: (bi, 0, 0)),
            pl.BlockSpec((1, lc, w), lambda bi: (bi, 0, 0)),
            _resident((CONV_W, w)),
            _resident((1, w)),
            _resident((2, w, 2 * w)),
            _resident((2, 1, 2 * w)),
            _resident((2, 1, w)),
        ],
        out_specs=out_specs,
        out_shape=out_shape,
        scratch_shapes=[
            pltpu.VMEM((l + 2 * LRU_HALO, w), F32),
            pltpu.VMEM((lc + 2 * LRU_HALO, w), F32),
            pltpu.VMEM((l, w), F32),
            pltpu.VMEM((lc, w), F32),
            pltpu.VMEM((LRU_TC, w), F32),
        ],
        compiler_params=_params(("parallel",)),
    )(u, uc, conv_w, conv_b.reshape(1, w), wg, bg, lam.reshape(2, 1, w))
    return (res[0], res[1]) if need_ctx else (res[0], None)


def _mla_prep_body(*refs, rotate, need_q):
    it = iter(refs)
    cq_ref = next(it) if need_q else None
    ckv_ref, ra_ref = next(it), next(it)
    rb_ref = next(it) if rotate else None
    if need_q:
        cosq_ref = next(it)
        sinq_ref = next(it) if rotate else None
    cosk_ref = next(it)
    sink_ref = next(it) if rotate else None
    if need_q:
        gq_ref, wq_ref = next(it), next(it)
        wqs_ref = next(it) if rotate else None
    gkv_ref, wkv_ref = next(it), next(it)
    q_out = next(it) if need_q else None
    k_out, v_out = next(it), next(it)
    nh = MLA_HEADS
    tm = ckv_ref.shape[1]

    if need_q:
        cqn = _rms(cq_ref[0].astype(F32), gq_ref[...]).astype(BF16)
        qm = jnp.dot(cqn, wq_ref[...], preferred_element_type=F32)
        cosq = cosq_ref[...]
        if rotate:
            qs = jnp.dot(cqn, wqs_ref[...], preferred_element_type=F32)
            sinq = sinq_ref[...]
        for h in range(nh):
            sl = slice(h * LANES, (h + 1) * LANES)
            qh = qm[:, sl] * cosq
            if rotate:
                qh = qh + qs[:, sl] * sinq
            q_out[0, h] = qh.astype(q_out.dtype)

    kvn = _rms(ckv_ref[0].astype(F32), gkv_ref[...]).astype(BF16)
    kv = jnp.dot(kvn, wkv_ref[...], preferred_element_type=F32)
    rope = ra_ref[0].astype(F32) * cosk_ref[...]
    if rotate:
        rope = rope + rb_ref[0].astype(F32) * sink_ref[...]
    lane = lax.broadcasted_iota(jnp.int32, (tm, LANES), 1)
    for h in range(nh):
        k_out[0, h] = (kv[:, h * LANES:(h + 1) * LANES] + rope).astype(k_out.dtype)
        vh = kv[:, (nh + h) * LANES:(nh + h + 1) * LANES]
        v_out[0, h] = jnp.where(lane == _sum_lane(h), 1.0, vh).astype(v_out.dtype)


def _sum_lane(h):
    return MLA_V if h % 2 == 0 else 0


def _mla_prep(cq, ckv, ra, rb, cosq, sinq, cosk, sink, gq, wq, wqs, gkv, wkv, rotate, need_q, tm):
    b, l, _ = ckv.shape
    nh = MLA_HEADS
    row = lambda wd: pl.BlockSpec((1, tm, wd), lambda bi, i: (bi, i, 0))
    tab = pl.BlockSpec((tm, LANES), lambda bi, i: (i, 0))
    args, specs = [], []

    def add(a, s):
        args.append(a)
        specs.append(s)

    if need_q:
        add(cq, row(Q_LORA))
    add(ckv, row(KV_LORA))
    add(ra, row(LANES))
    if rotate:
        add(rb, row(LANES))
    if need_q:
        add(cosq, tab)
        if rotate:
            add(sinq, tab)
    add(cosk, tab)
    if rotate:
        add(sink, tab)
    if need_q:
        add(gq.reshape(1, Q_LORA), _resident((1, Q_LORA)))
        add(wq, _resident(wq.shape))
        if rotate:
            add(wqs, _resident(wqs.shape))
    add(gkv.reshape(1, KV_LORA), _resident((1, KV_LORA)))
    add(wkv, _resident(wkv.shape))

    head = pl.BlockSpec((1, nh, tm, LANES), lambda bi, i: (bi, 0, i, 0))
    hshape = jax.ShapeDtypeStruct((b, nh, l, LANES), ATTN_DT)
    n_out = 3 if need_q else 2
    res = pl.pallas_call(
        functools.partial(_mla_prep_body, rotate=rotate, need_q=need_q),
        grid=(b, l // tm),
        in_specs=specs,
        out_specs=[head] * n_out,
        out_shape=[hshape] * n_out,
        compiler_params=_params(("parallel", "parallel")),
    )(*args)
    return res if need_q else (None, res[0], res[1])


ATTN_PAIRS_PER_STEP = 4
ATTN_DT = jnp.float8_e4m3fn
ATTN_P_SHIFT = 8.0


def _attn_body(q_ref, *refs, n_kv, tq):
    kv_refs = refs[:2 * n_kv]
    o_ref = refs[2 * n_kv]
    s_bufs = refs[2 * n_kv + 1:2 * n_kv + 3]
    p_bufs = refs[2 * n_kv + 3:2 * n_kv + 5]
    npair = ATTN_PAIRS_PER_STEP
    n_items = (q_ref.shape[2] // tq) * npair
    lks = [kv_refs[2 * j].shape[2] for j in range(n_kv)]
    offs = [sum(lks[:j]) for j in range(n_kv)]
    lane = lax.broadcasted_iota(jnp.int32, (tq, LANES), 1)
    nt_dims = (((1,), (1,)), ((), ()))

    s_bufs[1][...] = jnp.zeros(s_bufs[1].shape, s_bufs[1].dtype)
    p_bufs[0][...] = jnp.zeros(p_bufs[0].shape, p_bufs[0].dtype)

    def item(t):
        tile = lax.shift_right_logical(t, npair.bit_length() - 1)
        pair = jnp.bitwise_and(t, npair - 1)
        return pl.multiple_of(tile * tq, tq), pair

    def stages(t, slot):
        s_a, s_b = s_bufs[slot], s_bufs[1 - slot]
        p_b, p_c = p_bufs[1 - slot], p_bufs[slot]

        row_a, pair_a = item(jnp.minimum(t, n_items - 1))
        for e in range(2):
            h = pair_a * 2 + e
            q = q_ref[0, h, pl.ds(row_a, tq), :]
            for j in range(n_kv):
                s_a[e, :, offs[j]:offs[j] + lks[j]] = lax.dot_general(
                    q, kv_refs[2 * j][0, h], nt_dims, preferred_element_type=F32).astype(s_a.dtype)

        for e in range(2):
            s = s_b[e]
            bm = s[:, 0:LANES]
            for c0 in range(LANES, s.shape[1], LANES):
                bm = jnp.maximum(bm, s[:, c0:c0 + LANES])
            m = bm.astype(F32).max(axis=-1, keepdims=True)
            shift = (m - ATTN_P_SHIFT).astype(s.dtype)
            p_b[e] = jnp.exp2(s - shift).astype(p_b.dtype)

        row_c, pair_c = item(jnp.maximum(t - 2, 0))
        halves = []
        for e in range(2):
            h = pair_c * 2 + e
            o = None
            for j in range(n_kv):
                oj = jnp.dot(p_c[e, :, offs[j]:offs[j] + lks[j]], kv_refs[2 * j + 1][0, h],
                             preferred_element_type=F32)
                o = oj if o is None else o + oj
            one = _sum_lane(e)
            halves.append(o * (1.0 / o[:, one:one + 1]))
        blk = jnp.where(lane < MLA_V, halves[0], halves[1])
        o_ref[0, pair_c, pl.ds(row_c, tq), :] = blk.astype(o_ref.dtype)

    def body(i, carry):
        stages(2 * i, 0)
        stages(2 * i + 1, 1)
        return carry

    lax.fori_loop(0, (n_items + 2) // 2, body, 0)


def _attention(q, kvs, tq):
    b, nh, lq, _ = q.shape
    hs = 2 * ATTN_PAIRS_PER_STEP
    once = lambda n: pl.BlockSpec((1, hs, n, LANES), lambda bi, j: (bi, j, 0, 0))
    specs = [once(lq)]
    args = [q]
    lkt = 0
    for k, v in kvs:
        lk = k.shape[2]
        lkt += lk
        specs += [once(lk), once(lk)]
        args += [k, v]
    return pl.pallas_call(
        functools.partial(_attn_body, n_kv=len(kvs), tq=tq),
        grid=(b, nh // hs),
        in_specs=specs,
        out_specs=pl.BlockSpec((1, ATTN_PAIRS_PER_STEP, lq, LANES), lambda bi, j: (bi, j, 0, 0)),
        out_shape=jax.ShapeDtypeStruct((b, nh // 2, lq, LANES), BF16),
        scratch_shapes=[
            pltpu.VMEM((2, tq, lkt), BF16),
            pltpu.VMEM((2, tq, lkt), BF16),
            pltpu.VMEM((2, tq, lkt), ATTN_DT),
            pltpu.VMEM((2, tq, lkt), ATTN_DT),
        ],
        compiler_params=_params(("parallel", "parallel")),
    )(*args)


def _ret_body(*refs, need_ctx):
    it = iter(refs)
    q_ref, k_ref, v_ref = next(it), next(it), next(it)
    qc_ref = next(it) if need_ctx else None
    kc_ref, vc_ref = next(it), next(it)
    cos_ref, sin_ref = next(it), next(it)
    msum_ref, qdec_ref, kdec_ref, cd_ref = next(it), next(it), next(it), next(it)
    o_ref = next(it)
    oc_ref = next(it) if need_ctx else None
    qa, ka, va, u_s, r_s = (next(it) for _ in range(5))

    c = RET_CHUNK
    l = q_ref.shape[1]
    lc = kc_ref.shape[1]
    ncc = lc // c
    nt = (lc + l) // c
    kscale = RET_DK ** -0.5
    lane = lax.broadcasted_iota(jnp.int32, (c, LANES), 1)
    first_half = (lane % RET_DK) < (RET_DK // 2)

    def rotated(src_ref, t0, g0):
        x = src_ref[0, pl.ds(t0, c), :].astype(F32)
        sw = jnp.where(first_half, pltpu.roll(x, LANES - RET_DK // 2, 1),
                       pltpu.roll(x, RET_DK // 2, 1))
        return x * cos_ref[pl.ds(g0, c), :] + sw * sin_ref[pl.ds(g0, c), :]

    def stage(qsrc, ksrc, vsrc, n, base):
        def body(i, carry):
            t0 = pl.multiple_of(i * c, c)
            g0 = pl.multiple_of(base + i * c, c)
            if qsrc is not None:
                qa[pl.ds(g0, c), :] = rotated(qsrc, t0, g0)
            ka[pl.ds(g0, c), :] = rotated(ksrc, t0, g0) * kscale
            va[pl.ds(g0, c), :] = vsrc[0, pl.ds(t0, c), :]
            return carry

        lax.fori_loop(0, n // c, body, 0, unroll=2)

    stage(qc_ref, kc_ref, vc_ref, lc, 0)
    stage(q_ref, k_ref, v_ref, l, lc)

    tn_dims = (((0,), (0,)), ((), ()))
    nt_dims = (((1,), (1,)), ((), ()))

    def incr(g, carry):
        g0 = pl.multiple_of(g * c, c)
        kf = ka[pl.ds(g0, c), :]
        for e in range(2):
            v = va[pl.ds(g0, c), e * RET_DV:(e + 1) * RET_DV]
            kd = jnp.concatenate([kf * kdec_ref[0, e, 0], kf * kdec_ref[0, e, 1]], axis=1)
            u_s[g, e] = lax.dot_general(kd.astype(BF16), v, tn_dims, preferred_element_type=F32)
        return carry

    lax.fori_loop(0, nt, incr, 0, unroll=4)

    zero = tuple(jnp.zeros((LANES, RET_DV), F32) for _ in range(4))

    def recur(t, st):
        gs = (t, jnp.where(t < ncc, ncc - 1 - t, nt - 1 + ncc - t))
        new = []
        for d in range(2):
            rows = slice(d * LANES, (d + 1) * LANES)
            for e in range(2):
                r = st[2 * d + e]
                r_s[gs[d], e, rows, :] = r.astype(r_s.dtype)
                new.append(cd_ref[0, e, d] * r + u_s[gs[d], e, rows, :])
        return tuple(new)

    lax.fori_loop(0, nt, recur, zero, unroll=2)

    def emit(dst_ref, gbase):
        def body(i, carry):
            t0 = pl.multiple_of(i * c, c)
            g = gbase + i
            g0 = pl.multiple_of(g * c, c)
            qf = qa[pl.ds(g0, c), :]
            kb = ka[pl.ds(g0, c), :].astype(BF16)
            parts = []
            for e in range(2):
                v = va[pl.ds(g0, c), e * RET_DV:(e + 1) * RET_DV]
                qm = jnp.where((lane // RET_DK) == e, qf, 0.0).astype(BF16)
                s = lax.dot_general(qm, kb, nt_dims, preferred_element_type=F32)
                o = jnp.dot((s * msum_ref[0, e]).astype(BF16), v, preferred_element_type=F32)
                qd = jnp.concatenate([qf * qdec_ref[0, e, 0], qf * qdec_ref[0, e, 1]], axis=1)
                o = o + jnp.dot(qd.astype(BF16), r_s[g, e], preferred_element_type=F32)
                mu = jnp.mean(o, axis=-1, keepdims=True)
                oc = o - mu
                var = jnp.mean(oc * oc, axis=-1, keepdims=True)
                parts.append(oc * lax.rsqrt(var + EPS))
            dst_ref[0, pl.ds(t0, c), :] = jnp.concatenate(parts, axis=1).astype(dst_ref.dtype)
            return carry

        return body

    if need_ctx:
        lax.fori_loop(0, ncc, emit(oc_ref, 0), 0, unroll=2)
    lax.fori_loop(0, nt - ncc, emit(o_ref, ncc), 0, unroll=8)


def _retention(q, k, v, qc, kc, vc, tabs, dec, need_ctx):
    b, l, _ = q.shape
    lc = kc.shape[1]
    lt = lc + l
    c = RET_CHUNK
    nt = lt // c
    npair = RET_HEADS // 2
    seq = lambda n, wd: pl.BlockSpec((1, n, wd), lambda bi, p: (bi, 0, p))
    args = [q, k, v]
    specs = [seq(l, LANES), seq(l, LANES), seq(l, 2 * RET_DV)]
    if need_ctx:
        args.append(qc)
        specs.append(seq(lc, LANES))
    args += [kc, vc]
    specs += [seq(lc, LANES), seq(lc, 2 * RET_DV)]
    for t in tabs:
        args.append(t)
        specs.append(_resident((lt, LANES)))
    msum, qdec, kdec, cd = dec
    args += [msum, qdec, kdec, cd]
    specs += [
        pl.BlockSpec((1, 2, c, c), lambda bi, p: (p, 0, 0, 0)),
        pl.BlockSpec((1, 2, 2, c, LANES), lambda bi, p: (p, 0, 0, 0, 0)),
        pl.BlockSpec((1, 2, 2, c, LANES), lambda bi, p: (p, 0, 0, 0, 0)),
        pl.BlockSpec((1, 2, 2, 1, RET_DV), lambda bi, p: (p, 0, 0, 0, 0)),
    ]
    out_shape = [jax.ShapeDtypeStruct((b, l, RET_HEADS * RET_DV), BF16)]
    out_specs = [seq(l, 2 * RET_DV)]
    if need_ctx:
        out_shape.append(jax.ShapeDtypeStruct((b, lc, RET_HEADS * RET_DV), BF16))
        out_specs.append(seq(lc, 2 * RET_DV))
    res = pl.pallas_call(
        functools.partial(_ret_body, need_ctx=need_ctx),
        grid=(b, npair),
        in_specs=specs,
        out_specs=out_specs,
        out_shape=out_shape,
        scratch_shapes=[
            pltpu.VMEM((lt, LANES), F32),
            pltpu.VMEM((lt, LANES), F32),
            pltpu.VMEM((lt, 2 * RET_DV), BF16),
            pltpu.VMEM((nt, 2, 2 * LANES, RET_DV), F32),
            pltpu.VMEM((nt, 2, 2 * LANES, RET_DV), BF16),
        ],
        compiler_params=_params(("parallel", "parallel")),
    )(*args)
    return (res[0], res[1]) if need_ctx else (res[0], None)


FFN_CHUNKS = ((0, 1536), (1536, D_FF))
assert all(lo % MXU_TILE == 0 and hi % MXU_TILE == 0 for lo, hi in FFN_CHUNKS)


def _mix_ffn_body(x_ref, rec_ref, gl_ref, yb_ref, rn_ref, gr_ref, gm_ref, mod_ref,
                  woa_ref, wob_ref, woc_ref, wout_ref, g_ref, w1_ref, w3_ref, w2_ref, *rest, final):
    if final:
        gf_ref, o_ref = rest
    else:
        (o_ref,) = rest
    d = x_ref.shape[2]
    ga_a, sh_f, sc_f, ga_f = (mod_ref[0, j:j + 1, :] for j in range(4))

    ya = (jax.nn.gelu(gl_ref[0].astype(F32)) * rec_ref[0].astype(F32)).astype(BF16)
    yc = (jax.nn.silu(gr_ref[0].astype(F32)) * rn_ref[0].astype(F32)).astype(BF16)
    yb = jnp.concatenate([yb_ref[0, p] for p in range(yb_ref.shape[1])], axis=1)
    gm = gm_ref[0]
    m = jax.nn.sigmoid(gm[:, :d].astype(F32)) * jnp.dot(ya, woa_ref[...], preferred_element_type=F32)
    m = m + jax.nn.sigmoid(gm[:, d:2 * d].astype(F32)) * jnp.dot(
        yb, wob_ref[...], preferred_element_type=F32)
    m = m + jax.nn.sigmoid(gm[:, 2 * d:].astype(F32)) * jnp.dot(
        yc, woc_ref[...], preferred_element_type=F32)
    x = x_ref[0] + ga_a * jnp.dot(m.astype(BF16), wout_ref[...], preferred_element_type=F32)

    hb = (_rms(x, g_ref[...]) * (1.0 + sc_f) + sh_f).astype(BF16)
    acc = None
    for c0, c1 in FFN_CHUNKS:
        a = jnp.dot(hb, w1_ref[:, c0:c1], preferred_element_type=F32)
        bgate = jnp.dot(hb, w3_ref[:, c0:c1], preferred_element_type=F32)
        t = (jax.nn.silu(a) * bgate).astype(BF16)
        part = jnp.dot(t, w2_ref[c0:c1, :], preferred_element_type=F32)
        acc = part if acc is None else acc + part
    y = x + ga_f * acc
    if final:
        y = _rms(y, gf_ref[...])
    o_ref[0] = y


def _mix_ffn(x, rec, gl, yb, rn, gr, gm, mod4, woa, wob, woc, wout, g, w1, w3, w2, li, g_final, tm):
    b, l, d = x.shape
    final = g_final is not None
    row = lambda wd: pl.BlockSpec((1, tm, wd), lambda bi, i: (bi, i, 0))
    mod_map = (lambda bi, i: (bi, 0, 0)) if mod4.shape[0] > 1 else (lambda bi, i: (0, 0, 0))
    args = [x, rec, gl, yb, rn, gr, gm, mod4, woa, wob, woc, wout, g.reshape(1, d), w1, w3, w2]
    specs = [
        row(d), row(LRU_W), row(LRU_W),
        pl.BlockSpec((1, MLA_HEADS // 2, tm, LANES), lambda bi, i: (bi, 0, i, 0)),
        row(RET_HEADS * RET_DV), row(RET_HEADS * RET_DV), row(3 * d),
        pl.BlockSpec((1, 4, d), mod_map),
        _layer_spec(woa, li), _layer_spec(wob, li), _layer_spec(woc, li), _layer_spec(wout, li),
        _resident((1, d)),
        _layer_spec(w1, li), _layer_spec(w3, li), _layer_spec(w2, li),
    ]
    if final:
        args.append(g_final.reshape(1, d))
        specs.append(_resident((1, d)))
    return pl.pallas_call(
        functools.partial(_mix_ffn_body, final=final),
        grid=(b, l // tm),
        in_specs=specs,
        out_specs=row(d),
        out_shape=jax.ShapeDtypeStruct((b, l, d), F32),
        compiler_params=_params(("parallel", "parallel")),
    )(*args)


def _swap_half(w):
    half = w.shape[-1] // 2
    return jnp.concatenate([-w[..., half:], w[..., :half]], axis=-1)


def _arrange_w_in(w_in):
    w_in = w_in.astype(BF16)
    lead = w_in.shape[:-1]
    o = 0
    parts = {}
    for name, n in (("lru_x", LRU_W), ("ckv", KV_LORA), ("kr", MLA_ROPE),
                    ("ret_k", RET_HEADS * RET_DK), ("ret_v", RET_HEADS * RET_DV),
                    ("cq", Q_LORA), ("ret_q", RET_HEADS * RET_DK), ("g_lru", LRU_W),
                    ("g_ret", RET_HEADS * RET_DV), ("gm", 3 * D_MODEL)):
        parts[name] = w_in[..., o:o + n]
        o += n
    z = lambda n: jnp.zeros(lead + (n,), w_in.dtype)
    rope_a = jnp.concatenate([z(MLA_NOPE), parts["kr"], z(LANES - MLA_NOPE - MLA_ROPE)], axis=-1)
    rope_b = jnp.concatenate([z(MLA_NOPE), _swap_half(parts["kr"]),
                              z(LANES - MLA_NOPE - MLA_ROPE)], axis=-1)
    cols = [parts["lru_x"], parts["ckv"], rope_a, rope_b, parts["ret_k"], parts["ret_v"],
            parts["cq"], parts["ret_q"], parts["g_lru"], parts["g_ret"], parts["gm"]]
    return jnp.concatenate(cols, axis=-1)


def _arrange_mla(w_uq, w_ukv):
    nh = MLA_HEADS
    wq = w_uq.reshape(Q_LORA, nh, MLA_NOPE + MLA_ROPE)
    padq = jnp.zeros((Q_LORA, nh, LANES - MLA_NOPE - MLA_ROPE), w_uq.dtype)
    wq_arr = jnp.concatenate([wq, padq], axis=-1).reshape(Q_LORA, nh * LANES)
    wqs = jnp.concatenate([jnp.zeros((Q_LORA, nh, MLA_NOPE), w_uq.dtype),
                           _swap_half(wq[..., MLA_NOPE:]), padq], axis=-1)
    wqs_arr = wqs.reshape(Q_LORA, nh * LANES)
    wkv = w_ukv.reshape(KV_LORA, nh, MLA_NOPE + MLA_V)
    zk = jnp.zeros((KV_LORA, nh, LANES - MLA_NOPE), w_ukv.dtype)
    wk_arr = jnp.concatenate([wkv[..., :MLA_NOPE], zk], axis=-1).reshape(KV_LORA, nh * LANES)
    wv = wkv[..., MLA_NOPE:]
    zv = jnp.zeros_like(wv)
    even = (jnp.arange(nh) % 2 == 0)[None, :, None]
    wv_arr = jnp.concatenate([jnp.where(even, wv, zv), jnp.where(even, zv, wv)], axis=-1)
    wv_arr = wv_arr.reshape(KV_LORA, nh * LANES)
    return (wq_arr.astype(BF16), wqs_arr.astype(BF16),
            jnp.concatenate([wk_arr, wv_arr], axis=1).astype(BF16))


def _arrange_lru(wa, ba, wx, bx):
    def dense(wblk):
        eye = jnp.eye(LRU_BLOCKS, dtype=wblk.dtype)
        full = jnp.einsum("nde,nm->ndme", wblk, eye)
        return full.reshape(LRU_W, LRU_W)

    wg = jnp.stack([jnp.concatenate([dense(wa[d]), dense(wx[d])], axis=1) for d in range(2)])
    bg = jnp.stack([jnp.concatenate([ba[d], bx[d]])[None, :] for d in range(2)])
    return wg.astype(BF16), bg


def _mla_tables(n_tokens, rotate):
    scale = (MLA_NOPE + MLA_ROPE) ** -0.5 * math.log2(math.e)
    pad = LANES - MLA_NOPE - MLA_ROPE
    ones = jnp.ones((n_tokens, MLA_NOPE), F32)
    zeros = jnp.zeros((n_tokens, MLA_NOPE), F32)
    zpad = jnp.zeros((n_tokens, pad), F32)
    if rotate:
        rows = n_tokens // GRID_W
        rowi = jnp.repeat(jnp.arange(rows, dtype=F32), GRID_W)
        coli = jnp.tile(jnp.arange(GRID_W, dtype=F32), rows)
        n_freq = MLA_ROPE // 4
        inv = jnp.power(ROPE_BASE, -jnp.arange(n_freq, dtype=F32) / n_freq)
        ang = jnp.concatenate([rowi[:, None] * inv, coli[:, None] * inv], axis=-1)
        cos, sin = jnp.cos(ang), jnp.sin(ang)
    else:
        cos = jnp.ones((n_tokens, MLA_ROPE // 2), F32)
        sin = jnp.zeros((n_tokens, MLA_ROPE // 2), F32)
    cosq = jnp.concatenate([ones, cos, cos, zpad], axis=1) * scale
    sinq = jnp.concatenate([zeros, sin, sin, zpad], axis=1) * scale
    cosk = jnp.concatenate([zeros, cos, cos, zpad], axis=1)
    sink = jnp.concatenate([zeros, sin, sin, zpad], axis=1)
    return cosq, sinq, cosk, sink


def _ret_tables(n):
    theta = 1.0 / jnp.power(10000.0, jnp.linspace(0.0, 1.0, RET_DK // 2, dtype=F32))
    pos = jnp.arange(n, dtype=F32)
    ang = pos[:, None] * theta
    cos, sin = jnp.cos(ang), jnp.sin(ang)
    reps = LANES // RET_DK
    cos_t = jnp.tile(jnp.concatenate([cos, cos], axis=1), (1, reps))
    sin_t = jnp.tile(jnp.concatenate([-sin, sin], axis=1), (1, reps))
    return cos_t, sin_t


def _ret_decay_tables():
    c = RET_CHUNK
    h = jnp.arange(RET_HEADS, dtype=F32)
    lgs = jnp.stack([jnp.log1p(-jnp.exp2(-5.0 - h)), jnp.log1p(-jnp.exp2(-5.5 - h))], axis=1)
    pos = jnp.arange(c, dtype=F32)
    diff = pos[:, None] - pos[None, :]
    lg = lgs[:, :, None, None]
    inner = jnp.where(diff >= 0, jnp.exp(lg * jnp.maximum(diff, 0.0)), 0.0)
    msum = inner[:, 0] + jnp.swapaxes(inner[:, 1], -1, -2)
    lg1 = lgs[:, :, None]
    qd_f = jnp.exp(lg1 * (pos + 1.0))
    kd_f = jnp.exp(lg1 * (c - 1.0 - pos))
    qd = jnp.stack([qd_f[:, 0], qd_f[:, 1, ::-1]], axis=1)
    kd = jnp.stack([kd_f[:, 0], kd_f[:, 1, ::-1]], axis=1)
    lane_head = (jnp.arange(LANES) // RET_DK)
    e_of_h = jnp.arange(RET_HEADS) % 2
    mask = (lane_head[None, :] == e_of_h[:, None]).astype(F32)
    qdec = qd[..., None] * mask[:, None, None, :]
    kdec = kd[..., None] * mask[:, None, None, :]
    cd = jnp.broadcast_to(jnp.exp(lgs * c)[:, :, None, None], (RET_HEADS, 2, 1, RET_DV))
    pair = lambda t: t.reshape(RET_HEADS // 2, 2, *t.shape[1:])
    return pair(msum), pair(qdec), pair(kdec), pair(cd)


def _row_tile(n, pref):
    t = min(pref, n)
    while n % t:
        t //= 2
    return t


def kernel(x, c, ctx, c_ctx, w_mod, b_mod, g_mix, g_ffn, w_in, conv_w, conv_b, lru_wa, lru_ba,
           lru_wx, lru_bx, lru_lam, g_q, w_uq, g_kv, w_ukv, w_oa, w_ob, w_oc, w_out,
           w_ff1, w_ff3, w_ff2, g_final):
    b, l, d = x.shape
    lc = ctx.shape[1]
    depth = w_mod.shape[0]
    assert d == D_MODEL and l % LRU_TC == 0 and lc % LRU_TC == 0 and l % GRID_W == 0

    tm = _row_tile(l, 512)
    tmc = _row_tile(lc, 256)
    tq = _row_tile(l, 256)
    tqc = _row_tile(lc, 256)

    lat_tabs = _mla_tables(l, True)
    ctx_tabs = _mla_tables(lc, False)
    ret_tabs = _ret_tables(lc + l)
    dec = _ret_decay_tables()

    rows = -(-(b + 1) // SUBLANES) * SUBLANES
    cvec = jnp.zeros((rows, d), F32).at[:b].set(c).at[b].set(c_ctx)
    mod_all = _modulation(cvec, w_mod, b_mod)

    w_arr = _arrange_w_in(w_in)
    woa, wob, woc, wout = (w.astype(BF16) for w in (w_oa, w_ob, w_oc, w_out))
    w1, w3, w2 = (w.astype(BF16) for w in (w_ff1, w_ff3, w_ff2))

    xc = ctx
    for li in range(depth):
        last = li == depth - 1
        mod = mod_all[li, :b].reshape(b, 1, 6, d)
        sh_a, sc_a = mod[:, :, 0], mod[:, :, 1]
        mc = mod_all[li, b].reshape(1, 1, 6, d)
        mcs = [mc[:, :, j] for j in range(6)]

        wq_arr, wqs_arr, wkv_arr = _arrange_mla(w_uq[li], w_ukv[li])
        wg, bg = _arrange_lru(lru_wa[li], lru_ba[li], lru_wx[li], lru_bx[li])

        z = _inproj(x, g_mix[li], sh_a, sc_a, w_arr, li, KV_GROUPS + Q_GROUPS, tm)
        u, ckv, ra, rb, rk, rv, cq, rq, g_lru, g_ret, gm = z
        if last:
            zc = _inproj(xc, g_mix[li], mcs[0], mcs[1], w_arr, li, KV_GROUPS, tmc)
            uc, ckvc, rac, _, rkc, rvc = zc
            cqc = rqc = None
        else:
            zc = _inproj(xc, g_mix[li], mcs[0], mcs[1], w_arr, li, KV_GROUPS + Q_GROUPS, tmc)
            uc, ckvc, rac, _, rkc, rvc, cqc, rqc, g_lru_c, g_ret_c, gm_c = zc
        need_ctx = not last

        rec, rec_c = _lru(u, uc, conv_w[li], conv_b[li], wg, bg, lru_lam[li], need_ctx)

        q, k, v = _mla_prep(cq, ckv, ra, rb, *lat_tabs, g_q[li], wq_arr, wqs_arr, g_kv[li],
                            wkv_arr, True, True, tm)
        qc, kc, vc = _mla_prep(cqc, ckvc, rac, None, *ctx_tabs, g_q[li], wq_arr, None, g_kv[li],
                               wkv_arr, False, need_ctx, tmc)
        yb = _attention(q, [(kc, vc), (k, v)], tq)

        rn, rn_c = _retention(rq, rk, rv, rqc, rkc, rvc, ret_tabs, dec, need_ctx)

        x = _mix_ffn(x, rec, g_lru, yb, rn, g_ret, gm, mod[:, 0, 2:], woa, wob, woc, wout,
                     g_ffn[li], w1, w3, w2, li, g_final if last else None, tm)
        if need_ctx:
            yb_c = _attention(qc, [(kc, vc)], tqc)
            xc = _mix_ffn(xc, rec_c, g_lru_c, yb_c, rn_c, g_ret_c, gm_c, mc[:, 0, 2:], woa, wob, woc, wout,
                          g_ffn[li], w1, w3, w2, li, None, tmc)
    return x
```

```python
import functools
import math

import jax
import jax.numpy as jnp
from jax import lax
from jax.experimental import pallas as pl
from jax.experimental.pallas import tpu as pltpu

F32 = jnp.float32
BF16 = jnp.bfloat16

D_MODEL = 1024
EPS = 1e-6
GRID_W = 64
LRU_W = 512
LRU_BLOCKS = 8
LRU_BLOCK = LRU_W // LRU_BLOCKS
CONV_W = 4
CONV_PAD_L = 2
LRU_C = 8.0
MLA_HEADS = 8
MLA_NOPE = 64
MLA_ROPE = 32
MLA_V = 64
Q_LORA = 384
KV_LORA = 256
ROPE_BASE = 10000.0
RET_HEADS = 4
RET_DK = 64
RET_DV = 128
RET_CHUNK = 256
D_FF = 2816

LANES = 128
SUBLANES = 8
MXU_TILE = 256
VMEM_LIMIT = 56 * 1024 * 1024

KV_GROUPS = (LRU_W, KV_LORA, LANES, LANES, RET_HEADS * RET_DK, RET_HEADS * RET_DV)
Q_GROUPS = (Q_LORA, RET_HEADS * RET_DK, LRU_W, RET_HEADS * RET_DV, 3 * D_MODEL)
N_KV_ARR = sum(KV_GROUPS)


def _params(sem, flags=None):
    return pltpu.CompilerParams(dimension_semantics=sem, vmem_limit_bytes=VMEM_LIMIT, flags=flags)


def _resident(shape):
    nd = len(shape)
    return pl.BlockSpec(shape, lambda *_: (0,) * nd, pipeline_mode=pl.Buffered(1))


def _layer_spec(stacked, li, cols=None):
    shape = stacked.shape[1:] if cols is None else stacked.shape[1:-1] + (cols,)
    nd = len(shape)
    return pl.BlockSpec((None,) + shape, lambda *_: (li,) + (0,) * nd, pipeline_mode=pl.Buffered(1))


def _rms(x, g):
    y = x * lax.rsqrt(jnp.mean(x * x, axis=-1, keepdims=True) + EPS)
    return y * g


def _mod_body(c_ref, w_ref, b_ref, o_ref):
    cv = c_ref[...]
    act = cv * jax.nn.sigmoid(cv)
    o_ref[0] = jnp.dot(act, w_ref[0], precision=lax.Precision.HIGHEST,
                       preferred_element_type=F32) + b_ref[0]


def _modulation(cvec, w_mod, b_mod):
    depth, d, n = w_mod.shape
    rows = cvec.shape[0]
    tn = 1536
    return pl.pallas_call(
        _mod_body,
        grid=(depth, n // tn),
        in_specs=[
            pl.BlockSpec((rows, d), lambda l, j: (0, 0)),
            pl.BlockSpec((1, d, tn), lambda l, j: (l, 0, j)),
            pl.BlockSpec((1, 1, tn), lambda l, j: (l, 0, j)),
        ],
        out_specs=pl.BlockSpec((1, rows, tn), lambda l, j: (l, 0, j)),
        out_shape=jax.ShapeDtypeStruct((depth, rows, n), F32),
        compiler_params=_params(("arbitrary", "arbitrary")),
    )(cvec, w_mod, b_mod.reshape(depth, 1, n))


def _inproj_body(x_ref, g_ref, sh_ref, sc_ref, w_ref, *out_refs, widths):
    x = x_ref[0]
    h = _rms(x, g_ref[...]) * (1.0 + sc_ref[0]) + sh_ref[0]
    hb = h.astype(BF16)
    start, off = 0, 0
    for i, wd in enumerate(widths):
        off += wd
        if off % MXU_TILE == 0 or i == len(widths) - 1:
            base = sum(widths[:start])
            z = jnp.dot(hb, w_ref[:, base:off], preferred_element_type=F32)
            col = 0
            for j in range(start, i + 1):
                out_refs[j][0] = z[:, col:col + widths[j]].astype(out_refs[j].dtype)
                col += widths[j]
            start = i + 1


def _inproj(x, g, shift, scale, w_all, li, widths, tm):
    b, l, d = x.shape
    n = sum(widths)
    mod_map = (lambda bi, i: (bi, 0, 0)) if shift.shape[0] > 1 else (lambda bi, i: (0, 0, 0))
    return pl.pallas_call(
        functools.partial(_inproj_body, widths=widths),
        grid=(b, l // tm),
        in_specs=[
            pl.BlockSpec((1, tm, d), lambda bi, i: (bi, i, 0)),
            _resident((1, d)),
            pl.BlockSpec((1, 1, d), mod_map),
            pl.BlockSpec((1, 1, d), mod_map),
            _layer_spec(w_all, li, n),
        ],
        out_specs=[pl.BlockSpec((1, tm, wd), lambda bi, i: (bi, i, 0)) for wd in widths],
        out_shape=[jax.ShapeDtypeStruct((b, l, wd), BF16) for wd in widths],
        compiler_params=_params(("parallel", "parallel")),
    )(x, g.reshape(1, d), shift, scale, w_all)


LRU_TC = 256
LRU_HALO = SUBLANES


def _lru_body(u_ref, uc_ref, cw_ref, cb_ref, wg_ref, bg_ref, lam_ref, *rest, need_ctx):
    if need_ctx:
        out_ref, outc_ref, upad, upadc, ucv, ucvc, hbuf = rest
    else:
        out_ref, upad, upadc, ucv, ucvc, hbuf = rest
        outc_ref = None
    l = u_ref.shape[1]
    lc = uc_ref.shape[1]
    w = u_ref.shape[2]
    tc = LRU_TC
    nt = tc // SUBLANES

    zeros_halo = jnp.zeros((LRU_HALO, w), F32)
    for pad_ref, cv_ref, src_ref, n in ((upad, ucv, u_ref, l), (upadc, ucvc, uc_ref, lc)):
        pad_ref[0:LRU_HALO, :] = zeros_halo
        pad_ref[LRU_HALO + n:2 * LRU_HALO + n, :] = zeros_halo

        def copy(i, carry, pad_ref=pad_ref, src_ref=src_ref):
            t0 = pl.multiple_of(i * tc, tc)
            pad_ref[pl.ds(LRU_HALO + t0, tc), :] = src_ref[0, pl.ds(t0, tc), :].astype(F32)
            return carry

        lax.fori_loop(0, n // tc, copy, 0)

        def conv(i, carry, pad_ref=pad_ref, cv_ref=cv_ref):
            t0 = pl.multiple_of(i * tc, tc)
            win = pad_ref[pl.ds(t0, tc + 2 * LRU_HALO), :]
            u = cb_ref[...]
            for k in range(CONV_W):
                sh = (CONV_PAD_L - k) % (tc + 2 * LRU_HALO)
                tap = win if sh == 0 else pltpu.roll(win, sh, 0)
                u = u + cw_ref[k:k + 1, :] * tap[LRU_HALO:LRU_HALO + tc]
            cv_ref[pl.ds(t0, tc), :] = u
            return carry

        lax.fori_loop(0, n // tc, conv, 0)

    row = lax.broadcasted_iota(jnp.int32, (nt, SUBLANES, w), 1)

    def chunk_scan(cv_ref, t0, d, h):
        u = cv_ref[pl.ds(t0, tc), :]
        g = jnp.dot(u.astype(BF16), wg_ref[d], preferred_element_type=F32) + bg_ref[d]
        r = jax.nn.sigmoid(g[:, :w])
        gi = jax.nn.sigmoid(g[:, w:])
        z = -lam_ref[d]
        sp = jnp.maximum(z, 0.0) + jnp.log1p(jnp.exp(-jnp.abs(z)))
        log_a = r * (-LRU_C * sp)
        a = jnp.exp2(r * (-LRU_C * math.log2(math.e) * sp))
        th = jnp.tanh(log_a)
        om = -2.0 * th / (1.0 - th)
        root = jnp.where(om > 0.0, om * lax.rsqrt(om), 0.0)
        bb = root * (gi * u)
        a = a.reshape(nt, SUBLANES, w)
        bb = bb.reshape(nt, SUBLANES, w)
        s = 1
        while s < SUBLANES:
            if d == 0:
                keep = row >= s
                shift = s
            else:
                keep = row < SUBLANES - s
                shift = SUBLANES - s
            a_s = jnp.where(keep, pltpu.roll(a, shift, 1), 1.0)
            b_s = jnp.where(keep, pltpu.roll(bb, shift, 1), 0.0)
            bb = a * b_s + bb
            a = a * a_s
            s *= 2
        order = range(nt) if d == 0 else range(nt - 1, -1, -1)
        for k in order:
            hk = a[k] * h + bb[k]
            hbuf[k * SUBLANES:(k + 1) * SUBLANES, :] = hk
            h = hk[SUBLANES - 1:SUBLANES] if d == 0 else hk[0:1]
        return h

    def run(cv_ref, n, d, h, write):
        nchunks = n // tc

        def body(i, h):
            ci = i if d == 0 else nchunks - 1 - i
            t0 = pl.multiple_of(ci * tc, tc)
            h = chunk_scan(cv_ref, t0, d, h)
            write(t0)
            return h

        return lax.fori_loop(0, nchunks, body, h, unroll=2)

    def write_accc(t0):
        upadc[pl.ds(t0, tc), :] = hbuf[...]

    def write_acc(t0):
        upad[pl.ds(t0, tc), :] = hbuf[...]

    def write_outc(t0):
        if outc_ref is not None:
            outc_ref[0, pl.ds(t0, tc), :] = (upadc[pl.ds(t0, tc), :] + hbuf[...]).astype(outc_ref.dtype)

    def write_out(t0):
        out_ref[0, pl.ds(t0, tc), :] = (upad[pl.ds(t0, tc), :] + hbuf[...]).astype(out_ref.dtype)

    h0 = jnp.zeros((1, w), F32)
    h = run(ucvc, lc, 0, h0, write_accc)
    run(ucv, l, 0, h, write_acc)
    h = run(ucvc, lc, 1, h0, write_outc)
    run(ucv, l, 1, h, write_out)


def _lru(u, uc, conv_w, conv_b, wg, bg, lam, need_ctx):
    b, l, w = u.shape
    lc = uc.shape[1]
    out_shape = [jax.ShapeDtypeStruct((b, l, w), BF16)]
    out_specs = [pl.BlockSpec((1, l, w), lambda bi: (bi, 0, 0))]
    if need_ctx:
        out_shape.append(jax.ShapeDtypeStruct((b, lc, w), BF16))
        out_specs.append(pl.BlockSpec((1, lc, w), lambda bi: (bi, 0, 0)))
    res = pl.pallas_call(
        functools.partial(_lru_body, need_ctx=need_ctx),
        grid=(b,),
        in_specs=[
            pl.BlockSpec((1, l, w), lambda bi: (bi, 0, 0)),
            pl.BlockSpec((1, lc, w), lambda bi: (bi, 0, 0)),
            _resident((CONV_W, w)),
            _resident((1, w)),
            _resident((2, w, 2 * w)),
            _resident((2, 1, 2 * w)),
            _resident((2, 1, w)),
        ],
        out_specs=out_specs,
        out_shape=out_shape,
        scratch_shapes=[
            pltpu.VMEM((l + 2 * LRU_HALO, w), F32),
            pltpu.VMEM((lc + 2 * LRU_HALO, w), F32),
            pltpu.VMEM((l, w), F32),
            pltpu.VMEM((lc, w), F32),
            pltpu.VMEM((LRU_TC, w), F32),
        ],
        compiler_params=_params(("parallel",)),
    )(u, uc, conv_w, conv_b.reshape(1, w), wg, bg, lam.reshape(2, 1, w))
    return (res[0], res[1]) if need_ctx else (res[0], None)


def _mla_prep_body(*refs, rotate, need_q):
    it = iter(refs)
    cq_ref = next(it) if need_q else None
    ckv_ref, ra_ref = next(it), next(it)
    rb_ref = next(it) if rotate else None
    if need_q:
        cosq_ref = next(it)
        sinq_ref = next(it) if rotate else None
    cosk_ref = next(it)
    sink_ref = next(it) if rotate else None
    if need_q:
        gq_ref, wq_ref = next(it), next(it)
        wqs_ref = next(it) if rotate else None
    gkv_ref, wkv_ref = next(it), next(it)
    q_out = next(it) if need_q else None
    k_out, v_out = next(it), next(it)
    nh = MLA_HEADS
    tm = ckv_ref.shape[1]

    if need_q:
        cqn = _rms(cq_ref[0].astype(F32), gq_ref[...]).astype(BF16)
        qm = jnp.dot(cqn, wq_ref[...], preferred_element_type=F32)
        cosq = cosq_ref[...]
        if rotate:
            qs = jnp.dot(cqn, wqs_ref[...], preferred_element_type=F32)
            sinq = sinq_ref[...]
        for h in range(nh):
            sl = slice(h * LANES, (h + 1) * LANES)
            qh = qm[:, sl] * cosq
            if rotate:
                qh = qh + qs[:, sl] * sinq
            q_out[0, h] = qh.astype(q_out.dtype)

    kvn = _rms(ckv_ref[0].astype(F32), gkv_ref[...]).astype(BF16)
    kv = jnp.dot(kvn, wkv_ref[...], preferred_element_type=F32)
    rope = ra_ref[0].astype(F32) * cosk_ref[...]
    if rotate:
        rope = rope + rb_ref[0].astype(F32) * sink_ref[...]
    lane = lax.broadcasted_iota(jnp.int32, (tm, LANES), 1)
    for h in range(nh):
        k_out[0, h] = (kv[:, h * LANES:(h + 1) * LANES] + rope).astype(k_out.dtype)
        vh = kv[:, (nh + h) * LANES:(nh + h + 1) * LANES]
        v_out[0, h] = jnp.where(lane == _sum_lane(h), 1.0, vh).astype(v_out.dtype)


def _sum_lane(h):
    return MLA_V if h % 2 == 0 else 0


def _mla_prep(cq, ckv, ra, rb, cosq, sinq, cosk, sink, gq, wq, wqs, gkv, wkv, rotate, need_q, tm):
    b, l, _ = ckv.shape
    nh = MLA_HEADS
    row = lambda wd: pl.BlockSpec((1, tm, wd), lambda bi, i: (bi, i, 0))
    tab = pl.BlockSpec((tm, LANES), lambda bi, i: (i, 0))
    args, specs = [], []

    def add(a, s):
        args.append(a)
        specs.append(s)

    if need_q:
        add(cq, row(Q_LORA))
    add(ckv, row(KV_LORA))
    add(ra, row(LANES))
    if rotate:
        add(rb, row(LANES))
    if need_q:
        add(cosq, tab)
        if rotate:
            add(sinq, tab)
    add(cosk, tab)
    if rotate:
        add(sink, tab)
    if need_q:
        add(gq.reshape(1, Q_LORA), _resident((1, Q_LORA)))
        add(wq, _resident(wq.shape))
        if rotate:
            add(wqs, _resident(wqs.shape))
    add(gkv.reshape(1, KV_LORA), _resident((1, KV_LORA)))
    add(wkv, _resident(wkv.shape))

    head = pl.BlockSpec((1, nh, tm, LANES), lambda bi, i: (bi, 0, i, 0))
    hshape = jax.ShapeDtypeStruct((b, nh, l, LANES), ATTN_DT)
    n_out = 3 if need_q else 2
    res = pl.pallas_call(
        functools.partial(_mla_prep_body, rotate=rotate, need_q=need_q),
        grid=(b, l // tm),
        in_specs=specs,
        out_specs=[head] * n_out,
        out_shape=[hshape] * n_out,
        compiler_params=_params(("parallel", "parallel")),
    )(*args)
    return res if need_q else (None, res[0], res[1])


ATTN_PAIRS_PER_STEP = 4
ATTN_DT = jnp.float8_e4m3fn
ATTN_P_SHIFT = 8.0


def _attn_body(q_ref, *refs, n_kv, tq):
    kv_refs = refs[:2 * n_kv]
    o_ref = refs[2 * n_kv]
    s_bufs = refs[2 * n_kv + 1:2 * n_kv + 3]
    p_bufs = refs[2 * n_kv + 3:2 * n_kv + 5]
    npair = ATTN_PAIRS_PER_STEP
    n_items = (q_ref.shape[2] // tq) * npair
    lks = [kv_refs[2 * j].shape[2] for j in range(n_kv)]
    offs = [sum(lks[:j]) for j in range(n_kv)]
    lane = lax.broadcasted_iota(jnp.int32, (tq, LANES), 1)
    nt_dims = (((1,), (1,)), ((), ()))

    s_bufs[1][...] = jnp.zeros(s_bufs[1].shape, s_bufs[1].dtype)
    p_bufs[0][...] = jnp.zeros(p_bufs[0].shape, p_bufs[0].dtype)

    def item(t):
        tile = lax.shift_right_logical(t, npair.bit_length() - 1)
        pair = jnp.bitwise_and(t, npair - 1)
        return pl.multiple_of(tile * tq, tq), pair

    def stages(t, slot):
        s_a, s_b = s_bufs[slot], s_bufs[1 - slot]
        p_b, p_c = p_bufs[1 - slot], p_bufs[slot]

        row_a, pair_a = item(jnp.minimum(t, n_items - 1))
        for e in range(2):
            h = pair_a * 2 + e
            q = q_ref[0, h, pl.ds(row_a, tq), :]
            for j in range(n_kv):
                s_a[e, :, offs[j]:offs[j] + lks[j]] = lax.dot_general(
                    q, kv_refs[2 * j][0, h], nt_dims, preferred_element_type=F32).astype(s_a.dtype)

        for e in range(2):
            s = s_b[e]
            bm = s[:, 0:LANES]
            for c0 in range(LANES, s.shape[1], LANES):
                bm = jnp.maximum(bm, s[:, c0:c0 + LANES])
            m = bm.astype(F32).max(axis=-1, keepdims=True)
            shift = (m - ATTN_P_SHIFT).astype(s.dtype)
            p_b[e] = jnp.exp2(s - shift).astype(p_b.dtype)

        row_c, pair_c = item(jnp.maximum(t - 2, 0))
        halves = []
        for e in range(2):
            h = pair_c * 2 + e
            o = None
            for j in range(n_kv):
                oj = jnp.dot(p_c[e, :, offs[j]:offs[j] + lks[j]], kv_refs[2 * j + 1][0, h],
                             preferred_element_type=F32)
                o = oj if o is None else o + oj
            one = _sum_lane(e)
            halves.append(o * (1.0 / o[:, one:one + 1]))
        blk = jnp.where(lane < MLA_V, halves[0], halves[1])
        o_ref[0, pair_c, pl.ds(row_c, tq), :] = blk.astype(o_ref.dtype)

    def body(i, carry):
        stages(2 * i, 0)
        stages(2 * i + 1, 1)
        return carry

    lax.fori_loop(0, (n_items + 2) // 2, body, 0)


def _attention(q, kvs, tq):
    b, nh, lq, _ = q.shape
    hs = 2 * ATTN_PAIRS_PER_STEP
    once = lambda n: pl.BlockSpec((1, hs, n, LANES), lambda bi, j: (bi, j, 0, 0))
    specs = [once(lq)]
    args = [q]
    lkt = 0
    for k, v in kvs:
        lk = k.shape[2]
        lkt += lk
        specs += [once(lk), once(lk)]
        args += [k, v]
    return pl.pallas_call(
        functools.partial(_attn_body, n_kv=len(kvs), tq=tq),
        grid=(b, nh // hs),
        in_specs=specs,
        out_specs=pl.BlockSpec((1, ATTN_PAIRS_PER_STEP, lq, LANES), lambda bi, j: (bi, j, 0, 0)),
        out_shape=jax.ShapeDtypeStruct((b, nh // 2, lq, LANES), BF16),
        scratch_shapes=[
            pltpu.VMEM((2, tq, lkt), BF16),
            pltpu.VMEM((2, tq, lkt), BF16),
            pltpu.VMEM((2, tq, lkt), ATTN_DT),
            pltpu.VMEM((2, tq, lkt), ATTN_DT),
        ],
        compiler_params=_params(("parallel", "parallel")),
    )(*args)


def _ret_body(*refs, need_ctx):
    it = iter(refs)
    q_ref, k_ref, v_ref = next(it), next(it), next(it)
    qc_ref = next(it) if need_ctx else None
    kc_ref, vc_ref = next(it), next(it)
    cos_ref, sin_ref = next(it), next(it)
    msum_ref, qdec_ref, kdec_ref, cd_ref = next(it), next(it), next(it), next(it)
    o_ref = next(it)
    oc_ref = next(it) if need_ctx else None
    qa, ka, va, u_s, r_s = (next(it) for _ in range(5))

    c = RET_CHUNK
    l = q_ref.shape[1]
    lc = kc_ref.shape[1]
    ncc = lc // c
    nt = (lc + l) // c
    kscale = RET_DK ** -0.5
    lane = lax.broadcasted_iota(jnp.int32, (c, LANES), 1)
    first_half = (lane % RET_DK) < (RET_DK // 2)

    def rotated(src_ref, t0, g0):
        x = src_ref[0, pl.ds(t0, c), :].astype(F32)
        sw = jnp.where(first_half, pltpu.roll(x, LANES - RET_DK // 2, 1),
                       pltpu.roll(x, RET_DK // 2, 1))
        return x * cos_ref[pl.ds(g0, c), :] + sw * sin_ref[pl.ds(g0, c), :]

    def stage(qsrc, ksrc, vsrc, n, base):
        def body(i, carry):
            t0 = pl.multiple_of(i * c, c)
            g0 = pl.multiple_of(base + i * c, c)
            if qsrc is not None:
                qa[pl.ds(g0, c), :] = rotated(qsrc, t0, g0)
            ka[pl.ds(g0, c), :] = rotated(ksrc, t0, g0) * kscale
            va[pl.ds(g0, c), :] = vsrc[0, pl.ds(t0, c), :]
            return carry

        lax.fori_loop(0, n // c, body, 0, unroll=2)

    stage(qc_ref, kc_ref, vc_ref, lc, 0)
    stage(q_ref, k_ref, v_ref, l, lc)

    tn_dims = (((0,), (0,)), ((), ()))
    nt_dims = (((1,), (1,)), ((), ()))

    def incr(g, carry):
        g0 = pl.multiple_of(g * c, c)
        kf = ka[pl.ds(g0, c), :]
        for e in range(2):
            v = va[pl.ds(g0, c), e * RET_DV:(e + 1) * RET_DV]
            kd = jnp.concatenate([kf * kdec_ref[0, e, 0], kf * kdec_ref[0, e, 1]], axis=1)
            u_s[g, e] = lax.dot_general(kd.astype(BF16), v, tn_dims, preferred_element_type=F32)
        return carry

    lax.fori_loop(0, nt, incr, 0, unroll=4)

    zero = tuple(jnp.zeros((LANES, RET_DV), F32) for _ in range(4))

    def recur(t, st):
        gs = (t, jnp.where(t < ncc, ncc - 1 - t, nt - 1 + ncc - t))
        new = []
        for d in range(2):
            rows = slice(d * LANES, (d + 1) * LANES)
            for e in range(2):
                r = st[2 * d + e]
                r_s[gs[d], e, rows, :] = r.astype(r_s.dtype)
                new.append(cd_ref[0, e, d] * r + u_s[gs[d], e, rows, :])
        return tuple(new)

    lax.fori_loop(0, nt, recur, zero, unroll=2)

    def emit(dst_ref, gbase):
        def body(i, carry):
            t0 = pl.multiple_of(i * c, c)
            g = gbase + i
            g0 = pl.multiple_of(g * c, c)
            qf = qa[pl.ds(g0, c), :]
            kb = ka[pl.ds(g0, c), :].astype(BF16)
            parts = []
            for e in range(2):
                v = va[pl.ds(g0, c), e * RET_DV:(e + 1) * RET_DV]
                qm = jnp.where((lane // RET_DK) == e, qf, 0.0).astype(BF16)
                s = lax.dot_general(qm, kb, nt_dims, preferred_element_type=F32)
                o = jnp.dot((s * msum_ref[0, e]).astype(BF16), v, preferred_element_type=F32)
                qd = jnp.concatenate([qf * qdec_ref[0, e, 0], qf * qdec_ref[0, e, 1]], axis=1)
                o = o + jnp.dot(qd.astype(BF16), r_s[g, e], preferred_element_type=F32)
                mu = jnp.mean(o, axis=-1, keepdims=True)
                oc = o - mu
                var = jnp.mean(oc * oc, axis=-1, keepdims=True)
                parts.append(oc * lax.rsqrt(var + EPS))
            dst_ref[0, pl.ds(t0, c), :] = jnp.concatenate(parts, axis=1).astype(dst_ref.dtype)
            return carry

        return body

    if need_ctx:
        lax.fori_loop(0, ncc, emit(oc_ref, 0), 0, unroll=2)
    lax.fori_loop(0, nt - ncc, emit(o_ref, ncc), 0, unroll=8)


def _retention(q, k, v, qc, kc, vc, tabs, dec, need_ctx):
    b, l, _ = q.shape
    lc = kc.shape[1]
    lt = lc + l
    c = RET_CHUNK
    nt = lt // c
    npair = RET_HEADS // 2
    seq = lambda n, wd: pl.BlockSpec((1, n, wd), lambda bi, p: (bi, 0, p))
    args = [q, k, v]
    specs = [seq(l, LANES), seq(l, LANES), seq(l, 2 * RET_DV)]
    if need_ctx:
        args.append(qc)
        specs.append(seq(lc, LANES))
    args += [kc, vc]
    specs += [seq(lc, LANES), seq(lc, 2 * RET_DV)]
    for t in tabs:
        args.append(t)
        specs.append(_resident((lt, LANES)))
    msum, qdec, kdec, cd = dec
    args += [msum, qdec, kdec, cd]
    specs += [
        pl.BlockSpec((1, 2, c, c), lambda bi, p: (p, 0, 0, 0)),
        pl.BlockSpec((1, 2, 2, c, LANES), lambda bi, p: (p, 0, 0, 0, 0)),
        pl.BlockSpec((1, 2, 2, c, LANES), lambda bi, p: (p, 0, 0, 0, 0)),
        pl.BlockSpec((1, 2, 2, 1, RET_DV), lambda bi, p: (p, 0, 0, 0, 0)),
    ]
    out_shape = [jax.ShapeDtypeStruct((b, l, RET_HEADS * RET_DV), BF16)]
    out_specs = [seq(l, 2 * RET_DV)]
    if need_ctx:
        out_shape.append(jax.ShapeDtypeStruct((b, lc, RET_HEADS * RET_DV), BF16))
        out_specs.append(seq(lc, 2 * RET_DV))
    res = pl.pallas_call(
        functools.partial(_ret_body, need_ctx=need_ctx),
        grid=(b, npair),
        in_specs=specs,
        out_specs=out_specs,
        out_shape=out_shape,
        scratch_shapes=[
            pltpu.VMEM((lt, LANES), F32),
            pltpu.VMEM((lt, LANES), F32),
            pltpu.VMEM((lt, 2 * RET_DV), BF16),
            pltpu.VMEM((nt, 2, 2 * LANES, RET_DV), F32),
            pltpu.VMEM((nt, 2, 2 * LANES, RET_DV), BF16),
        ],
        compiler_params=_params(("parallel", "parallel")),
    )(*args)
    return (res[0], res[1]) if need_ctx else (res[0], None)


FFN_CHUNKS = ((0, 1536), (1536, D_FF))
assert all(lo % MXU_TILE == 0 and hi % MXU_TILE == 0 for lo, hi in FFN_CHUNKS)


def _mix_ffn_body(x_ref, rec_ref, gl_ref, yb_ref, rn_ref, gr_ref, gm_ref, mod_ref,
                  woa_ref, wob_ref, woc_ref, wout_ref, g_ref, w1_ref, w3_ref, w2_ref, *rest, final):
    if final:
        gf_ref, o_ref = rest
    else:
        (o_ref,) = rest
    d = x_ref.shape[2]
    ga_a, sh_f, sc_f, ga_f = (mod_ref[0, j:j + 1, :] for j in range(4))

    ya = (jax.nn.gelu(gl_ref[0].astype(F32)) * rec_ref[0].astype(F32)).astype(BF16)
    yc = (jax.nn.silu(gr_ref[0].astype(F32)) * rn_ref[0].astype(F32)).astype(BF16)
    yb = jnp.concatenate([yb_ref[0, p] for p in range(yb_ref.shape[1])], axis=1)
    gm = gm_ref[0]
    m = jax.nn.sigmoid(gm[:, :d].astype(F32)) * jnp.dot(ya, woa_ref[...], preferred_element_type=F32)
    m = m + jax.nn.sigmoid(gm[:, d:2 * d].astype(F32)) * jnp.dot(
        yb, wob_ref[...], preferred_element_type=F32)
    m = m + jax.nn.sigmoid(gm[:, 2 * d:].astype(F32)) * jnp.dot(
        yc, woc_ref[...], preferred_element_type=F32)
    x = x_ref[0] + ga_a * jnp.dot(m.astype(BF16), wout_ref[...], preferred_element_type=F32)

    hb = (_rms(x, g_ref[...]) * (1.0 + sc_f) + sh_f).astype(BF16)
    acc = None
    for c0, c1 in FFN_CHUNKS:
        a = jnp.dot(hb, w1_ref[:, c0:c1], preferred_element_type=F32)
        bgate = jnp.dot(hb, w3_ref[:, c0:c1], preferred_element_type=F32)
        t = (jax.nn.silu(a) * bgate).astype(BF16)
        part = jnp.dot(t, w2_ref[c0:c1, :], preferred_element_type=F32)
        acc = part if acc is None else acc + part
    y = x + ga_f * acc
    if final:
        y = _rms(y, gf_ref[...])
    o_ref[0] = y


def _mix_ffn(x, rec, gl, yb, rn, gr, gm, mod4, woa, wob, woc, wout, g, w1, w3, w2, li, g_final, tm):
    b, l, d = x.shape
    final = g_final is not None
    row = lambda wd: pl.BlockSpec((1, tm, wd), lambda bi, i: (bi, i, 0))
    mod_map = (lambda bi, i: (bi, 0, 0)) if mod4.shape[0] > 1 else (lambda bi, i: (0, 0, 0))
    args = [x, rec, gl, yb, rn, gr, gm, mod4, woa, wob, woc, wout, g.reshape(1, d), w1, w3, w2]
    specs = [
        row(d), row(LRU_W), row(LRU_W),
        pl.BlockSpec((1, MLA_HEADS // 2, tm, LANES), lambda bi, i: (bi, 0, i, 0)),
        row(RET_HEADS * RET_DV), row(RET_HEADS * RET_DV), row(3 * d),
        pl.BlockSpec((1, 4, d), mod_map),
        _layer_spec(woa, li), _layer_spec(wob, li), _layer_spec(woc, li), _layer_spec(wout, li),
        _resident((1, d)),
        _layer_spec(w1, li), _layer_spec(w3, li), _layer_spec(w2, li),
    ]
    if final:
        args.append(g_final.reshape(1, d))
        specs.append(_resident((1, d)))
    return pl.pallas_call(
        functools.partial(_mix_ffn_body, final=final),
        grid=(b, l // tm),
        in_specs=specs,
        out_specs=row(d),
        out_shape=jax.ShapeDtypeStruct((b, l, d), F32),
        compiler_params=_params(("parallel", "parallel")),
    )(*args)


def _swap_half(w):
    half = w.shape[-1] // 2
    return jnp.concatenate([-w[..., half:], w[..., :half]], axis=-1)


def _arrange_w_in(w_in):
    w_in = w_in.astype(BF16)
    lead = w_in.shape[:-1]
    o = 0
    parts = {}
    for name, n in (("lru_x", LRU_W), ("ckv", KV_LORA), ("kr", MLA_ROPE),
                    ("ret_k", RET_HEADS * RET_DK), ("ret_v", RET_HEADS * RET_DV),
                    ("cq", Q_LORA), ("ret_q", RET_HEADS * RET_DK), ("g_lru", LRU_W),
                    ("g_ret", RET_HEADS * RET_DV), ("gm", 3 * D_MODEL)):
        parts[name] = w_in[..., o:o + n]
        o += n
    z = lambda n: jnp.zeros(lead + (n,), w_in.dtype)
    rope_a = jnp.concatenate([z(MLA_NOPE), parts["kr"], z(LANES - MLA_NOPE - MLA_ROPE)], axis=-1)
    rope_b = jnp.concatenate([z(MLA_NOPE), _swap_half(parts["kr"]),
                              z(LANES - MLA_NOPE - MLA_ROPE)], axis=-1)
    cols = [parts["lru_x"], parts["ckv"], rope_a, rope_b, parts["ret_k"], parts["ret_v"],
            parts["cq"], parts["ret_q"], parts["g_lru"], parts["g_ret"], parts["gm"]]
    return jnp.concatenate(cols, axis=-1)


def _arrange_mla(w_uq, w_ukv):
    nh = MLA_HEADS
    wq = w_uq.reshape(Q_LORA, nh, MLA_NOPE + MLA_ROPE)
    padq = jnp.zeros((Q_LORA, nh, LANES - MLA_NOPE - MLA_ROPE), w_uq.dtype)
    wq_arr = jnp.concatenate([wq, padq], axis=-1).reshape(Q_LORA, nh * LANES)
    wqs = jnp.concatenate([jnp.zeros((Q_LORA, nh, MLA_NOPE), w_uq.dtype),
                           _swap_half(wq[..., MLA_NOPE:]), padq], axis=-1)
    wqs_arr = wqs.reshape(Q_LORA, nh * LANES)
    wkv = w_ukv.reshape(KV_LORA, nh, MLA_NOPE + MLA_V)
    zk = jnp.zeros((KV_LORA, nh, LANES - MLA_NOPE), w_ukv.dtype)
    wk_arr = jnp.concatenate([wkv[..., :MLA_NOPE], zk], axis=-1).reshape(KV_LORA, nh * LANES)
    wv = wkv[..., MLA_NOPE:]
    zv = jnp.zeros_like(wv)
    even = (jnp.arange(nh) % 2 == 0)[None, :, None]
    wv_arr = jnp.concatenate([jnp.where(even, wv, zv), jnp.where(even, zv, wv)], axis=-1)
    wv_arr = wv_arr.reshape(KV_LORA, nh * LANES)
    return (wq_arr.astype(BF16), wqs_arr.astype(BF16),
            jnp.concatenate([wk_arr, wv_arr], axis=1).astype(BF16))


def _arrange_lru(wa, ba, wx, bx):
    def dense(wblk):
        eye = jnp.eye(LRU_BLOCKS, dtype=wblk.dtype)
        full = jnp.einsum("nde,nm->ndme", wblk, eye)
        return full.reshape(LRU_W, LRU_W)

    wg = jnp.stack([jnp.concatenate([dense(wa[d]), dense(wx[d])], axis=1) for d in range(2)])
    bg = jnp.stack([jnp.concatenate([ba[d], bx[d]])[None, :] for d in range(2)])
    return wg.astype(BF16), bg


def _mla_tables(n_tokens, rotate):
    scale = (MLA_NOPE + MLA_ROPE) ** -0.5 * math.log2(math.e)
    pad = LANES - MLA_NOPE - MLA_ROPE
    ones = jnp.ones((n_tokens, MLA_NOPE), F32)
    zeros = jnp.zeros((n_tokens, MLA_NOPE), F32)
    zpad = jnp.zeros((n_tokens, pad), F32)
    if rotate:
        rows = n_tokens // GRID_W
        rowi = jnp.repeat(jnp.arange(rows, dtype=F32), GRID_W)
        coli = jnp.tile(jnp.arange(GRID_W, dtype=F32), rows)
        n_freq = MLA_ROPE // 4
        inv = jnp.power(ROPE_BASE, -jnp.arange(n_freq, dtype=F32) / n_freq)
        ang = jnp.concatenate([rowi[:, None] * inv, coli[:, None] * inv], axis=-1)
        cos, sin = jnp.cos(ang), jnp.sin(ang)
    else:
        cos = jnp.ones((n_tokens, MLA_ROPE // 2), F32)
        sin = jnp.zeros((n_tokens, MLA_ROPE // 2), F32)
    cosq = jnp.concatenate([ones, cos, cos, zpad], axis=1) * scale
    sinq = jnp.concatenate([zeros, sin, sin, zpad], axis=1) * scale
    cosk = jnp.concatenate([zeros, cos, cos, zpad], axis=1)
    sink = jnp.concatenate([zeros, sin, sin, zpad], axis=1)
    return cosq, sinq, cosk, sink


def _ret_tables(n):
    theta = 1.0 / jnp.power(10000.0, jnp.linspace(0.0, 1.0, RET_DK // 2, dtype=F32))
    pos = jnp.arange(n, dtype=F32)
    ang = pos[:, None] * theta
    cos, sin = jnp.cos(ang), jnp.sin(ang)
    reps = LANES // RET_DK
    cos_t = jnp.tile(jnp.concatenate([cos, cos], axis=1), (1, reps))
    sin_t = jnp.tile(jnp.concatenate([-sin, sin], axis=1), (1, reps))
    return cos_t, sin_t


def _ret_decay_tables():
    c = RET_CHUNK
    h = jnp.arange(RET_HEADS, dtype=F32)
    lgs = jnp.stack([jnp.log1p(-jnp.exp2(-5.0 - h)), jnp.log1p(-jnp.exp2(-5.5 - h))], axis=1)
    pos = jnp.arange(c, dtype=F32)
    diff = pos[:, None] - pos[None, :]
    lg = lgs[:, :, None, None]
    inner = jnp.where(diff >= 0, jnp.exp(lg * jnp.maximum(diff, 0.0)), 0.0)
    msum = inner[:, 0] + jnp.swapaxes(inner[:, 1], -1, -2)
    lg1 = lgs[:, :, None]
    qd_f = jnp.exp(lg1 * (pos + 1.0))
    kd_f = jnp.exp(lg1 * (c - 1.0 - pos))
    qd = jnp.stack([qd_f[:, 0], qd_f[:, 1, ::-1]], axis=1)
    kd = jnp.stack([kd_f[:, 0], kd_f[:, 1, ::-1]], axis=1)
    lane_head = (jnp.arange(LANES) // RET_DK)
    e_of_h = jnp.arange(RET_HEADS) % 2
    mask = (lane_head[None, :] == e_of_h[:, None]).astype(F32)
    qdec = qd[..., None] * mask[:, None, None, :]
    kdec = kd[..., None] * mask[:, None, None, :]
    cd = jnp.broadcast_to(jnp.exp(lgs * c)[:, :, None, None], (RET_HEADS, 2, 1, RET_DV))
    pair = lambda t: t.reshape(RET_HEADS // 2, 2, *t.shape[1:])
    return pair(msum), pair(qdec), pair(kdec), pair(cd)


def _row_tile(n, pref):
    t = min(pref, n)
    while n % t:
        t //= 2
    return t


def kernel(x, c, ctx, c_ctx, w_mod, b_mod, g_mix, g_ffn, w_in, conv_w, conv_b, lru_wa, lru_ba,
           lru_wx, lru_bx, lru_lam, g_q, w_uq, g_kv, w_ukv, w_oa, w_ob, w_oc, w_out,
           w_ff1, w_ff3, w_ff2, g_final):
    b, l, d = x.shape
    lc = ctx.shape[1]
    depth = w_mod.shape[0]
    assert d == D_MODEL and l % LRU_TC == 0 and lc % LRU_TC == 0 and l % GRID_W == 0

    tm = _row_tile(l, 512)
    tmc = _row_tile(lc, 256)
    tq = _row_tile(l, 256)
    tqc = _row_tile(lc, 256)

    lat_tabs = _mla_tables(l, True)
    ctx_tabs = _mla_tables(lc, False)
    ret_tabs = _ret_tables(lc + l)
    dec = _ret_decay_tables()

    rows = -(-(b + 1) // SUBLANES) * SUBLANES
    cvec = jnp.zeros((rows, d), F32).at[:b].set(c).at[b].set(c_ctx)
    mod_all = _modulation(cvec, w_mod, b_mod)

    w_arr = _arrange_w_in(w_in)
    woa, wob, woc, wout = (w.astype(BF16) for w in (w_oa, w_ob, w_oc, w_out))
    w1, w3, w2 = (w.astype(BF16) for w in (w_ff1, w_ff3, w_ff2))

    xc = ctx
    for li in range(depth):
        last = li == depth - 1
        mod = mod_all[li, :b].reshape(b, 1, 6, d)
        sh_a, sc_a = mod[:, :, 0], mod[:, :, 1]
        mc = mod_all[li, b].reshape(1, 1, 6, d)
        mcs = [mc[:, :, j] for j in range(6)]

        wq_arr, wqs_arr, wkv_arr = _arrange_mla(w_uq[li], w_ukv[li])
        wg, bg = _arrange_lru(lru_wa[li], lru_ba[li], lru_wx[li], lru_bx[li])

        z = _inproj(x, g_mix[li], sh_a, sc_a, w_arr, li, KV_GROUPS + Q_GROUPS, tm)
        u, ckv, ra, rb, rk, rv, cq, rq, g_lru, g_ret, gm = z
        if last:
            zc = _inproj(xc, g_mix[li], mcs[0], mcs[1], w_arr, li, KV_GROUPS, tmc)
            uc, ckvc, rac, _, rkc, rvc = zc
            cqc = rqc = None
        else:
            zc = _inproj(xc, g_mix[li], mcs[0], mcs[1], w_arr, li, KV_GROUPS + Q_GROUPS, tmc)
            uc, ckvc, rac, _, rkc, rvc, cqc, rqc, g_lru_c, g_ret_c, gm_c = zc
        need_ctx = not last

        rec, rec_c = _lru(u, uc, conv_w[li], conv_b[li], wg, bg, lru_lam[li], need_ctx)

        q, k, v = _mla_prep(cq, ckv, ra, rb, *lat_tabs, g_q[li], wq_arr, wqs_arr, g_kv[li],
                            wkv_arr, True, True, tm)
        qc, kc, vc = _mla_prep(cqc, ckvc, rac, None, *ctx_tabs, g_q[li], wq_arr, None, g_kv[li],
                               wkv_arr, False, need_ctx, tmc)
        yb = _attention(q, [(kc, vc), (k, v)], tq)

        rn, rn_c = _retention(rq, rk, rv, rqc, rkc, rvc, ret_tabs, dec, need_ctx)

        x = _mix_ffn(x, rec, g_lru, yb, rn, g_ret, gm, mod[:, 0, 2:], woa, wob, woc, wout,
                     g_ffn[li], w1, w3, w2, li, g_final if last else None, tm)
        if need_ctx:
            yb_c = _attention(qc, [(kc, vc)], tqc)
            xc = _mix_ffn(xc, rec_c, g_lru_c, yb_c, rn_c, g_ret_c, gm_c, mc[:, 0, 2:], woa, wob, woc, wout,
                          g_ffn[li], w1, w3, w2, li, None, tmc)
    return x
```

```python
import functools
import math

import jax
import jax.numpy as jnp
from jax import lax
from jax.experimental import pallas as pl
from jax.experimental.pallas import tpu as pltpu

F32 = jnp.float32
BF16 = jnp.bfloat16

D_MODEL = 1024
EPS = 1e-6
GRID_W = 64
LRU_W = 512
LRU_BLOCKS = 8
LRU_BLOCK = LRU_W // LRU_BLOCKS
CONV_W = 4
CONV_PAD_L = 2
LRU_C = 8.0
MLA_HEADS = 8
MLA_NOPE = 64
MLA_ROPE = 32
MLA_V = 64
Q_LORA = 384
KV_LORA = 256
ROPE_BASE = 10000.0
RET_HEADS = 4
RET_DK = 64
RET_DV = 128
RET_CHUNK = 256
D_FF = 2816

LANES = 128
SUBLANES = 8
MXU_TILE = 256
VMEM_LIMIT = 56 * 1024 * 1024

KV_GROUPS = (LRU_W, KV_LORA, LANES, LANES, RET_HEADS * RET_DK, RET_HEADS * RET_DV)
Q_GROUPS = (Q_LORA, RET_HEADS * RET_DK, LRU_W, RET_HEADS * RET_DV, 3 * D_MODEL)
N_KV_ARR = sum(KV_GROUPS)


def _params(sem, flags=None):
    return pltpu.CompilerParams(dimension_semantics=sem, vmem_limit_bytes=VMEM_LIMIT, flags=flags)


def _resident(shape):
    nd = len(shape)
    return pl.BlockSpec(shape, lambda *_: (0,) * nd, pipeline_mode=pl.Buffered(1))


def _layer_spec(stacked, li, cols=None):
    shape = stacked.shape[1:] if cols is None else stacked.shape[1:-1] + (cols,)
    nd = len(shape)
    return pl.BlockSpec((None,) + shape, lambda *_: (li,) + (0,) * nd, pipeline_mode=pl.Buffered(1))


def _rms(x, g):
    y = x * lax.rsqrt(jnp.mean(x * x, axis=-1, keepdims=True) + EPS)
    return y * g


def _mod_body(c_ref, w_ref, b_ref, o_ref):
    cv = c_ref[...]
    act = cv * jax.nn.sigmoid(cv)
    o_ref[0] = jnp.dot(act, w_ref[0], precision=lax.Precision.HIGHEST,
                       preferred_element_type=F32) + b_ref[0]


def _modulation(cvec, w_mod, b_mod):
    depth, d, n = w_mod.shape
    rows = cvec.shape[0]
    tn = 1536
    return pl.pallas_call(
        _mod_body,
        grid=(depth, n // tn),
        in_specs=[
            pl.BlockSpec((rows, d), lambda l, j: (0, 0)),
            pl.BlockSpec((1, d, tn), lambda l, j: (l, 0, j)),
            pl.BlockSpec((1, 1, tn), lambda l, j: (l, 0, j)),
        ],
        out_specs=pl.BlockSpec((1, rows, tn), lambda l, j: (l, 0, j)),
        out_shape=jax.ShapeDtypeStruct((depth, rows, n), F32),
        compiler_params=_params(("arbitrary", "arbitrary")),
    )(cvec, w_mod, b_mod.reshape(depth, 1, n))


def _inproj_body(x_ref, g_ref, sh_ref, sc_ref, w_ref, *out_refs, widths):
    x = x_ref[0]
    h = _rms(x, g_ref[...]) * (1.0 + sc_ref[0]) + sh_ref[0]
    hb = h.astype(BF16)
    start, off = 0, 0
    for i, wd in enumerate(widths):
        off += wd
        if off % MXU_TILE == 0 or i == len(widths) - 1:
            base = sum(widths[:start])
            z = jnp.dot(hb, w_ref[:, base:off], preferred_element_type=F32)
            col = 0
            for j in range(start, i + 1):
                out_refs[j][0] = z[:, col:col + widths[j]].astype(out_refs[j].dtype)
                col += widths[j]
            start = i + 1


def _inproj(x, g, shift, scale, w_all, li, widths, tm):
    b, l, d = x.shape
    n = sum(widths)
    mod_map = (lambda bi, i: (bi, 0, 0)) if shift.shape[0] > 1 else (lambda bi, i: (0, 0, 0))
    return pl.pallas_call(
        functools.partial(_inproj_body, widths=widths),
        grid=(b, l // tm),
        in_specs=[
            pl.BlockSpec((1, tm, d), lambda bi, i: (bi, i, 0)),
            _resident((1, d)),
            pl.BlockSpec((1, 1, d), mod_map),
            pl.BlockSpec((1, 1, d), mod_map),
            _layer_spec(w_all, li, n),
        ],
        out_specs=[pl.BlockSpec((1, tm, wd), lambda bi, i: (bi, i, 0)) for wd in widths],
        out_shape=[jax.ShapeDtypeStruct((b, l, wd), BF16) for wd in widths],
        compiler_params=_params(("parallel", "parallel")),
    )(x, g.reshape(1, d), shift, scale, w_all)


LRU_TC = 256
LRU_HALO = SUBLANES


def _lru_body(u_ref, uc_ref, cw_ref, cb_ref, wg_ref, bg_ref, lam_ref, *rest, need_ctx):
    if need_ctx:
        out_ref, outc_ref, upad, upadc, ucv, ucvc, hbuf = rest
    else:
        out_ref, upad, upadc, ucv, ucvc, hbuf = rest
        outc_ref = None
    l = u_ref.shape[1]
    lc = uc_ref.shape[1]
    w = u_ref.shape[2]
    tc = LRU_TC
    nt = tc // SUBLANES

    zeros_halo = jnp.zeros((LRU_HALO, w), F32)
    for pad_ref, cv_ref, src_ref, n in ((upad, ucv, u_ref, l), (upadc, ucvc, uc_ref, lc)):
        pad_ref[0:LRU_HALO, :] = zeros_halo
        pad_ref[LRU_HALO + n:2 * LRU_HALO + n, :] = zeros_halo

        def copy(i, carry, pad_ref=pad_ref, src_ref=src_ref):
            t0 = pl.multiple_of(i * tc, tc)
            pad_ref[pl.ds(LRU_HALO + t0, tc), :] = src_ref[0, pl.ds(t0, tc), :].astype(F32)
            return carry

        lax.fori_loop(0, n // tc, copy, 0)

        def conv(i, carry, pad_ref=pad_ref, cv_ref=cv_ref):
            t0 = pl.multiple_of(i * tc, tc)
            win = pad_ref[pl.ds(t0, tc + 2 * LRU_HALO), :]
            u = cb_ref[...]
            for k in range(CONV_W):
                sh = (CONV_PAD_L - k) % (tc + 2 * LRU_HALO)
                tap = win if sh == 0 else pltpu.roll(win, sh, 0)
                u = u + cw_ref[k:k + 1, :] * tap[LRU_HALO:LRU_HALO + tc]
            cv_ref[pl.ds(t0, tc), :] = u
            return carry

        lax.fori_loop(0, n // tc, conv, 0)

    row = lax.broadcasted_iota(jnp.int32, (nt, SUBLANES, w), 1)

    def chunk_scan(cv_ref, t0, d, h):
        u = cv_ref[pl.ds(t0, tc), :]
        g = jnp.dot(u.astype(BF16), wg_ref[d], preferred_element_type=F32) + bg_ref[d]
        r = jax.nn.sigmoid(g[:, :w])
        gi = jax.nn.sigmoid(g[:, w:])
        z = -lam_ref[d]
        sp = jnp.maximum(z, 0.0) + jnp.log1p(jnp.exp(-jnp.abs(z)))
        log_a = r * (-LRU_C * sp)
        a = jnp.exp2(r * (-LRU_C * math.log2(math.e) * sp))
        th = jnp.tanh(log_a)
        om = -2.0 * th / (1.0 - th)
        root = jnp.where(om > 0.0, om * lax.rsqrt(om), 0.0)
        bb = root * (gi * u)
        a = a.reshape(nt, SUBLANES, w)
        bb = bb.reshape(nt, SUBLANES, w)
        s = 1
        while s < SUBLANES:
            if d == 0:
                keep = row >= s
                shift = s
            else:
                keep = row < SUBLANES - s
                shift = SUBLANES - s
            a_s = jnp.where(keep, pltpu.roll(a, shift, 1), 1.0)
            b_s = jnp.where(keep, pltpu.roll(bb, shift, 1), 0.0)
            bb = a * b_s + bb
            a = a * a_s
            s *= 2
        order = range(nt) if d == 0 else range(nt - 1, -1, -1)
        for k in order:
            hk = a[k] * h + bb[k]
            hbuf[k * SUBLANES:(k + 1) * SUBLANES, :] = hk
            h = hk[SUBLANES - 1:SUBLANES] if d == 0 else hk[0:1]
        return h

    def run(cv_ref, n, d, h, write):
        nchunks = n // tc

        def body(i, h):
            ci = i if d == 0 else nchunks - 1 - i
            t0 = pl.multiple_of(ci * tc, tc)
            h = chunk_scan(cv_ref, t0, d, h)
            write(t0)
            return h

        return lax.fori_loop(0, nchunks, body, h, unroll=2)

    def write_accc(t0):
        upadc[pl.ds(t0, tc), :] = hbuf[...]

    def write_acc(t0):
        upad[pl.ds(t0, tc), :] = hbuf[...]

    def write_outc(t0):
        if outc_ref is not None:
            outc_ref[0, pl.ds(t0, tc), :] = (upadc[pl.ds(t0, tc), :] + hbuf[...]).astype(outc_ref.dtype)

    def write_out(t0):
        out_ref[0, pl.ds(t0, tc), :] = (upad[pl.ds(t0, tc), :] + hbuf[...]).astype(out_ref.dtype)

    h0 = jnp.zeros((1, w), F32)
    h = run(ucvc, lc, 0, h0, write_accc)
    run(ucv, l, 0, h, write_acc)
    h = run(ucvc, lc, 1, h0, write_outc)
    run(ucv, l, 1, h, write_out)


def _lru(u, uc, conv_w, conv_b, wg, bg, lam, need_ctx):
    b, l, w = u.shape
    lc = uc.shape[1]
    out_shape = [jax.ShapeDtypeStruct((b, l, w), BF16)]
    out_specs = [pl.BlockSpec((1, l, w), lambda bi: (bi, 0, 0))]
    if need_ctx:
        out_shape.append(jax.ShapeDtypeStruct((b, lc, w), BF16))
        out_specs.append(pl.BlockSpec((1, lc, w), lambda bi: (bi, 0, 0)))
    res = pl.pallas_call(
        functools.partial(_lru_body, need_ctx=need_ctx),
        grid=(b,),
        in_specs=[
            pl.BlockSpec((1, l, w), lambda bi: (bi, 0, 0)),
            pl.BlockSpec((1, lc, w), lambda bi: (bi, 0, 0)),
            _resident((CONV_W, w)),
            _resident((1, w)),
            _resident((2, w, 2 * w)),
            _resident((2, 1, 2 * w)),
            _resident((2, 1, w)),
        ],
        out_specs=out_specs,
        out_shape=out_shape,
        scratch_shapes=[
            pltpu.VMEM((l + 2 * LRU_HALO, w), F32),
            pltpu.VMEM((lc + 2 * LRU_HALO, w), F32),
            pltpu.VMEM((l, w), F32),
            pltpu.VMEM((lc, w), F32),
            pltpu.VMEM((LRU_TC, w), F32),
        ],
        compiler_params=_params(("parallel",)),
    )(u, uc, conv_w, conv_b.reshape(1, w), wg, bg, lam.reshape(2, 1, w))
    return (res[0], res[1]) if need_ctx else (res[0], None)


def _mla_prep_body(*refs, rotate, need_q):
    it = iter(refs)
    cq_ref = next(it) if need_q else None
    ckv_ref, ra_ref = next(it), next(it)
    rb_ref = next(it) if rotate else None
    if need_q:
        cosq_ref = next(it)
        sinq_ref = next(it) if rotate else None
    cosk_ref = next(it)
    sink_ref = next(it) if rotate else None
    if need_q:
        gq_ref, wq_ref = next(it), next(it)
        wqs_ref = next(it) if rotate else None
    gkv_ref, wkv_ref = next(it), next(it)
    q_out = next(it) if need_q else None
    k_out, v_out = next(it), next(it)
    nh = MLA_HEADS
    tm = ckv_ref.shape[1]

    if need_q:
        cqn = _rms(cq_ref[0].astype(F32), gq_ref[...]).astype(BF16)
        qm = jnp.dot(cqn, wq_ref[...], preferred_element_type=F32)
        cosq = cosq_ref[...]
        if rotate:
            qs = jnp.dot(cqn, wqs_ref[...], preferred_element_type=F32)
            sinq = sinq_ref[...]
        for h in range(nh):
            sl = slice(h * LANES, (h + 1) * LANES)
            qh = qm[:, sl] * cosq
            if rotate:
                qh = qh + qs[:, sl] * sinq
            q_out[0, h] = qh.astype(q_out.dtype)

    kvn = _rms(ckv_ref[0].astype(F32), gkv_ref[...]).astype(BF16)
    kv = jnp.dot(kvn, wkv_ref[...], preferred_element_type=F32)
    rope = ra_ref[0].astype(F32) * cosk_ref[...]
    if rotate:
        rope = rope + rb_ref[0].astype(F32) * sink_ref[...]
    lane = lax.broadcasted_iota(jnp.int32, (tm, LANES), 1)
    for h in range(nh):
        k_out[0, h] = (kv[:, h * LANES:(h + 1) * LANES] + rope).astype(k_out.dtype)
        vh = kv[:, (nh + h) * LANES:(nh + h + 1) * LANES]
        v_out[0, h] = jnp.where(lane == _sum_lane(h), 1.0, vh).astype(v_out.dtype)


def _sum_lane(h):
    return MLA_V if h % 2 == 0 else 0


def _mla_prep(cq, ckv, ra, rb, cosq, sinq, cosk, sink, gq, wq, wqs, gkv, wkv, rotate, need_q, tm):
    b, l, _ = ckv.shape
    nh = MLA_HEADS
    row = lambda wd: pl.BlockSpec((1, tm, wd), lambda bi, i: (bi, i, 0))
    tab = pl.BlockSpec((tm, LANES), lambda bi, i: (i, 0))
    args, specs = [], []

    def add(a, s):
        args.append(a)
        specs.append(s)

    if need_q:
        add(cq, row(Q_LORA))
    add(ckv, row(KV_LORA))
    add(ra, row(LANES))
    if rotate:
        add(rb, row(LANES))
    if need_q:
        add(cosq, tab)
        if rotate:
            add(sinq, tab)
    add(cosk, tab)
    if rotate:
        add(sink, tab)
    if need_q:
        add(gq.reshape(1, Q_LORA), _resident((1, Q_LORA)))
        add(wq, _resident(wq.shape))
        if rotate:
            add(wqs, _resident(wqs.shape))
    add(gkv.reshape(1, KV_LORA), _resident((1, KV_LORA)))
    add(wkv, _resident(wkv.shape))

    head = pl.BlockSpec((1, nh, tm, LANES), lambda bi, i: (bi, 0, i, 0))
    hshape = jax.ShapeDtypeStruct((b, nh, l, LANES), ATTN_DT)
    n_out = 3 if need_q else 2
    res = pl.pallas_call(
        functools.partial(_mla_prep_body, rotate=rotate, need_q=need_q),
        grid=(b, l // tm),
        in_specs=specs,
        out_specs=[head] * n_out,
        out_shape=[hshape] * n_out,
        compiler_params=_params(("parallel", "parallel")),
    )(*args)
    return res if need_q else (None, res[0], res[1])


ATTN_PAIRS_PER_STEP = 4
ATTN_DT = jnp.float8_e4m3fn
ATTN_P_SHIFT = 8.0


def _attn_body(q_ref, *refs, n_kv, tq):
    kv_refs = refs[:2 * n_kv]
    o_ref = refs[2 * n_kv]
    s_bufs = refs[2 * n_kv + 1:2 * n_kv + 3]
    p_bufs = refs[2 * n_kv + 3:2 * n_kv + 5]
    m_bufs = refs[2 * n_kv + 5:2 * n_kv + 7]
    npair = ATTN_PAIRS_PER_STEP
    n_items = (q_ref.shape[2] // tq) * npair
    lks = [kv_refs[2 * j].shape[2] for j in range(n_kv)]
    offs = [sum(lks[:j]) for j in range(n_kv)]
    lane = lax.broadcasted_iota(jnp.int32, (tq, LANES), 1)
    nt_dims = (((1,), (1,)), ((), ()))

    s_bufs[1][...] = jnp.zeros(s_bufs[1].shape, s_bufs[1].dtype)
    p_bufs[0][...] = jnp.zeros(p_bufs[0].shape, p_bufs[0].dtype)
    m_bufs[1][...] = jnp.zeros(m_bufs[1].shape, m_bufs[1].dtype)

    def item(t):
        tile = lax.shift_right_logical(t, npair.bit_length() - 1)
        pair = jnp.bitwise_and(t, npair - 1)
        return pl.multiple_of(tile * tq, tq), pair

    def stages(t, slot):
        s_a, s_b = s_bufs[slot], s_bufs[1 - slot]
        p_b, p_c = p_bufs[1 - slot], p_bufs[slot]
        m_a, m_b = m_bufs[slot], m_bufs[1 - slot]

        row_a, pair_a = item(jnp.minimum(t, n_items - 1))
        for e in range(2):
            h = pair_a * 2 + e
            q = q_ref[0, h, pl.ds(row_a, tq), :]
            bm = None
            for j in range(n_kv):
                sj = lax.dot_general(q, kv_refs[2 * j][0, h], nt_dims,
                                     preferred_element_type=F32).astype(s_a.dtype)
                s_a[e, :, offs[j]:offs[j] + lks[j]] = sj
                for c0 in range(0, lks[j], LANES):
                    blk = sj[:, c0:c0 + LANES]
                    bm = blk if bm is None else jnp.maximum(bm, blk)
            m_a[e] = bm

        for e in range(2):
            s = s_b[e]
            m = m_b[e].astype(F32).max(axis=-1, keepdims=True)
            shift = (m - ATTN_P_SHIFT).astype(s.dtype)
            p_b[e] = jnp.exp2(s - shift).astype(p_b.dtype)

        row_c, pair_c = item(jnp.maximum(t - 2, 0))
        halves = []
        for e in range(2):
            h = pair_c * 2 + e
            o = None
            for j in range(n_kv):
                oj = jnp.dot(p_c[e, :, offs[j]:offs[j] + lks[j]], kv_refs[2 * j + 1][0, h],
                             preferred_element_type=F32)
                o = oj if o is None else o + oj
            one = _sum_lane(e)
            halves.append(o * (1.0 / o[:, one:one + 1]))
        blk = jnp.where(lane < MLA_V, halves[0], halves[1])
        o_ref[0, pair_c, pl.ds(row_c, tq), :] = blk.astype(o_ref.dtype)

    def body(i, carry):
        stages(2 * i, 0)
        stages(2 * i + 1, 1)
        return carry

    lax.fori_loop(0, (n_items + 2) // 2, body, 0)


def _attention(q, kvs, tq):
    b, nh, lq, _ = q.shape
    hs = 2 * ATTN_PAIRS_PER_STEP
    once = lambda n: pl.BlockSpec((1, hs, n, LANES), lambda bi, j: (bi, j, 0, 0))
    specs = [once(lq)]
    args = [q]
    lkt = 0
    for k, v in kvs:
        lk = k.shape[2]
        lkt += lk
        specs += [once(lk), once(lk)]
        args += [k, v]
    return pl.pallas_call(
        functools.partial(_attn_body, n_kv=len(kvs), tq=tq),
        grid=(b, nh // hs),
        in_specs=specs,
        out_specs=pl.BlockSpec((1, ATTN_PAIRS_PER_STEP, lq, LANES), lambda bi, j: (bi, j, 0, 0)),
        out_shape=jax.ShapeDtypeStruct((b, nh // 2, lq, LANES), BF16),
        scratch_shapes=[
            pltpu.VMEM((2, tq, lkt), BF16),
            pltpu.VMEM((2, tq, lkt), BF16),
            pltpu.VMEM((2, tq, lkt), ATTN_DT),
            pltpu.VMEM((2, tq, lkt), ATTN_DT),
            pltpu.VMEM((2, tq, LANES), BF16),
            pltpu.VMEM((2, tq, LANES), BF16),
        ],
        compiler_params=_params(("parallel", "parallel")),
    )(*args)


def _ret_body(*refs, need_ctx):
    it = iter(refs)
    q_ref, k_ref, v_ref = next(it), next(it), next(it)
    qc_ref = next(it) if need_ctx else None
    kc_ref, vc_ref = next(it), next(it)
    cos_ref, sin_ref = next(it), next(it)
    msum_ref, qdec_ref, kdec_ref, cd_ref = next(it), next(it), next(it), next(it)
    o_ref = next(it)
    oc_ref = next(it) if need_ctx else None
    qa, ka, va, u_s, r_s = (next(it) for _ in range(5))

    c = RET_CHUNK
    l = q_ref.shape[1]
    lc = kc_ref.shape[1]
    ncc = lc // c
    nt = (lc + l) // c
    kscale = RET_DK ** -0.5
    lane = lax.broadcasted_iota(jnp.int32, (c, LANES), 1)
    first_half = (lane % RET_DK) < (RET_DK // 2)

    def rotated(src_ref, t0, g0):
        x = src_ref[0, pl.ds(t0, c), :].astype(F32)
        sw = jnp.where(first_half, pltpu.roll(x, LANES - RET_DK // 2, 1),
                       pltpu.roll(x, RET_DK // 2, 1))
        return x * cos_ref[pl.ds(g0, c), :] + sw * sin_ref[pl.ds(g0, c), :]

    def stage(qsrc, ksrc, vsrc, n, base):
        def body(i, carry):
            t0 = pl.multiple_of(i * c, c)
            g0 = pl.multiple_of(base + i * c, c)
            if qsrc is not None:
                qa[pl.ds(g0, c), :] = rotated(qsrc, t0, g0)
            ka[pl.ds(g0, c), :] = rotated(ksrc, t0, g0) * kscale
            va[pl.ds(g0, c), :] = vsrc[0, pl.ds(t0, c), :]
            return carry

        lax.fori_loop(0, n // c, body, 0, unroll=2)

    stage(qc_ref, kc_ref, vc_ref, lc, 0)
    stage(q_ref, k_ref, v_ref, l, lc)

    tn_dims = (((0,), (0,)), ((), ()))
    nt_dims = (((1,), (1,)), ((), ()))

    def incr(g, carry):
        g0 = pl.multiple_of(g * c, c)
        kf = ka[pl.ds(g0, c), :]
        for e in range(2):
            v = va[pl.ds(g0, c), e * RET_DV:(e + 1) * RET_DV]
            kd = jnp.concatenate([kf * kdec_ref[0, e, 0], kf * kdec_ref[0, e, 1]], axis=1)
            u_s[g, e] = lax.dot_general(kd.astype(BF16), v, tn_dims, preferred_element_type=F32)
        return carry

    lax.fori_loop(0, nt, incr, 0, unroll=4)

    zero = tuple(jnp.zeros((LANES, RET_DV), F32) for _ in range(4))

    def recur(t, st):
        gs = (t, jnp.where(t < ncc, ncc - 1 - t, nt - 1 + ncc - t))
        new = []
        for d in range(2):
            rows = slice(d * LANES, (d + 1) * LANES)
            for e in range(2):
                r = st[2 * d + e]
                r_s[gs[d], e, rows, :] = r.astype(r_s.dtype)
                new.append(cd_ref[0, e, d] * r + u_s[gs[d], e, rows, :])
        return tuple(new)

    lax.fori_loop(0, nt, recur, zero, unroll=2)

    def emit(dst_ref, gbase):
        def body(i, carry):
            t0 = pl.multiple_of(i * c, c)
            g = gbase + i
            g0 = pl.multiple_of(g * c, c)
            qf = qa[pl.ds(g0, c), :]
            kb = ka[pl.ds(g0, c), :].astype(BF16)
            parts = []
            for e in range(2):
                v = va[pl.ds(g0, c), e * RET_DV:(e + 1) * RET_DV]
                qm = jnp.where((lane // RET_DK) == e, qf, 0.0).astype(BF16)
                s = lax.dot_general(qm, kb, nt_dims, preferred_element_type=F32)
                o = jnp.dot((s * msum_ref[0, e]).astype(BF16), v, preferred_element_type=F32)
                qd = jnp.concatenate([qf * qdec_ref[0, e, 0], qf * qdec_ref[0, e, 1]], axis=1)
                o = o + jnp.dot(qd.astype(BF16), r_s[g, e], preferred_element_type=F32)
                mu = jnp.mean(o, axis=-1, keepdims=True)
                oc = o - mu
                var = jnp.mean(oc * oc, axis=-1, keepdims=True)
                parts.append(oc * lax.rsqrt(var + EPS))
            dst_ref[0, pl.ds(t0, c), :] = jnp.concatenate(parts, axis=1).astype(dst_ref.dtype)
            return carry

        return body

    if need_ctx:
        lax.fori_loop(0, ncc, emit(oc_ref, 0), 0, unroll=2)
    lax.fori_loop(0, nt - ncc, emit(o_ref, ncc), 0, unroll=8)


def _retention(q, k, v, qc, kc, vc, tabs, dec, need_ctx):
    b, l, _ = q.shape
    lc = kc.shape[1]
    lt = lc + l
    c = RET_CHUNK
    nt = lt // c
    npair = RET_HEADS // 2
    seq = lambda n, wd: pl.BlockSpec((1, n, wd), lambda bi, p: (bi, 0, p))
    args = [q, k, v]
    specs = [seq(l, LANES), seq(l, LANES), seq(l, 2 * RET_DV)]
    if need_ctx:
        args.append(qc)
        specs.append(seq(lc, LANES))
    args += [kc, vc]
    specs += [seq(lc, LANES), seq(lc, 2 * RET_DV)]
    for t in tabs:
        args.append(t)
        specs.append(_resident((lt, LANES)))
    msum, qdec, kdec, cd = dec
    args += [msum, qdec, kdec, cd]
    specs += [
        pl.BlockSpec((1, 2, c, c), lambda bi, p: (p, 0, 0, 0)),
        pl.BlockSpec((1, 2, 2, c, LANES), lambda bi, p: (p, 0, 0, 0, 0)),
        pl.BlockSpec((1, 2, 2, c, LANES), lambda bi, p: (p, 0, 0, 0, 0)),
        pl.BlockSpec((1, 2, 2, 1, RET_DV), lambda bi, p: (p, 0, 0, 0, 0)),
    ]
    out_shape = [jax.ShapeDtypeStruct((b, l, RET_HEADS * RET_DV), BF16)]
    out_specs = [seq(l, 2 * RET_DV)]
    if need_ctx:
        out_shape.append(jax.ShapeDtypeStruct((b, lc, RET_HEADS * RET_DV), BF16))
        out_specs.append(seq(lc, 2 * RET_DV))
    res = pl.pallas_call(
        functools.partial(_ret_body, need_ctx=need_ctx),
        grid=(b, npair),
        in_specs=specs,
        out_specs=out_specs,
        out_shape=out_shape,
        scratch_shapes=[
            pltpu.VMEM((lt, LANES), F32),
            pltpu.VMEM((lt, LANES), F32),
            pltpu.VMEM((lt, 2 * RET_DV), BF16),
            pltpu.VMEM((nt, 2, 2 * LANES, RET_DV), F32),
            pltpu.VMEM((nt, 2, 2 * LANES, RET_DV), BF16),
        ],
        compiler_params=_params(("parallel", "parallel")),
    )(*args)
    return (res[0], res[1]) if need_ctx else (res[0], None)


FFN_CHUNKS = ((0, 1536), (1536, D_FF))
assert all(lo % MXU_TILE == 0 and hi % MXU_TILE == 0 for lo, hi in FFN_CHUNKS)


def _mix_ffn_body(x_ref, rec_ref, gl_ref, yb_ref, rn_ref, gr_ref, gm_ref, mod_ref,
                  woa_ref, wob_ref, woc_ref, wout_ref, g_ref, w1_ref, w3_ref, w2_ref, *rest, final):
    if final:
        gf_ref, o_ref = rest
    else:
        (o_ref,) = rest
    d = x_ref.shape[2]
    ga_a, sh_f, sc_f, ga_f = (mod_ref[0, j:j + 1, :] for j in range(4))

    ya = (jax.nn.gelu(gl_ref[0].astype(F32)) * rec_ref[0].astype(F32)).astype(BF16)
    yc = (jax.nn.silu(gr_ref[0].astype(F32)) * rn_ref[0].astype(F32)).astype(BF16)
    yb = jnp.concatenate([yb_ref[0, p] for p in range(yb_ref.shape[1])], axis=1)
    gm = gm_ref[0]
    m = jax.nn.sigmoid(gm[:, :d].astype(F32)) * jnp.dot(ya, woa_ref[...], preferred_element_type=F32)
    m = m + jax.nn.sigmoid(gm[:, d:2 * d].astype(F32)) * jnp.dot(
        yb, wob_ref[...], preferred_element_type=F32)
    m = m + jax.nn.sigmoid(gm[:, 2 * d:].astype(F32)) * jnp.dot(
        yc, woc_ref[...], preferred_element_type=F32)
    x = x_ref[0] + ga_a * jnp.dot(m.astype(BF16), wout_ref[...], preferred_element_type=F32)

    hb = (_rms(x, g_ref[...]) * (1.0 + sc_f) + sh_f).astype(BF16)
    acc = None
    for c0, c1 in FFN_CHUNKS:
        a = jnp.dot(hb, w1_ref[:, c0:c1], preferred_element_type=F32)
        bgate = jnp.dot(hb, w3_ref[:, c0:c1], preferred_element_type=F32)
        t = (jax.nn.silu(a) * bgate).astype(BF16)
        part = jnp.dot(t, w2_ref[c0:c1, :], preferred_element_type=F32)
        acc = part if acc is None else acc + part
    y = x + ga_f * acc
    if final:
        y = _rms(y, gf_ref[...])
    o_ref[0] = y


def _mix_ffn(x, rec, gl, yb, rn, gr, gm, mod4, woa, wob, woc, wout, g, w1, w3, w2, li, g_final, tm):
    b, l, d = x.shape
    final = g_final is not None
    row = lambda wd: pl.BlockSpec((1, tm, wd), lambda bi, i: (bi, i, 0))
    mod_map = (lambda bi, i: (bi, 0, 0)) if mod4.shape[0] > 1 else (lambda bi, i: (0, 0, 0))
    args = [x, rec, gl, yb, rn, gr, gm, mod4, woa, wob, woc, wout, g.reshape(1, d), w1, w3, w2]
    specs = [
        row(d), row(LRU_W), row(LRU_W),
        pl.BlockSpec((1, MLA_HEADS // 2, tm, LANES), lambda bi, i: (bi, 0, i, 0)),
        row(RET_HEADS * RET_DV), row(RET_HEADS * RET_DV), row(3 * d),
        pl.BlockSpec((1, 4, d), mod_map),
        _layer_spec(woa, li), _layer_spec(wob, li), _layer_spec(woc, li), _layer_spec(wout, li),
        _resident((1, d)),
        _layer_spec(w1, li), _layer_spec(w3, li), _layer_spec(w2, li),
    ]
    if final:
        args.append(g_final.reshape(1, d))
        specs.append(_resident((1, d)))
    return pl.pallas_call(
        functools.partial(_mix_ffn_body, final=final),
        grid=(b, l // tm),
        in_specs=specs,
        out_specs=row(d),
        out_shape=jax.ShapeDtypeStruct((b, l, d), F32),
        compiler_params=_params(("parallel", "parallel")),
    )(*args)


def _swap_half(w):
    half = w.shape[-1] // 2
    return jnp.concatenate([-w[..., half:], w[..., :half]], axis=-1)


def _arrange_w_in(w_in):
    w_in = w_in.astype(BF16)
    lead = w_in.shape[:-1]
    o = 0
    parts = {}
    for name, n in (("lru_x", LRU_W), ("ckv", KV_LORA), ("kr", MLA_ROPE),
                    ("ret_k", RET_HEADS * RET_DK), ("ret_v", RET_HEADS * RET_DV),
                    ("cq", Q_LORA), ("ret_q", RET_HEADS * RET_DK), ("g_lru", LRU_W),
                    ("g_ret", RET_HEADS * RET_DV), ("gm", 3 * D_MODEL)):
        parts[name] = w_in[..., o:o + n]
        o += n
    z = lambda n: jnp.zeros(lead + (n,), w_in.dtype)
    rope_a = jnp.concatenate([z(MLA_NOPE), parts["kr"], z(LANES - MLA_NOPE - MLA_ROPE)], axis=-1)
    rope_b = jnp.concatenate([z(MLA_NOPE), _swap_half(parts["kr"]),
                              z(LANES - MLA_NOPE - MLA_ROPE)], axis=-1)
    cols = [parts["lru_x"], parts["ckv"], rope_a, rope_b, parts["ret_k"], parts["ret_v"],
            parts["cq"], parts["ret_q"], parts["g_lru"], parts["g_ret"], parts["gm"]]
    return jnp.concatenate(cols, axis=-1)


def _arrange_mla(w_uq, w_ukv):
    nh = MLA_HEADS
    wq = w_uq.reshape(Q_LORA, nh, MLA_NOPE + MLA_ROPE)
    padq = jnp.zeros((Q_LORA, nh, LANES - MLA_NOPE - MLA_ROPE), w_uq.dtype)
    wq_arr = jnp.concatenate([wq, padq], axis=-1).reshape(Q_LORA, nh * LANES)
    wqs = jnp.concatenate([jnp.zeros((Q_LORA, nh, MLA_NOPE), w_uq.dtype),
                           _swap_half(wq[..., MLA_NOPE:]), padq], axis=-1)
    wqs_arr = wqs.reshape(Q_LORA, nh * LANES)
    wkv = w_ukv.reshape(KV_LORA, nh, MLA_NOPE + MLA_V)
    zk = jnp.zeros((KV_LORA, nh, LANES - MLA_NOPE), w_ukv.dtype)
    wk_arr = jnp.concatenate([wkv[..., :MLA_NOPE], zk], axis=-1).reshape(KV_LORA, nh * LANES)
    wv = wkv[..., MLA_NOPE:]
    zv = jnp.zeros_like(wv)
    even = (jnp.arange(nh) % 2 == 0)[None, :, None]
    wv_arr = jnp.concatenate([jnp.where(even, wv, zv), jnp.where(even, zv, wv)], axis=-1)
    wv_arr = wv_arr.reshape(KV_LORA, nh * LANES)
    return (wq_arr.astype(BF16), wqs_arr.astype(BF16),
            jnp.concatenate([wk_arr, wv_arr], axis=1).astype(BF16))


def _arrange_lru(wa, ba, wx, bx):
    def dense(wblk):
        eye = jnp.eye(LRU_BLOCKS, dtype=wblk.dtype)
        full = jnp.einsum("nde,nm->ndme", wblk, eye)
        return full.reshape(LRU_W, LRU_W)

    wg = jnp.stack([jnp.concatenate([dense(wa[d]), dense(wx[d])], axis=1) for d in range(2)])
    bg = jnp.stack([jnp.concatenate([ba[d], bx[d]])[None, :] for d in range(2)])
    return wg.astype(BF16), bg


def _mla_tables(n_tokens, rotate):
    scale = (MLA_NOPE + MLA_ROPE) ** -0.5 * math.log2(math.e)
    pad = LANES - MLA_NOPE - MLA_ROPE
    ones = jnp.ones((n_tokens, MLA_NOPE), F32)
    zeros = jnp.zeros((n_tokens, MLA_NOPE), F32)
    zpad = jnp.zeros((n_tokens, pad), F32)
    if rotate:
        rows = n_tokens // GRID_W
        rowi = jnp.repeat(jnp.arange(rows, dtype=F32), GRID_W)
        coli = jnp.tile(jnp.arange(GRID_W, dtype=F32), rows)
        n_freq = MLA_ROPE // 4
        inv = jnp.power(ROPE_BASE, -jnp.arange(n_freq, dtype=F32) / n_freq)
        ang = jnp.concatenate([rowi[:, None] * inv, coli[:, None] * inv], axis=-1)
        cos, sin = jnp.cos(ang), jnp.sin(ang)
    else:
        cos = jnp.ones((n_tokens, MLA_ROPE // 2), F32)
        sin = jnp.zeros((n_tokens, MLA_ROPE // 2), F32)
    cosq = jnp.concatenate([ones, cos, cos, zpad], axis=1) * scale
    sinq = jnp.concatenate([zeros, sin, sin, zpad], axis=1) * scale
    cosk = jnp.concatenate([zeros, cos, cos, zpad], axis=1)
    sink = jnp.concatenate([zeros, sin, sin, zpad], axis=1)
    return cosq, sinq, cosk, sink


def _ret_tables(n):
    theta = 1.0 / jnp.power(10000.0, jnp.linspace(0.0, 1.0, RET_DK // 2, dtype=F32))
    pos = jnp.arange(n, dtype=F32)
    ang = pos[:, None] * theta
    cos, sin = jnp.cos(ang), jnp.sin(ang)
    reps = LANES // RET_DK
    cos_t = jnp.tile(jnp.concatenate([cos, cos], axis=1), (1, reps))
    sin_t = jnp.tile(jnp.concatenate([-sin, sin], axis=1), (1, reps))
    return cos_t, sin_t


def _ret_decay_tables():
    c = RET_CHUNK
    h = jnp.arange(RET_HEADS, dtype=F32)
    lgs = jnp.stack([jnp.log1p(-jnp.exp2(-5.0 - h)), jnp.log1p(-jnp.exp2(-5.5 - h))], axis=1)
    pos = jnp.arange(c, dtype=F32)
    diff = pos[:, None] - pos[None, :]
    lg = lgs[:, :, None, None]
    inner = jnp.where(diff >= 0, jnp.exp(lg * jnp.maximum(diff, 0.0)), 0.0)
    msum = inner[:, 0] + jnp.swapaxes(inner[:, 1], -1, -2)
    lg1 = lgs[:, :, None]
    qd_f = jnp.exp(lg1 * (pos + 1.0))
    kd_f = jnp.exp(lg1 * (c - 1.0 - pos))
    qd = jnp.stack([qd_f[:, 0], qd_f[:, 1, ::-1]], axis=1)
    kd = jnp.stack([kd_f[:, 0], kd_f[:, 1, ::-1]], axis=1)
    lane_head = (jnp.arange(LANES) // RET_DK)
    e_of_h = jnp.arange(RET_HEADS) % 2
    mask = (lane_head[None, :] == e_of_h[:, None]).astype(F32)
    qdec = qd[..., None] * mask[:, None, None, :]
    kdec = kd[..., None] * mask[:, None, None, :]
    cd = jnp.broadcast_to(jnp.exp(lgs * c)[:, :, None, None], (RET_HEADS, 2, 1, RET_DV))
    pair = lambda t: t.reshape(RET_HEADS // 2, 2, *t.shape[1:])
    return pair(msum), pair(qdec), pair(kdec), pair(cd)


def _row_tile(n, pref):
    t = min(pref, n)
    while n % t:
        t //= 2
    return t


def kernel(x, c, ctx, c_ctx, w_mod, b_mod, g_mix, g_ffn, w_in, conv_w, conv_b, lru_wa, lru_ba,
           lru_wx, lru_bx, lru_lam, g_q, w_uq, g_kv, w_ukv, w_oa, w_ob, w_oc, w_out,
           w_ff1, w_ff3, w_ff2, g_final):
    b, l, d = x.shape
    lc = ctx.shape[1]
    depth = w_mod.shape[0]
    assert d == D_MODEL and l % LRU_TC == 0 and lc % LRU_TC == 0 and l % GRID_W == 0

    tm = _row_tile(l, 512)
    tmc = _row_tile(lc, 256)
    tq = _row_tile(l, 256)
    tqc = _row_tile(lc, 256)

    lat_tabs = _mla_tables(l, True)
    ctx_tabs = _mla_tables(lc, False)
    ret_tabs = _ret_tables(lc + l)
    dec = _ret_decay_tables()

    rows = -(-(b + 1) // SUBLANES) * SUBLANES
    cvec = jnp.zeros((rows, d), F32).at[:b].set(c).at[b].set(c_ctx)
    mod_all = _modulation(cvec, w_mod, b_mod)

    w_arr = _arrange_w_in(w_in)
    woa, wob, woc, wout = (w.astype(BF16) for w in (w_oa, w_ob, w_oc, w_out))
    w1, w3, w2 = (w.astype(BF16) for w in (w_ff1, w_ff3, w_ff2))

    xc = ctx
    for li in range(depth):
        last = li == depth - 1
        mod = mod_all[li, :b].reshape(b, 1, 6, d)
        sh_a, sc_a = mod[:, :, 0], mod[:, :, 1]
        mc = mod_all[li, b].reshape(1, 1, 6, d)
        mcs = [mc[:, :, j] for j in range(6)]

        wq_arr, wqs_arr, wkv_arr = _arrange_mla(w_uq[li], w_ukv[li])
        wg, bg = _arrange_lru(lru_wa[li], lru_ba[li], lru_wx[li], lru_bx[li])

        z = _inproj(x, g_mix[li], sh_a, sc_a, w_arr, li, KV_GROUPS + Q_GROUPS, tm)
        u, ckv, ra, rb, rk, rv, cq, rq, g_lru, g_ret, gm = z
        if last:
            zc = _inproj(xc, g_mix[li], mcs[0], mcs[1], w_arr, li, KV_GROUPS, tmc)
            uc, ckvc, rac, _, rkc, rvc = zc
            cqc = rqc = None
        else:
            zc = _inproj(xc, g_mix[li], mcs[0], mcs[1], w_arr, li, KV_GROUPS + Q_GROUPS, tmc)
            uc, ckvc, rac, _, rkc, rvc, cqc, rqc, g_lru_c, g_ret_c, gm_c = zc
        need_ctx = not last

        rec, rec_c = _lru(u, uc, conv_w[li], conv_b[li], wg, bg, lru_lam[li], need_ctx)

        q, k, v = _mla_prep(cq, ckv, ra, rb, *lat_tabs, g_q[li], wq_arr, wqs_arr, g_kv[li],
                            wkv_arr, True, True, tm)
        qc, kc, vc = _mla_prep(cqc, ckvc, rac, None, *ctx_tabs, g_q[li], wq_arr, None, g_kv[li],
                               wkv_arr, False, need_ctx, tmc)
        yb = _attention(q, [(kc, vc), (k, v)], tq)

        rn, rn_c = _retention(rq, rk, rv, rqc, rkc, rvc, ret_tabs, dec, need_ctx)

        x = _mix_ffn(x, rec, g_lru, yb, rn, g_ret, gm, mod[:, 0, 2:], woa, wob, woc, wout,
                     g_ffn[li], w1, w3, w2, li, g_final if last else None, tm)
        if need_ctx:
            yb_c = _attention(qc, [(kc, vc)], tqc)
            xc = _mix_ffn(xc, rec_c, g_lru_c, yb_c, rn_c, g_ret_c, gm_c, mc[:, 0, 2:], woa, wob, woc, wout,
                          g_ffn[li], w1, w3, w2, li, None, tmc)
    return x
```

```python
import functools
import math

import jax
import jax.numpy as jnp
from jax import lax
from jax.experimental import pallas as pl
from jax.experimental.pallas import tpu as pltpu

F32 = jnp.float32
BF16 = jnp.bfloat16

D_MODEL = 1024
EPS = 1e-6
GRID_W = 64
LRU_W = 512
LRU_BLOCKS = 8
LRU_BLOCK = LRU_W // LRU_BLOCKS
CONV_W = 4
CONV_PAD_L = 2
LRU_C = 8.0
MLA_HEADS = 8
MLA_NOPE = 64
MLA_ROPE = 32
MLA_V = 64
Q_LORA = 384
KV_LORA = 256
ROPE_BASE = 10000.0
RET_HEADS = 4
RET_DK = 64
RET_DV = 128
RET_CHUNK = 256
D_FF = 2816

LANES = 128
SUBLANES = 8
MXU_TILE = 256
VMEM_LIMIT = 56 * 1024 * 1024

KV_GROUPS = (LRU_W, KV_LORA, LANES, LANES, RET_HEADS * RET_DK, RET_HEADS * RET_DV)
Q_GROUPS = (Q_LORA, RET_HEADS * RET_DK, LRU_W, RET_HEADS * RET_DV, 3 * D_MODEL)
N_KV_ARR = sum(KV_GROUPS)


def _params(sem):
    return pltpu.CompilerParams(dimension_semantics=sem, vmem_limit_bytes=VMEM_LIMIT)


def _resident(shape):
    nd = len(shape)
    return pl.BlockSpec(shape, lambda *_: (0,) * nd, pipeline_mode=pl.Buffered(1))


def _layer_spec(stacked, li, cols=None):
    shape = stacked.shape[1:] if cols is None else stacked.shape[1:-1] + (cols,)
    nd = len(shape)
    return pl.BlockSpec((None,) + shape, lambda *_: (li,) + (0,) * nd, pipeline_mode=pl.Buffered(1))


def _rms(x, g):
    y = x * lax.rsqrt(jnp.mean(x * x, axis=-1, keepdims=True) + EPS)
    return y * g


def _mod_body(c_ref, w_ref, b_ref, o_ref):
    cv = c_ref[...]
    act = cv * jax.nn.sigmoid(cv)
    o_ref[0] = jnp.dot(act, w_ref[0], precision=lax.Precision.HIGHEST,
                       preferred_element_type=F32) + b_ref[0]


def _modulation(cvec, w_mod, b_mod):
    depth, d, n = w_mod.shape
    rows = cvec.shape[0]
    tn = n // 4
    return pl.pallas_call(
        _mod_body,
        grid=(depth, n // tn),
        in_specs=[
            pl.BlockSpec((rows, d), lambda l, j: (0, 0)),
            pl.BlockSpec((1, d, tn), lambda l, j: (l, 0, j)),
            pl.BlockSpec((1, 1, tn), lambda l, j: (l, 0, j)),
        ],
        out_specs=pl.BlockSpec((1, rows, tn), lambda l, j: (l, 0, j)),
        out_shape=jax.ShapeDtypeStruct((depth, rows, n), F32),
        compiler_params=_params(("arbitrary", "arbitrary")),
    )(cvec, w_mod, b_mod.reshape(depth, 1, n))


def _inproj_body(x_ref, g_ref, sh_ref, sc_ref, w_ref, *out_refs, widths):
    x = x_ref[0]
    h = _rms(x, g_ref[...]) * (1.0 + sc_ref[0]) + sh_ref[0]
    hb = h.astype(BF16)
    start, off = 0, 0
    for i, wd in enumerate(widths):
        off += wd
        if off % MXU_TILE == 0 or i == len(widths) - 1:
            base = sum(widths[:start])
            z = jnp.dot(hb, w_ref[:, base:off], preferred_element_type=F32)
            col = 0
            for j in range(start, i + 1):
                out_refs[j][0] = z[:, col:col + widths[j]].astype(out_refs[j].dtype)
                col += widths[j]
            start = i + 1


def _inproj(x, g, shift, scale, w_all, li, widths, tm):
    b, l, d = x.shape
    n = sum(widths)
    mod_map = (lambda bi, i: (bi, 0, 0)) if shift.shape[0] > 1 else (lambda bi, i: (0, 0, 0))
    return pl.pallas_call(
        functools.partial(_inproj_body, widths=widths),
        grid=(b, l // tm),
        in_specs=[
            pl.BlockSpec((1, tm, d), lambda bi, i: (bi, i, 0)),
            _resident((1, d)),
            pl.BlockSpec((1, 1, d), mod_map),
            pl.BlockSpec((1, 1, d), mod_map),
            _layer_spec(w_all, li, n),
        ],
        out_specs=[pl.BlockSpec((1, tm, wd), lambda bi, i: (bi, i, 0)) for wd in widths],
        out_shape=[jax.ShapeDtypeStruct((b, l, wd), BF16) for wd in widths],
        compiler_params=_params(("parallel", "parallel")),
    )(x, g.reshape(1, d), shift, scale, w_all)


LRU_TC = 256
LRU_HALO = SUBLANES


def _lru_body(u_ref, uc_ref, cw_ref, cb_ref, wg_ref, bg_ref, lam_ref, *rest, need_ctx):
    if need_ctx:
        out_ref, outc_ref, upad, upadc, ucv, ucvc, hbuf = rest
    else:
        out_ref, upad, upadc, ucv, ucvc, hbuf = rest
        outc_ref = None
    l = u_ref.shape[1]
    lc = uc_ref.shape[1]
    w = u_ref.shape[2]
    tc = LRU_TC
    nt = tc // SUBLANES

    zeros_halo = jnp.zeros((LRU_HALO, w), F32)
    for pad_ref, cv_ref, src_ref, n in ((upad, ucv, u_ref, l), (upadc, ucvc, uc_ref, lc)):
        pad_ref[0:LRU_HALO, :] = zeros_halo
        pad_ref[LRU_HALO + n:2 * LRU_HALO + n, :] = zeros_halo

        def copy(i, carry, pad_ref=pad_ref, src_ref=src_ref):
            t0 = pl.multiple_of(i * tc, tc)
            pad_ref[pl.ds(LRU_HALO + t0, tc), :] = src_ref[0, pl.ds(t0, tc), :].astype(F32)
            return carry

        lax.fori_loop(0, n // tc, copy, 0)

        def conv(i, carry, pad_ref=pad_ref, cv_ref=cv_ref):
            t0 = pl.multiple_of(i * tc, tc)
            win = pad_ref[pl.ds(t0, tc + 2 * LRU_HALO), :]
            u = cb_ref[...]
            for k in range(CONV_W):
                sh = (CONV_PAD_L - k) % (tc + 2 * LRU_HALO)
                tap = win if sh == 0 else pltpu.roll(win, sh, 0)
                u = u + cw_ref[k:k + 1, :] * tap[LRU_HALO:LRU_HALO + tc]
            cv_ref[pl.ds(t0, tc), :] = u
            return carry

        lax.fori_loop(0, n // tc, conv, 0)

    row = lax.broadcasted_iota(jnp.int32, (nt, SUBLANES, w), 1)

    def chunk_scan(cv_ref, t0, d, h):
        u = cv_ref[pl.ds(t0, tc), :]
        g = jnp.dot(u.astype(BF16), wg_ref[d], preferred_element_type=F32) + bg_ref[d]
        r = jax.nn.sigmoid(g[:, :w])
        gi = jax.nn.sigmoid(g[:, w:])
        z = -lam_ref[d]
        sp = jnp.maximum(z, 0.0) + jnp.log1p(jnp.exp(-jnp.abs(z)))
        log_a = r * (-LRU_C * sp)
        a = jnp.exp2(r * (-LRU_C * math.log2(math.e) * sp))
        th = jnp.tanh(log_a)
        om = -2.0 * th / (1.0 - th)
        root = jnp.where(om > 0.0, om * lax.rsqrt(om), 0.0)
        bb = root * (gi * u)
        a = a.reshape(nt, SUBLANES, w)
        bb = bb.reshape(nt, SUBLANES, w)
        s = 1
        while s < SUBLANES:
            if d == 0:
                keep = row >= s
                shift = s
            else:
                keep = row < SUBLANES - s
                shift = SUBLANES - s
            a_s = jnp.where(keep, pltpu.roll(a, shift, 1), 1.0)
            b_s = jnp.where(keep, pltpu.roll(bb, shift, 1), 0.0)
            bb = a * b_s + bb
            a = a * a_s
            s *= 2
        order = range(nt) if d == 0 else range(nt - 1, -1, -1)
        for k in order:
            hk = a[k] * h + bb[k]
            hbuf[k * SUBLANES:(k + 1) * SUBLANES, :] = hk
            h = hk[SUBLANES - 1:SUBLANES] if d == 0 else hk[0:1]
        return h

    def run(cv_ref, n, d, h, write):
        nchunks = n // tc

        def body(i, h):
            ci = i if d == 0 else nchunks - 1 - i
            t0 = pl.multiple_of(ci * tc, tc)
            h = chunk_scan(cv_ref, t0, d, h)
            write(t0)
            return h

        return lax.fori_loop(0, nchunks, body, h, unroll=2)

    def write_accc(t0):
        upadc[pl.ds(t0, tc), :] = hbuf[...]

    def write_acc(t0):
        upad[pl.ds(t0, tc), :] = hbuf[...]

    def write_outc(t0):
        if outc_ref is not None:
            outc_ref[0, pl.ds(t0, tc), :] = (upadc[pl.ds(t0, tc), :] + hbuf[...]).astype(outc_ref.dtype)

    def write_out(t0):
        out_ref[0, pl.ds(t0, tc), :] = (upad[pl.ds(t0, tc), :] + hbuf[...]).astype(out_ref.dtype)

    h0 = jnp.zeros((1, w), F32)
    h = run(ucvc, lc, 0, h0, write_accc)
    run(ucv, l, 0, h, write_acc)
    h = run(ucvc, lc, 1, h0, write_outc)
    run(ucv, l, 1, h, write_out)


def _lru(u, uc, conv_w, conv_b, wg, bg, lam, need_ctx):
    b, l, w = u.shape
    lc = uc.shape[1]
    out_shape = [jax.ShapeDtypeStruct((b, l, w), BF16)]
    out_specs = [pl.BlockSpec((1, l, w), lambda bi: (bi, 0, 0))]
    if need_ctx:
        out_shape.append(jax.ShapeDtypeStruct((b, lc, w), BF16))
        out_specs.append(pl.BlockSpec((1, lc, w), lambda bi: (bi, 0, 0)))
    res = pl.pallas_call(
        functools.partial(_lru_body, need_ctx=need_ctx),
        grid=(b,),
        in_specs=[
            pl.BlockSpec((1, l, w), lambda bi: (bi, 0, 0)),
            pl.BlockSpec((1, lc, w), lambda bi: (bi, 0, 0)),
            _resident((CONV_W, w)),
            _resident((1, w)),
            _resident((2, w, 2 * w)),
            _resident((2, 1, 2 * w)),
            _resident((2, 1, w)),
        ],
        out_specs=out_specs,
        out_shape=out_shape,
        scratch_shapes=[
            pltpu.VMEM((l + 2 * LRU_HALO, w), F32),
            pltpu.VMEM((lc + 2 * LRU_HALO, w), F32),
            pltpu.VMEM((l, w), F32),
            pltpu.VMEM((lc, w), F32),
            pltpu.VMEM((LRU_TC, w), F32),
        ],
        compiler_params=_params(("parallel",)),
    )(u, uc, conv_w, conv_b.reshape(1, w), wg, bg, lam.reshape(2, 1, w))
    return (res[0], res[1]) if need_ctx else (res[0], None)


def _mla_prep_body(*refs, rotate, need_q):
    it = iter(refs)
    cq_ref = next(it) if need_q else None
    ckv_ref, ra_ref = next(it), next(it)
    rb_ref = next(it) if rotate else None
    if need_q:
        cosq_ref = next(it)
        sinq_ref = next(it) if rotate else None
    cosk_ref = next(it)
    sink_ref = next(it) if rotate else None
    if need_q:
        gq_ref, wq_ref = next(it), next(it)
        wqs_ref = next(it) if rotate else None
    gkv_ref, wkv_ref = next(it), next(it)
    q_out = next(it) if need_q else None
    k_out, v_out = next(it), next(it)
    nh = MLA_HEADS
    tm = ckv_ref.shape[1]

    if need_q:
        cqn = _rms(cq_ref[0].astype(F32), gq_ref[...]).astype(BF16)
        qm = jnp.dot(cqn, wq_ref[...], preferred_element_type=F32)
        cosq = cosq_ref[...]
        if rotate:
            qs = jnp.dot(cqn, wqs_ref[...], preferred_element_type=F32)
            sinq = sinq_ref[...]
        for h in range(nh):
            sl = slice(h * LANES, (h + 1) * LANES)
            qh = qm[:, sl] * cosq
            if rotate:
                qh = qh + qs[:, sl] * sinq
            q_out[0, h] = qh.astype(q_out.dtype)

    kvn = _rms(ckv_ref[0].astype(F32), gkv_ref[...]).astype(BF16)
    kv = jnp.dot(kvn, wkv_ref[...], preferred_element_type=F32)
    rope = ra_ref[0].astype(F32) * cosk_ref[...]
    if rotate:
        rope = rope + rb_ref[0].astype(F32) * sink_ref[...]
    lane = lax.broadcasted_iota(jnp.int32, (tm, LANES), 1)
    for h in range(nh):
        k_out[0, h] = (kv[:, h * LANES:(h + 1) * LANES] + rope).astype(k_out.dtype)
        vh = kv[:, (nh + h) * LANES:(nh + h + 1) * LANES]
        v_out[0, h] = jnp.where(lane == _sum_lane(h), 1.0, vh).astype(v_out.dtype)


def _sum_lane(h):
    return MLA_V if h % 2 == 0 else 0


def _mla_prep(cq, ckv, ra, rb, cosq, sinq, cosk, sink, gq, wq, wqs, gkv, wkv, rotate, need_q, tm):
    b, l, _ = ckv.shape
    nh = MLA_HEADS
    row = lambda wd: pl.BlockSpec((1, tm, wd), lambda bi, i: (bi, i, 0))
    tab = pl.BlockSpec((tm, LANES), lambda bi, i: (i, 0))
    args, specs = [], []

    def add(a, s):
        args.append(a)
        specs.append(s)

    if need_q:
        add(cq, row(Q_LORA))
    add(ckv, row(KV_LORA))
    add(ra, row(LANES))
    if rotate:
        add(rb, row(LANES))
    if need_q:
        add(cosq, tab)
        if rotate:
            add(sinq, tab)
    add(cosk, tab)
    if rotate:
        add(sink, tab)
    if need_q:
        add(gq.reshape(1, Q_LORA), _resident((1, Q_LORA)))
        add(wq, _resident(wq.shape))
        if rotate:
            add(wqs, _resident(wqs.shape))
    add(gkv.reshape(1, KV_LORA), _resident((1, KV_LORA)))
    add(wkv, _resident(wkv.shape))

    head = pl.BlockSpec((1, nh, tm, LANES), lambda bi, i: (bi, 0, i, 0))
    hshape = jax.ShapeDtypeStruct((b, nh, l, LANES), ATTN_DT)
    n_out = 3 if need_q else 2
    res = pl.pallas_call(
        functools.partial(_mla_prep_body, rotate=rotate, need_q=need_q),
        grid=(b, l // tm),
        in_specs=specs,
        out_specs=[head] * n_out,
        out_shape=[hshape] * n_out,
        compiler_params=_params(("parallel", "parallel")),
    )(*args)
    return res if need_q else (None, res[0], res[1])


ATTN_PAIRS_PER_STEP = 4
ATTN_DT = jnp.float8_e4m3fn
ATTN_P_SHIFT = 8.0


def _attn_body(q_ref, *refs, n_kv, tq):
    kv_refs = refs[:2 * n_kv]
    o_ref = refs[2 * n_kv]
    s_bufs = refs[2 * n_kv + 1:2 * n_kv + 3]
    p_bufs = refs[2 * n_kv + 3:2 * n_kv + 5]
    m_bufs = refs[2 * n_kv + 5:2 * n_kv + 7]
    npair = ATTN_PAIRS_PER_STEP
    n_items = (q_ref.shape[2] // tq) * npair
    lks = [kv_refs[2 * j].shape[2] for j in range(n_kv)]
    offs = [sum(lks[:j]) for j in range(n_kv)]
    lane = lax.broadcasted_iota(jnp.int32, (tq, LANES), 1)
    nt_dims = (((1,), (1,)), ((), ()))

    s_bufs[1][...] = jnp.zeros(s_bufs[1].shape, s_bufs[1].dtype)
    p_bufs[0][...] = jnp.zeros(p_bufs[0].shape, p_bufs[0].dtype)
    m_bufs[1][...] = jnp.zeros(m_bufs[1].shape, m_bufs[1].dtype)

    def item(t):
        tile = lax.shift_right_logical(t, npair.bit_length() - 1)
        pair = jnp.bitwise_and(t, npair - 1)
        return pl.multiple_of(tile * tq, tq), pair

    def stages(t, slot):
        s_a, s_b = s_bufs[slot], s_bufs[1 - slot]
        p_b, p_c = p_bufs[1 - slot], p_bufs[slot]
        m_a, m_b = m_bufs[slot], m_bufs[1 - slot]

        row_a, pair_a = item(jnp.minimum(t, n_items - 1))
        for e in range(2):
            h = pair_a * 2 + e
            q = q_ref[0, h, pl.ds(row_a, tq), :]
            bm = None
            for j in range(n_kv):
                sj = lax.dot_general(q, kv_refs[2 * j][0, h], nt_dims,
                                     preferred_element_type=F32).astype(s_a.dtype)
                s_a[e, :, offs[j]:offs[j] + lks[j]] = sj
                for c0 in range(0, lks[j], LANES):
                    blk = sj[:, c0:c0 + LANES]
                    bm = blk if bm is None else jnp.maximum(bm, blk)
            m_a[e] = bm

        for e in range(2):
            s = s_b[e]
            m = m_b[e].astype(F32).max(axis=-1, keepdims=True)
            shift = (m - ATTN_P_SHIFT).astype(s.dtype)
            p_b[e] = jnp.exp2(s - shift).astype(p_b.dtype)

        row_c, pair_c = item(jnp.maximum(t - 2, 0))
        halves = []
        for e in range(2):
            h = pair_c * 2 + e
            o = None
            for j in range(n_kv):
                oj = jnp.dot(p_c[e, :, offs[j]:offs[j] + lks[j]], kv_refs[2 * j + 1][0, h],
                             preferred_element_type=F32)
                o = oj if o is None else o + oj
            one = _sum_lane(e)
            halves.append(o * (1.0 / o[:, one:one + 1]))
        blk = jnp.where(lane < MLA_V, halves[0], halves[1])
        o_ref[0, pair_c, pl.ds(row_c, tq), :] = blk.astype(o_ref.dtype)

    def body(i, carry):
        stages(2 * i, 0)
        stages(2 * i + 1, 1)
        return carry

    lax.fori_loop(0, (n_items + 2) // 2, body, 0)


def _attention(q, kvs, tq):
    b, nh, lq, _ = q.shape
    hs = 2 * ATTN_PAIRS_PER_STEP
    once = lambda n: pl.BlockSpec((1, hs, n, LANES), lambda bi, j: (bi, j, 0, 0))
    specs = [once(lq)]
    args = [q]
    lkt = 0
    for k, v in kvs:
        lk = k.shape[2]
        lkt += lk
        specs += [once(lk), once(lk)]
        args += [k, v]
    return pl.pallas_call(
        functools.partial(_attn_body, n_kv=len(kvs), tq=tq),
        grid=(b, nh // hs),
        in_specs=specs,
        out_specs=pl.BlockSpec((1, ATTN_PAIRS_PER_STEP, lq, LANES), lambda bi, j: (bi, j, 0, 0)),
        out_shape=jax.ShapeDtypeStruct((b, nh // 2, lq, LANES), BF16),
        scratch_shapes=[
            pltpu.VMEM((2, tq, lkt), BF16),
            pltpu.VMEM((2, tq, lkt), BF16),
            pltpu.VMEM((2, tq, lkt), ATTN_DT),
            pltpu.VMEM((2, tq, lkt), ATTN_DT),
            pltpu.VMEM((2, tq, LANES), BF16),
            pltpu.VMEM((2, tq, LANES), BF16),
        ],
        compiler_params=_params(("parallel", "parallel")),
    )(*args)


def _ret_body(*refs, need_ctx):
    it = iter(refs)
    q_ref, k_ref, v_ref = next(it), next(it), next(it)
    qc_ref = next(it) if need_ctx else None
    kc_ref, vc_ref = next(it), next(it)
    cos_ref, sin_ref = next(it), next(it)
    msum_ref, qdec_ref, kdec_ref, cd_ref = next(it), next(it), next(it), next(it)
    o_ref = next(it)
    oc_ref = next(it) if need_ctx else None
    qa, ka, va, u_s, r_s = (next(it) for _ in range(5))

    c = RET_CHUNK
    l = q_ref.shape[1]
    lc = kc_ref.shape[1]
    ncc = lc // c
    nt = (lc + l) // c
    kscale = RET_DK ** -0.5
    lane = lax.broadcasted_iota(jnp.int32, (c, LANES), 1)
    first_half = (lane % RET_DK) < (RET_DK // 2)

    def rotated(src_ref, t0, g0):
        x = src_ref[0, pl.ds(t0, c), :].astype(F32)
        sw = jnp.where(first_half, pltpu.roll(x, LANES - RET_DK // 2, 1),
                       pltpu.roll(x, RET_DK // 2, 1))
        return x * cos_ref[pl.ds(g0, c), :] + sw * sin_ref[pl.ds(g0, c), :]

    def stage(qsrc, ksrc, vsrc, n, base):
        def body(i, carry):
            t0 = pl.multiple_of(i * c, c)
            g0 = pl.multiple_of(base + i * c, c)
            if qsrc is not None:
                qa[pl.ds(g0, c), :] = rotated(qsrc, t0, g0)
            ka[pl.ds(g0, c), :] = rotated(ksrc, t0, g0) * kscale
            va[pl.ds(g0, c), :] = vsrc[0, pl.ds(t0, c), :]
            return carry

        lax.fori_loop(0, n // c, body, 0, unroll=2)

    stage(qc_ref, kc_ref, vc_ref, lc, 0)
    stage(q_ref, k_ref, v_ref, l, lc)

    tn_dims = (((0,), (0,)), ((), ()))
    nt_dims = (((1,), (1,)), ((), ()))

    def incr(g, carry):
        g0 = pl.multiple_of(g * c, c)
        kf = ka[pl.ds(g0, c), :]
        for e in range(2):
            v = va[pl.ds(g0, c), e * RET_DV:(e + 1) * RET_DV]
            kd = jnp.concatenate([kf * kdec_ref[0, e, 0], kf * kdec_ref[0, e, 1]], axis=1)
            u_s[g, e] = lax.dot_general(kd.astype(BF16), v, tn_dims, preferred_element_type=F32)
        return carry

    lax.fori_loop(0, nt, incr, 0, unroll=4)

    zero = tuple(jnp.zeros((LANES, RET_DV), F32) for _ in range(4))

    def recur(t, st):
        gs = (t, jnp.where(t < ncc, ncc - 1 - t, nt - 1 + ncc - t))
        new = []
        for d in range(2):
            rows = slice(d * LANES, (d + 1) * LANES)
            for e in range(2):
                r = st[2 * d + e]
                r_s[gs[d], e, rows, :] = r.astype(r_s.dtype)
                new.append(cd_ref[0, e, d] * r + u_s[gs[d], e, rows, :])
        return tuple(new)

    lax.fori_loop(0, nt, recur, zero, unroll=2)

    def emit(dst_ref, gbase):
        def body(i, carry):
            t0 = pl.multiple_of(i * c, c)
            g = gbase + i
            g0 = pl.multiple_of(g * c, c)
            qf = qa[pl.ds(g0, c), :]
            kb = ka[pl.ds(g0, c), :].astype(BF16)
            parts = []
            for e in range(2):
                v = va[pl.ds(g0, c), e * RET_DV:(e + 1) * RET_DV]
                qm = jnp.where((lane // RET_DK) == e, qf, 0.0).astype(BF16)
                s = lax.dot_general(qm, kb, nt_dims, preferred_element_type=F32)
                o = jnp.dot((s * msum_ref[0, e]).astype(BF16), v, preferred_element_type=F32)
                qd = jnp.concatenate([qf * qdec_ref[0, e, 0], qf * qdec_ref[0, e, 1]], axis=1)
                o = o + jnp.dot(qd.astype(BF16), r_s[g, e], preferred_element_type=F32)
                mu = jnp.mean(o, axis=-1, keepdims=True)
                oc = o - mu
                var = jnp.mean(oc * oc, axis=-1, keepdims=True)
                parts.append(oc * lax.rsqrt(var + EPS))
            dst_ref[0, pl.ds(t0, c), :] = jnp.concatenate(parts, axis=1).astype(dst_ref.dtype)
            return carry

        return body

    if need_ctx:
        lax.fori_loop(0, ncc, emit(oc_ref, 0), 0, unroll=2)
    lax.fori_loop(0, nt - ncc, emit(o_ref, ncc), 0, unroll=8)


def _retention(q, k, v, qc, kc, vc, tabs, dec, need_ctx):
    b, l, _ = q.shape
    lc = kc.shape[1]
    lt = lc + l
    c = RET_CHUNK
    nt = lt // c
    npair = RET_HEADS // 2
    seq = lambda n, wd: pl.BlockSpec((1, n, wd), lambda bi, p: (bi, 0, p))
    args = [q, k, v]
    specs = [seq(l, LANES), seq(l, LANES), seq(l, 2 * RET_DV)]
    if need_ctx:
        args.append(qc)
        specs.append(seq(lc, LANES))
    args += [kc, vc]
    specs += [seq(lc, LANES), seq(lc, 2 * RET_DV)]
    for t in tabs:
        args.append(t)
        specs.append(_resident((lt, LANES)))
    msum, qdec, kdec, cd = dec
    args += [msum, qdec, kdec, cd]
    specs += [
        pl.BlockSpec((1, 2, c, c), lambda bi, p: (p, 0, 0, 0)),
        pl.BlockSpec((1, 2, 2, c, LANES), lambda bi, p: (p, 0, 0, 0, 0)),
        pl.BlockSpec((1, 2, 2, c, LANES), lambda bi, p: (p, 0, 0, 0, 0)),
        pl.BlockSpec((1, 2, 2, 1, RET_DV), lambda bi, p: (p, 0, 0, 0, 0)),
    ]
    out_shape = [jax.ShapeDtypeStruct((b, l, RET_HEADS * RET_DV), BF16)]
    out_specs = [seq(l, 2 * RET_DV)]
    if need_ctx:
        out_shape.append(jax.ShapeDtypeStruct((b, lc, RET_HEADS * RET_DV), BF16))
        out_specs.append(seq(lc, 2 * RET_DV))
    res = pl.pallas_call(
        functools.partial(_ret_body, need_ctx=need_ctx),
        grid=(b, npair),
        in_specs=specs,
        out_specs=out_specs,
        out_shape=out_shape,
        scratch_shapes=[
            pltpu.VMEM((lt, LANES), F32),
            pltpu.VMEM((lt, LANES), F32),
            pltpu.VMEM((lt, 2 * RET_DV), BF16),
            pltpu.VMEM((nt, 2, 2 * LANES, RET_DV), F32),
            pltpu.VMEM((nt, 2, 2 * LANES, RET_DV), BF16),
        ],
        compiler_params=_params(("parallel", "parallel")),
    )(*args)
    return (res[0], res[1]) if need_ctx else (res[0], None)


FFN_CHUNKS = ((0, 6 * MXU_TILE), (6 * MXU_TILE, D_FF))
assert all(lo % MXU_TILE == 0 and hi % MXU_TILE == 0 for lo, hi in FFN_CHUNKS)


def _mix_ffn_body(x_ref, rec_ref, gl_ref, yb_ref, rn_ref, gr_ref, gm_ref, mod_ref,
                  woa_ref, wob_ref, woc_ref, wout_ref, g_ref, w1_ref, w3_ref, w2_ref, *rest, final):
    if final:
        gf_ref, o_ref = rest
    else:
        (o_ref,) = rest
    d = x_ref.shape[2]
    ga_a, sh_f, sc_f, ga_f = (mod_ref[0, j:j + 1, :] for j in range(4))

    ya = (jax.nn.gelu(gl_ref[0].astype(F32)) * rec_ref[0].astype(F32)).astype(BF16)
    yc = (jax.nn.silu(gr_ref[0].astype(F32)) * rn_ref[0].astype(F32)).astype(BF16)
    yb = jnp.concatenate([yb_ref[0, p] for p in range(yb_ref.shape[1])], axis=1)
    gm = gm_ref[0]
    m = jax.nn.sigmoid(gm[:, :d].astype(F32)) * jnp.dot(ya, woa_ref[...], preferred_element_type=F32)
    m = m + jax.nn.sigmoid(gm[:, d:2 * d].astype(F32)) * jnp.dot(
        yb, wob_ref[...], preferred_element_type=F32)
    m = m + jax.nn.sigmoid(gm[:, 2 * d:].astype(F32)) * jnp.dot(
        yc, woc_ref[...], preferred_element_type=F32)
    x = x_ref[0] + ga_a * jnp.dot(m.astype(BF16), wout_ref[...], preferred_element_type=F32)

    hb = (_rms(x, g_ref[...]) * (1.0 + sc_f) + sh_f).astype(BF16)
    acc = None
    for c0, c1 in FFN_CHUNKS:
        a = jnp.dot(hb, w1_ref[:, c0:c1], preferred_element_type=F32)
        bgate = jnp.dot(hb, w3_ref[:, c0:c1], preferred_element_type=F32)
        t = (jax.nn.silu(a) * bgate).astype(BF16)
        part = jnp.dot(t, w2_ref[c0:c1, :], preferred_element_type=F32)
        acc = part if acc is None else acc + part
    y = x + ga_f * acc
    if final:
        y = _rms(y, gf_ref[...])
    o_ref[0] = y


def _mix_ffn(x, rec, gl, yb, rn, gr, gm, mod4, woa, wob, woc, wout, g, w1, w3, w2, li, g_final, tm):
    b, l, d = x.shape
    final = g_final is not None
    row = lambda wd: pl.BlockSpec((1, tm, wd), lambda bi, i: (bi, i, 0))
    mod_map = (lambda bi, i: (bi, 0, 0)) if mod4.shape[0] > 1 else (lambda bi, i: (0, 0, 0))
    args = [x, rec, gl, yb, rn, gr, gm, mod4, woa, wob, woc, wout, g.reshape(1, d), w1, w3, w2]
    specs = [
        row(d), row(LRU_W), row(LRU_W),
        pl.BlockSpec((1, MLA_HEADS // 2, tm, LANES), lambda bi, i: (bi, 0, i, 0)),
        row(RET_HEADS * RET_DV), row(RET_HEADS * RET_DV), row(3 * d),
        pl.BlockSpec((1, 4, d), mod_map),
        _layer_spec(woa, li), _layer_spec(wob, li), _layer_spec(woc, li), _layer_spec(wout, li),
        _resident((1, d)),
        _layer_spec(w1, li), _layer_spec(w3, li), _layer_spec(w2, li),
    ]
    if final:
        args.append(g_final.reshape(1, d))
        specs.append(_resident((1, d)))
    return pl.pallas_call(
        functools.partial(_mix_ffn_body, final=final),
        grid=(b, l // tm),
        in_specs=specs,
        out_specs=row(d),
        out_shape=jax.ShapeDtypeStruct((b, l, d), F32),
        compiler_params=_params(("parallel", "parallel")),
    )(*args)


def _swap_half(w):
    half = w.shape[-1] // 2
    return jnp.concatenate([-w[..., half:], w[..., :half]], axis=-1)


def _arrange_w_in(w_in):
    k0 = LRU_W + KV_LORA
    kr = w_in[..., k0:k0 + MLA_ROPE]
    z = lambda n: jnp.zeros(w_in.shape[:-1] + (n,), w_in.dtype)
    pad = LANES - MLA_NOPE - MLA_ROPE
    rope_a = jnp.concatenate([z(MLA_NOPE), kr, z(pad)], axis=-1)
    rope_b = jnp.concatenate([z(MLA_NOPE), _swap_half(kr), z(pad)], axis=-1)
    cols = [w_in[..., :k0], rope_a, rope_b, w_in[..., k0 + MLA_ROPE:]]
    return jnp.concatenate(cols, axis=-1).astype(BF16)


def _arrange_mla(w_uq, w_ukv):
    nh = MLA_HEADS
    wq = w_uq.reshape(Q_LORA, nh, MLA_NOPE + MLA_ROPE)
    padq = jnp.zeros((Q_LORA, nh, LANES - MLA_NOPE - MLA_ROPE), w_uq.dtype)
    wq_arr = jnp.concatenate([wq, padq], axis=-1).reshape(Q_LORA, nh * LANES)
    wqs = jnp.concatenate([jnp.zeros((Q_LORA, nh, MLA_NOPE), w_uq.dtype),
                           _swap_half(wq[..., MLA_NOPE:]), padq], axis=-1)
    wqs_arr = wqs.reshape(Q_LORA, nh * LANES)
    wkv = w_ukv.reshape(KV_LORA, nh, MLA_NOPE + MLA_V)
    zk = jnp.zeros((KV_LORA, nh, LANES - MLA_NOPE), w_ukv.dtype)
    wk_arr = jnp.concatenate([wkv[..., :MLA_NOPE], zk], axis=-1).reshape(KV_LORA, nh * LANES)
    wv = wkv[..., MLA_NOPE:]
    zv = jnp.zeros_like(wv)
    even = (jnp.arange(nh) % 2 == 0)[None, :, None]
    wv_arr = jnp.concatenate([jnp.where(even, wv, zv), jnp.where(even, zv, wv)], axis=-1)
    wv_arr = wv_arr.reshape(KV_LORA, nh * LANES)
    return (wq_arr.astype(BF16), wqs_arr.astype(BF16),
            jnp.concatenate([wk_arr, wv_arr], axis=1).astype(BF16))


def _arrange_lru(wa, ba, wx, bx):
    def dense(wblk):
        eye = jnp.eye(LRU_BLOCKS, dtype=wblk.dtype)
        full = jnp.einsum("nde,nm->ndme", wblk, eye)
        return full.reshape(LRU_W, LRU_W)

    wg = jnp.stack([jnp.concatenate([dense(wa[d]), dense(wx[d])], axis=1) for d in range(2)])
    bg = jnp.stack([jnp.concatenate([ba[d], bx[d]])[None, :] for d in range(2)])
    return wg.astype(BF16), bg


def _mla_tables(n_tokens, rotate):
    scale = (MLA_NOPE + MLA_ROPE) ** -0.5 * math.log2(math.e)
    pad = LANES - MLA_NOPE - MLA_ROPE
    ones = jnp.ones((n_tokens, MLA_NOPE), F32)
    zeros = jnp.zeros((n_tokens, MLA_NOPE), F32)
    zpad = jnp.zeros((n_tokens, pad), F32)
    if rotate:
        rows = n_tokens // GRID_W
        rowi = jnp.repeat(jnp.arange(rows, dtype=F32), GRID_W)
        coli = jnp.tile(jnp.arange(GRID_W, dtype=F32), rows)
        n_freq = MLA_ROPE // 4
        inv = jnp.power(ROPE_BASE, -jnp.arange(n_freq, dtype=F32) / n_freq)
        ang = jnp.concatenate([rowi[:, None] * inv, coli[:, None] * inv], axis=-1)
        cos, sin = jnp.cos(ang), jnp.sin(ang)
    else:
        cos = jnp.ones((n_tokens, MLA_ROPE // 2), F32)
        sin = jnp.zeros((n_tokens, MLA_ROPE // 2), F32)
    cosq = jnp.concatenate([ones, cos, cos, zpad], axis=1) * scale
    sinq = jnp.concatenate([zeros, sin, sin, zpad], axis=1) * scale
    cosk = jnp.concatenate([zeros, cos, cos, zpad], axis=1)
    sink = jnp.concatenate([zeros, sin, sin, zpad], axis=1)
    return cosq, sinq, cosk, sink


def _ret_tables(n):
    theta = 1.0 / jnp.power(10000.0, jnp.linspace(0.0, 1.0, RET_DK // 2, dtype=F32))
    pos = jnp.arange(n, dtype=F32)
    ang = pos[:, None] * theta
    cos, sin = jnp.cos(ang), jnp.sin(ang)
    reps = LANES // RET_DK
    cos_t = jnp.tile(jnp.concatenate([cos, cos], axis=1), (1, reps))
    sin_t = jnp.tile(jnp.concatenate([-sin, sin], axis=1), (1, reps))
    return cos_t, sin_t


def _ret_decay_tables():
    c = RET_CHUNK
    h = jnp.arange(RET_HEADS, dtype=F32)
    lgs = jnp.stack([jnp.log1p(-jnp.exp2(-5.0 - h)), jnp.log1p(-jnp.exp2(-5.5 - h))], axis=1)
    pos = jnp.arange(c, dtype=F32)
    diff = pos[:, None] - pos[None, :]
    lg = lgs[:, :, None, None]
    inner = jnp.where(diff >= 0, jnp.exp(lg * jnp.maximum(diff, 0.0)), 0.0)
    msum = inner[:, 0] + jnp.swapaxes(inner[:, 1], -1, -2)
    lg1 = lgs[:, :, None]
    qd_f = jnp.exp(lg1 * (pos + 1.0))
    kd_f = jnp.exp(lg1 * (c - 1.0 - pos))
    qd = jnp.stack([qd_f[:, 0], qd_f[:, 1, ::-1]], axis=1)
    kd = jnp.stack([kd_f[:, 0], kd_f[:, 1, ::-1]], axis=1)
    lane_head = (jnp.arange(LANES) // RET_DK)
    e_of_h = jnp.arange(RET_HEADS) % 2
    mask = (lane_head[None, :] == e_of_h[:, None]).astype(F32)
    qdec = qd[..., None] * mask[:, None, None, :]
    kdec = kd[..., None] * mask[:, None, None, :]
    cd = jnp.broadcast_to(jnp.exp(lgs * c)[:, :, None, None], (RET_HEADS, 2, 1, RET_DV))
    pair = lambda t: t.reshape(RET_HEADS // 2, 2, *t.shape[1:])
    return pair(msum), pair(qdec), pair(kdec), pair(cd)


def _row_tile(n, pref):
    t = min(pref, n)
    while n % t:
        t //= 2
    return t


def kernel(x, c, ctx, c_ctx, w_mod, b_mod, g_mix, g_ffn, w_in, conv_w, conv_b, lru_wa, lru_ba,
           lru_wx, lru_bx, lru_lam, g_q, w_uq, g_kv, w_ukv, w_oa, w_ob, w_oc, w_out,
           w_ff1, w_ff3, w_ff2, g_final):
    b, l, d = x.shape
    lc = ctx.shape[1]
    depth = w_mod.shape[0]
    assert d == D_MODEL and l % LRU_TC == 0 and lc % LRU_TC == 0 and l % GRID_W == 0

    tm = _row_tile(l, 512)
    tmc = _row_tile(lc, 256)
    tq = _row_tile(l, 256)
    tqc = _row_tile(lc, 256)

    lat_tabs = _mla_tables(l, True)
    ctx_tabs = _mla_tables(lc, False)
    ret_tabs = _ret_tables(lc + l)
    dec = _ret_decay_tables()

    rows = -(-(b + 1) // SUBLANES) * SUBLANES
    cvec = jnp.zeros((rows, d), F32).at[:b].set(c).at[b].set(c_ctx)
    mod_all = _modulation(cvec, w_mod, b_mod)

    w_arr = _arrange_w_in(w_in)
    woa, wob, woc, wout = (w.astype(BF16) for w in (w_oa, w_ob, w_oc, w_out))
    w1, w3, w2 = (w.astype(BF16) for w in (w_ff1, w_ff3, w_ff2))

    xc = ctx
    for li in range(depth):
        last = li == depth - 1
        mod = mod_all[li, :b].reshape(b, 1, 6, d)
        sh_a, sc_a = mod[:, :, 0], mod[:, :, 1]
        mc = mod_all[li, b].reshape(1, 1, 6, d)
        mcs = [mc[:, :, j] for j in range(6)]

        wq_arr, wqs_arr, wkv_arr = _arrange_mla(w_uq[li], w_ukv[li])
        wg, bg = _arrange_lru(lru_wa[li], lru_ba[li], lru_wx[li], lru_bx[li])

        z = _inproj(x, g_mix[li], sh_a, sc_a, w_arr, li, KV_GROUPS + Q_GROUPS, tm)
        u, ckv, ra, rb, rk, rv, cq, rq, g_lru, g_ret, gm = z
        if last:
            zc = _inproj(xc, g_mix[li], mcs[0], mcs[1], w_arr, li, KV_GROUPS, tmc)
            uc, ckvc, rac, _, rkc, rvc = zc
            cqc = rqc = None
        else:
            zc = _inproj(xc, g_mix[li], mcs[0], mcs[1], w_arr, li, KV_GROUPS + Q_GROUPS, tmc)
            uc, ckvc, rac, _, rkc, rvc, cqc, rqc, g_lru_c, g_ret_c, gm_c = zc
        need_ctx = not last

        rec, rec_c = _lru(u, uc, conv_w[li], conv_b[li], wg, bg, lru_lam[li], need_ctx)

        q, k, v = _mla_prep(cq, ckv, ra, rb, *lat_tabs, g_q[li], wq_arr, wqs_arr, g_kv[li],
                            wkv_arr, True, True, _row_tile(l, 1024))
        qc, kc, vc = _mla_prep(cqc, ckvc, rac, None, *ctx_tabs, g_q[li], wq_arr, None, g_kv[li],
                               wkv_arr, False, need_ctx, tmc)
        yb = _attention(q, [(kc, vc), (k, v)], tq)

        rn, rn_c = _retention(rq, rk, rv, rqc, rkc, rvc, ret_tabs, dec, need_ctx)

        x = _mix_ffn(x, rec, g_lru, yb, rn, g_ret, gm, mod[:, 0, 2:], woa, wob, woc, wout,
                     g_ffn[li], w1, w3, w2, li, g_final if last else None, tm)
        if need_ctx:
            yb_c = _attention(qc, [(kc, vc)], tqc)
            xc = _mix_ffn(xc, rec_c, g_lru_c, yb_c, rn_c, g_ret_c, gm_c, mc[:, 0, 2:], woa, wob, woc, wout,
                          g_ffn[li], w1, w3, w2, li, None, tmc)
    return x
```

```python
import functools
import math

import jax
import jax.numpy as jnp
from jax import lax
from jax.experimental import pallas as pl
from jax.experimental.pallas import tpu as pltpu

F32 = jnp.float32
BF16 = jnp.bfloat16

D_MODEL = 1024
EPS = 1e-6
GRID_W = 64
LRU_W = 512
LRU_BLOCKS = 8
LRU_BLOCK = LRU_W // LRU_BLOCKS
CONV_W = 4
CONV_PAD_L = 2
LRU_C = 8.0
MLA_HEADS = 8
MLA_NOPE = 64
MLA_ROPE = 32
MLA_V = 64
Q_LORA = 384
KV_LORA = 256
ROPE_BASE = 10000.0
RET_HEADS = 4
RET_DK = 64
RET_DV = 128
RET_CHUNK = 256
D_FF = 2816

LANES = 128
SUBLANES = 8
MXU_TILE = 256
VMEM_LIMIT = 56 * 1024 * 1024

KV_GROUPS = (LRU_W, KV_LORA, LANES, RET_HEADS * RET_DK, RET_HEADS * RET_DV)
Q_GROUPS = (Q_LORA, RET_HEADS * RET_DK, LRU_W, RET_HEADS * RET_DV, 3 * D_MODEL)
N_KV_ARR = sum(KV_GROUPS)


def _params(sem):
    return pltpu.CompilerParams(dimension_semantics=sem, vmem_limit_bytes=VMEM_LIMIT)


def _resident(shape):
    nd = len(shape)
    return pl.BlockSpec(shape, lambda *_: (0,) * nd, pipeline_mode=pl.Buffered(1))


def _layer_spec(stacked, li, cols=None):
    shape = stacked.shape[1:] if cols is None else stacked.shape[1:-1] + (cols,)
    nd = len(shape)
    return pl.BlockSpec((None,) + shape, lambda *_: (li,) + (0,) * nd, pipeline_mode=pl.Buffered(1))


def _rms(x, g):
    y = x * lax.rsqrt(jnp.mean(x * x, axis=-1, keepdims=True) + EPS)
    return y * g


def _mod_body(c_ref, w_ref, b_ref, o_ref):
    cv = c_ref[...]
    act = cv * jax.nn.sigmoid(cv)
    o_ref[0] = jnp.dot(act, w_ref[0], precision=lax.Precision.HIGHEST,
                       preferred_element_type=F32) + b_ref[0]


def _modulation(cvec, w_mod, b_mod):
    depth, d, n = w_mod.shape
    rows = cvec.shape[0]
    tn = n // 4
    return pl.pallas_call(
        _mod_body,
        grid=(depth, n // tn),
        in_specs=[
            pl.BlockSpec((rows, d), lambda l, j: (0, 0)),
            pl.BlockSpec((1, d, tn), lambda l, j: (l, 0, j)),
            pl.BlockSpec((1, 1, tn), lambda l, j: (l, 0, j)),
        ],
        out_specs=pl.BlockSpec((1, rows, tn), lambda l, j: (l, 0, j)),
        out_shape=jax.ShapeDtypeStruct((depth, rows, n), F32),
        compiler_params=_params(("arbitrary", "arbitrary")),
    )(cvec, w_mod, b_mod.reshape(depth, 1, n))


def _inproj_body(x_ref, g_ref, sh_ref, sc_ref, w_ref, *out_refs, widths):
    x = x_ref[0]
    h = _rms(x, g_ref[...]) * (1.0 + sc_ref[0]) + sh_ref[0]
    hb = h.astype(BF16)
    start, off = 0, 0
    for i, wd in enumerate(widths):
        off += wd
        if off % MXU_TILE == 0 or i == len(widths) - 1:
            base = sum(widths[:start])
            z = jnp.dot(hb, w_ref[:, base:off], preferred_element_type=F32)
            col = 0
            for j in range(start, i + 1):
                out_refs[j][0] = z[:, col:col + widths[j]].astype(out_refs[j].dtype)
                col += widths[j]
            start = i + 1


def _inproj(x, g, shift, scale, w_all, li, widths, tm):
    b, l, d = x.shape
    n = sum(widths)
    mod_map = (lambda bi, i: (bi, 0, 0)) if shift.shape[0] > 1 else (lambda bi, i: (0, 0, 0))
    return pl.pallas_call(
        functools.partial(_inproj_body, widths=widths),
        grid=(b, l // tm),
        in_specs=[
            pl.BlockSpec((1, tm, d), lambda bi, i: (bi, i, 0)),
            _resident((1, d)),
            pl.BlockSpec((1, 1, d), mod_map),
            pl.BlockSpec((1, 1, d), mod_map),
            _layer_spec(w_all, li, n),
        ],
        out_specs=[pl.BlockSpec((1, tm, wd), lambda bi, i: (bi, i, 0)) for wd in widths],
        out_shape=[jax.ShapeDtypeStruct((b, l, wd), BF16) for wd in widths],
        compiler_params=_params(("parallel", "parallel")),
    )(x, g.reshape(1, d), shift, scale, w_all)


LRU_TC = 256
LRU_HALO = SUBLANES


def _lru_body(u_ref, uc_ref, cw_ref, cb_ref, wg_ref, bg_ref, lam_ref, *rest, need_ctx):
    if need_ctx:
        out_ref, outc_ref, upad, upadc, ucv, ucvc, hbuf = rest
    else:
        out_ref, upad, upadc, ucv, ucvc, hbuf = rest
        outc_ref = None
    l = u_ref.shape[1]
    lc = uc_ref.shape[1]
    w = u_ref.shape[2]
    tc = LRU_TC
    nt = tc // SUBLANES

    zeros_halo = jnp.zeros((LRU_HALO, w), F32)
    for pad_ref, cv_ref, src_ref, n in ((upad, ucv, u_ref, l), (upadc, ucvc, uc_ref, lc)):
        pad_ref[0:LRU_HALO, :] = zeros_halo
        pad_ref[LRU_HALO + n:2 * LRU_HALO + n, :] = zeros_halo

        def copy(i, carry, pad_ref=pad_ref, src_ref=src_ref):
            t0 = pl.multiple_of(i * tc, tc)
            pad_ref[pl.ds(LRU_HALO + t0, tc), :] = src_ref[0, pl.ds(t0, tc), :].astype(F32)
            return carry

        lax.fori_loop(0, n // tc, copy, 0)

        def conv(i, carry, pad_ref=pad_ref, cv_ref=cv_ref):
            t0 = pl.multiple_of(i * tc, tc)
            win = pad_ref[pl.ds(t0, tc + 2 * LRU_HALO), :]
            u = cb_ref[...]
            for k in range(CONV_W):
                sh = (CONV_PAD_L - k) % (tc + 2 * LRU_HALO)
                tap = win if sh == 0 else pltpu.roll(win, sh, 0)
                u = u + cw_ref[k:k + 1, :] * tap[LRU_HALO:LRU_HALO + tc]
            cv_ref[pl.ds(t0, tc), :] = u
            return carry

        lax.fori_loop(0, n // tc, conv, 0)

    row = lax.broadcasted_iota(jnp.int32, (nt, SUBLANES, w), 1)

    def chunk_scan(cv_ref, t0, d, h):
        u = cv_ref[pl.ds(t0, tc), :]
        g = jnp.dot(u.astype(BF16), wg_ref[d], preferred_element_type=F32) + bg_ref[d]
        r = jax.nn.sigmoid(g[:, :w])
        gi = jax.nn.sigmoid(g[:, w:])
        z = -lam_ref[d]
        sp = jnp.maximum(z, 0.0) + jnp.log1p(jnp.exp(-jnp.abs(z)))
        log_a = r * (-LRU_C * sp)
        a = jnp.exp2(r * (-LRU_C * math.log2(math.e) * sp))
        th = jnp.tanh(log_a)
        om = -2.0 * th / (1.0 - th)
        root = jnp.where(om > 0.0, om * lax.rsqrt(om), 0.0)
        bb = root * (gi * u)
        a = a.reshape(nt, SUBLANES, w)
        bb = bb.reshape(nt, SUBLANES, w)
        s = 1
        while s < SUBLANES:
            if d == 0:
                keep = row >= s
                shift = s
            else:
                keep = row < SUBLANES - s
                shift = SUBLANES - s
            a_s = jnp.where(keep, pltpu.roll(a, shift, 1), 1.0)
            b_s = jnp.where(keep, pltpu.roll(bb, shift, 1), 0.0)
            bb = a * b_s + bb
            a = a * a_s
            s *= 2
        order = range(nt) if d == 0 else range(nt - 1, -1, -1)
        for k in order:
            hk = a[k] * h + bb[k]
            hbuf[k * SUBLANES:(k + 1) * SUBLANES, :] = hk
            h = hk[SUBLANES - 1:SUBLANES] if d == 0 else hk[0:1]
        return h

    def run(cv_ref, n, d, h, write):
        nchunks = n // tc

        def body(i, h):
            ci = i if d == 0 else nchunks - 1 - i
            t0 = pl.multiple_of(ci * tc, tc)
            h = chunk_scan(cv_ref, t0, d, h)
            write(t0)
            return h

        return lax.fori_loop(0, nchunks, body, h, unroll=2)

    def write_accc(t0):
        upadc[pl.ds(t0, tc), :] = hbuf[...]

    def write_acc(t0):
        upad[pl.ds(t0, tc), :] = hbuf[...]

    def write_outc(t0):
        if outc_ref is not None:
            outc_ref[0, pl.ds(t0, tc), :] = (upadc[pl.ds(t0, tc), :] + hbuf[...]).astype(outc_ref.dtype)

    def write_out(t0):
        out_ref[0, pl.ds(t0, tc), :] = (upad[pl.ds(t0, tc), :] + hbuf[...]).astype(out_ref.dtype)

    h0 = jnp.zeros((1, w), F32)
    h = run(ucvc, lc, 0, h0, write_accc)
    run(ucv, l, 0, h, write_acc)
    h = run(ucvc, lc, 1, h0, write_outc)
    run(ucv, l, 1, h, write_out)


def _lru(u, uc, conv_w, conv_b, wg, bg, lam, need_ctx):
    b, l, w = u.shape
    lc = uc.shape[1]
    out_shape = [jax.ShapeDtypeStruct((b, l, w), BF16)]
    out_specs = [pl.BlockSpec((1, l, w), lambda bi: (bi, 0, 0))]
    if need_ctx:
        out_shape.append(jax.ShapeDtypeStruct((b, lc, w), BF16))
        out_specs.append(pl.BlockSpec((1, lc, w), lambda bi: (bi, 0, 0)))
    res = pl.pallas_call(
        functools.partial(_lru_body, need_ctx=need_ctx),
        grid=(b,),
        in_specs=[
            pl.BlockSpec((1, l, w), lambda bi: (bi, 0, 0)),
            pl.BlockSpec((1, lc, w), lambda bi: (bi, 0, 0)),
            _resident((CONV_W, w)),
            _resident((1, w)),
            _resident((2, w, 2 * w)),
            _resident((2, 1, 2 * w)),
            _resident((2, 1, w)),
        ],
        out_specs=out_specs,
        out_shape=out_shape,
        scratch_shapes=[
            pltpu.VMEM((l + 2 * LRU_HALO, w), F32),
            pltpu.VMEM((lc + 2 * LRU_HALO, w), F32),
            pltpu.VMEM((l, w), F32),
            pltpu.VMEM((lc, w), F32),
            pltpu.VMEM((LRU_TC, w), F32),
        ],
        compiler_params=_params(("parallel",)),
    )(u, uc, conv_w, conv_b.reshape(1, w), wg, bg, lam.reshape(2, 1, w))
    return (res[0], res[1]) if need_ctx else (res[0], None)


def _mla_prep_body(*refs, rotate, need_q):
    it = iter(refs)
    cq_ref = next(it) if need_q else None
    ckv_ref, ra_ref = next(it), next(it)
    if need_q:
        cosq_ref = next(it)
        sinq_ref = next(it) if rotate else None
    cosk_ref = next(it)
    sink_ref = next(it) if rotate else None
    if need_q:
        gq_ref, wq_ref = next(it), next(it)
    gkv_ref, wkv_ref = next(it), next(it)
    q_out = next(it) if need_q else None
    k_out, v_out = next(it), next(it)
    nh = MLA_HEADS
    tm = ckv_ref.shape[1]
    lane = lax.broadcasted_iota(jnp.int32, (tm, LANES), 1)
    half = MLA_ROPE // 2
    first_half = lane < MLA_NOPE + half

    def rotated(x, cos, sin):
        if not rotate:
            return x * cos
        partner = jnp.where(first_half, pltpu.roll(x, LANES - half, 1), pltpu.roll(x, half, 1))
        return x * cos + partner * sin

    if need_q:
        cqn = _rms(cq_ref[0].astype(F32), gq_ref[...]).astype(BF16)
        qm = jnp.dot(cqn, wq_ref[...], preferred_element_type=F32)
        cosq = cosq_ref[...]
        sinq = sinq_ref[...] if rotate else None
        for h in range(nh):
            qh = rotated(qm[:, h * LANES:(h + 1) * LANES], cosq, sinq)
            q_out[0, h] = qh.astype(q_out.dtype)

    kvn = _rms(ckv_ref[0].astype(F32), gkv_ref[...]).astype(BF16)
    kv = jnp.dot(kvn, wkv_ref[...], preferred_element_type=F32)
    rope = rotated(ra_ref[0].astype(F32), cosk_ref[...], sink_ref[...] if rotate else None)
    for h in range(nh):
        k_out[0, h] = (kv[:, h * LANES:(h + 1) * LANES] + rope).astype(k_out.dtype)
        vh = kv[:, (nh + h) * LANES:(nh + h + 1) * LANES]
        v_out[0, h] = jnp.where(lane == _sum_lane(h), 1.0, vh).astype(v_out.dtype)


def _sum_lane(h):
    return MLA_V if h % 2 == 0 else 0


def _mla_prep(cq, ckv, ra, cosq, sinq, cosk, sink, gq, wq, gkv, wkv, rotate, need_q, tm):
    b, l, _ = ckv.shape
    nh = MLA_HEADS
    row = lambda wd: pl.BlockSpec((1, tm, wd), lambda bi, i: (bi, i, 0))
    tab = pl.BlockSpec((tm, LANES), lambda bi, i: (i, 0))
    args, specs = [], []

    def add(a, s):
        args.append(a)
        specs.append(s)

    if need_q:
        add(cq, row(Q_LORA))
    add(ckv, row(KV_LORA))
    add(ra, row(LANES))
    if need_q:
        add(cosq, tab)
        if rotate:
            add(sinq, tab)
    add(cosk, tab)
    if rotate:
        add(sink, tab)
    if need_q:
        add(gq.reshape(1, Q_LORA), _resident((1, Q_LORA)))
        add(wq, _resident(wq.shape))
    add(gkv.reshape(1, KV_LORA), _resident((1, KV_LORA)))
    add(wkv, _resident(wkv.shape))

    head = pl.BlockSpec((1, nh, tm, LANES), lambda bi, i: (bi, 0, i, 0))
    hshape = jax.ShapeDtypeStruct((b, nh, l, LANES), ATTN_DT)
    n_out = 3 if need_q else 2
    res = pl.pallas_call(
        functools.partial(_mla_prep_body, rotate=rotate, need_q=need_q),
        grid=(b, l // tm),
        in_specs=specs,
        out_specs=[head] * n_out,
        out_shape=[hshape] * n_out,
        compiler_params=_params(("parallel", "parallel")),
    )(*args)
    return res if need_q else (None, res[0], res[1])


ATTN_PAIRS_PER_STEP = 4
ATTN_DT = jnp.float8_e4m3fn
ATTN_P_SHIFT = 8.0


def _attn_body(q_ref, *refs, n_kv, tq):
    kv_refs = refs[:2 * n_kv]
    o_ref = refs[2 * n_kv]
    s_bufs = refs[2 * n_kv + 1:2 * n_kv + 3]
    p_bufs = refs[2 * n_kv + 3:2 * n_kv + 5]
    m_bufs = refs[2 * n_kv + 5:2 * n_kv + 7]
    npair = ATTN_PAIRS_PER_STEP
    n_items = (q_ref.shape[2] // tq) * npair
    lks = [kv_refs[2 * j].shape[2] for j in range(n_kv)]
    offs = [sum(lks[:j]) for j in range(n_kv)]
    lane = lax.broadcasted_iota(jnp.int32, (tq, LANES), 1)
    nt_dims = (((1,), (1,)), ((), ()))

    s_bufs[1][...] = jnp.zeros(s_bufs[1].shape, s_bufs[1].dtype)
    p_bufs[0][...] = jnp.zeros(p_bufs[0].shape, p_bufs[0].dtype)
    m_bufs[1][...] = jnp.zeros(m_bufs[1].shape, m_bufs[1].dtype)

    def item(t):
        tile = lax.shift_right_logical(t, npair.bit_length() - 1)
        pair = jnp.bitwise_and(t, npair - 1)
        return pl.multiple_of(tile * tq, tq), pair

    def stages(t, slot):
        s_a, s_b = s_bufs[slot], s_bufs[1 - slot]
        p_b, p_c = p_bufs[1 - slot], p_bufs[slot]
        m_a, m_b = m_bufs[slot], m_bufs[1 - slot]

        row_a, pair_a = item(jnp.minimum(t, n_items - 1))
        for e in range(2):
            h = pair_a * 2 + e
            q = q_ref[0, h, pl.ds(row_a, tq), :]
            bm = None
            for j in range(n_kv):
                sj = lax.dot_general(q, kv_refs[2 * j][0, h], nt_dims,
                                     preferred_element_type=F32).astype(s_a.dtype)
                s_a[e, :, offs[j]:offs[j] + lks[j]] = sj
                for c0 in range(0, lks[j], LANES):
                    blk = sj[:, c0:c0 + LANES]
                    bm = blk if bm is None else jnp.maximum(bm, blk)
            m_a[e] = bm

        for e in range(2):
            s = s_b[e]
            m = m_b[e].astype(F32).max(axis=-1, keepdims=True)
            shift = (m - ATTN_P_SHIFT).astype(s.dtype)
            p_b[e] = jnp.exp2(s - shift).astype(p_b.dtype)

        row_c, pair_c = item(jnp.maximum(t - 2, 0))
        halves = []
        for e in range(2):
            h = pair_c * 2 + e
            o = None
            for j in range(n_kv):
                oj = jnp.dot(p_c[e, :, offs[j]:offs[j] + lks[j]], kv_refs[2 * j + 1][0, h],
                             preferred_element_type=F32)
                o = oj if o is None else o + oj
            one = _sum_lane(e)
            halves.append(o * (1.0 / o[:, one:one + 1]))
        blk = jnp.where(lane < MLA_V, halves[0], halves[1])
        o_ref[0, pair_c, pl.ds(row_c, tq), :] = blk.astype(o_ref.dtype)

    def body(i, carry):
        stages(2 * i, 0)
        stages(2 * i + 1, 1)
        return carry

    lax.fori_loop(0, (n_items + 2) // 2, body, 0)


def _attention(q, kvs, tq):
    b, nh, lq, _ = q.shape
    hs = 2 * ATTN_PAIRS_PER_STEP
    once = lambda n: pl.BlockSpec((1, hs, n, LANES), lambda bi, j: (bi, j, 0, 0))
    specs = [once(lq)]
    args = [q]
    lkt = 0
    for k, v in kvs:
        lk = k.shape[2]
        lkt += lk
        specs += [once(lk), once(lk)]
        args += [k, v]
    return pl.pallas_call(
        functools.partial(_attn_body, n_kv=len(kvs), tq=tq),
        grid=(b, nh // hs),
        in_specs=specs,
        out_specs=pl.BlockSpec((1, ATTN_PAIRS_PER_STEP, lq, LANES), lambda bi, j: (bi, j, 0, 0)),
        out_shape=jax.ShapeDtypeStruct((b, nh // 2, lq, LANES), BF16),
        scratch_shapes=[
            pltpu.VMEM((2, tq, lkt), BF16),
            pltpu.VMEM((2, tq, lkt), BF16),
            pltpu.VMEM((2, tq, lkt), ATTN_DT),
            pltpu.VMEM((2, tq, lkt), ATTN_DT),
            pltpu.VMEM((2, tq, LANES), BF16),
            pltpu.VMEM((2, tq, LANES), BF16),
        ],
        compiler_params=_params(("parallel", "parallel")),
    )(*args)


def _ret_body(*refs, need_ctx):
    it = iter(refs)
    q_ref, k_ref, v_ref = next(it), next(it), next(it)
    qc_ref = next(it) if need_ctx else None
    kc_ref, vc_ref = next(it), next(it)
    cos_ref, sin_ref = next(it), next(it)
    msum_ref, qdec_ref, kdec_ref, cd_ref = next(it), next(it), next(it), next(it)
    o_ref = next(it)
    oc_ref = next(it) if need_ctx else None
    qa, ka, va, u_s, r_s = (next(it) for _ in range(5))

    c = RET_CHUNK
    l = q_ref.shape[1]
    lc = kc_ref.shape[1]
    ncc = lc // c
    nt = (lc + l) // c
    kscale = RET_DK ** -0.5
    lane = lax.broadcasted_iota(jnp.int32, (c, LANES), 1)
    first_half = (lane % RET_DK) < (RET_DK // 2)

    def rotated(src_ref, t0, g0):
        x = src_ref[0, pl.ds(t0, c), :].astype(F32)
        sw = jnp.where(first_half, pltpu.roll(x, LANES - RET_DK // 2, 1),
                       pltpu.roll(x, RET_DK // 2, 1))
        return x * cos_ref[pl.ds(g0, c), :] + sw * sin_ref[pl.ds(g0, c), :]

    def stage(qsrc, ksrc, vsrc, n, base):
        def body(i, carry):
            t0 = pl.multiple_of(i * c, c)
            g0 = pl.multiple_of(base + i * c, c)
            if qsrc is not None:
                qa[pl.ds(g0, c), :] = rotated(qsrc, t0, g0)
            ka[pl.ds(g0, c), :] = rotated(ksrc, t0, g0) * kscale
            va[pl.ds(g0, c), :] = vsrc[0, pl.ds(t0, c), :]
            return carry

        lax.fori_loop(0, n // c, body, 0, unroll=2)

    stage(qc_ref, kc_ref, vc_ref, lc, 0)
    stage(q_ref, k_ref, v_ref, l, lc)

    tn_dims = (((0,), (0,)), ((), ()))
    nt_dims = (((1,), (1,)), ((), ()))

    def incr(g, carry):
        g0 = pl.multiple_of(g * c, c)
        kf = ka[pl.ds(g0, c), :]
        for e in range(2):
            v = va[pl.ds(g0, c), e * RET_DV:(e + 1) * RET_DV]
            kd = jnp.concatenate([kf * kdec_ref[0, e, 0], kf * kdec_ref[0, e, 1]], axis=1)
            u_s[g, e] = lax.dot_general(kd.astype(BF16), v, tn_dims, preferred_element_type=F32)
        return carry

    lax.fori_loop(0, nt, incr, 0, unroll=4)

    zero = tuple(jnp.zeros((LANES, RET_DV), F32) for _ in range(4))

    def recur(t, st):
        gs = (t, jnp.where(t < ncc, ncc - 1 - t, nt - 1 + ncc - t))
        new = []
        for d in range(2):
            rows = slice(d * LANES, (d + 1) * LANES)
            for e in range(2):
                r = st[2 * d + e]
                r_s[gs[d], e, rows, :] = r.astype(r_s.dtype)
                new.append(cd_ref[0, e, d] * r + u_s[gs[d], e, rows, :])
        return tuple(new)

    lax.fori_loop(0, nt, recur, zero, unroll=2)

    def emit(dst_ref, gbase):
        def body(i, carry):
            t0 = pl.multiple_of(i * c, c)
            g = gbase + i
            g0 = pl.multiple_of(g * c, c)
            qf = qa[pl.ds(g0, c), :]
            kb = ka[pl.ds(g0, c), :].astype(BF16)
            parts = []
            for e in range(2):
                v = va[pl.ds(g0, c), e * RET_DV:(e + 1) * RET_DV]
                qm = jnp.where((lane // RET_DK) == e, qf, 0.0).astype(BF16)
                s = lax.dot_general(qm, kb, nt_dims, preferred_element_type=F32)
                o = jnp.dot((s * msum_ref[0, e]).astype(BF16), v, preferred_element_type=F32)
                qd = jnp.concatenate([qf * qdec_ref[0, e, 0], qf * qdec_ref[0, e, 1]], axis=1)
                o = o + jnp.dot(qd.astype(BF16), r_s[g, e], preferred_element_type=F32)
                mu = jnp.mean(o, axis=-1, keepdims=True)
                oc = o - mu
                var = jnp.mean(oc * oc, axis=-1, keepdims=True)
                parts.append(oc * lax.rsqrt(var + EPS))
            dst_ref[0, pl.ds(t0, c), :] = jnp.concatenate(parts, axis=1).astype(dst_ref.dtype)
            return carry

        return body

    if need_ctx:
        lax.fori_loop(0, ncc, emit(oc_ref, 0), 0, unroll=2)
    lax.fori_loop(0, nt - ncc, emit(o_ref, ncc), 0, unroll=8)


def _retention(q, k, v, qc, kc, vc, tabs, dec, need_ctx):
    b, l, _ = q.shape
    lc = kc.shape[1]
    lt = lc + l
    c = RET_CHUNK
    nt = lt // c
    npair = RET_HEADS // 2
    seq = lambda n, wd: pl.BlockSpec((1, n, wd), lambda bi, p: (bi, 0, p))
    args = [q, k, v]
    specs = [seq(l, LANES), seq(l, LANES), seq(l, 2 * RET_DV)]
    if need_ctx:
        args.append(qc)
        specs.append(seq(lc, LANES))
    args += [kc, vc]
    specs += [seq(lc, LANES), seq(lc, 2 * RET_DV)]
    for t in tabs:
        args.append(t)
        specs.append(_resident((lt, LANES)))
    msum, qdec, kdec, cd = dec
    args += [msum, qdec, kdec, cd]
    specs += [
        pl.BlockSpec((1, 2, c, c), lambda bi, p: (p, 0, 0, 0)),
        pl.BlockSpec((1, 2, 2, c, LANES), lambda bi, p: (p, 0, 0, 0, 0)),
        pl.BlockSpec((1, 2, 2, c, LANES), lambda bi, p: (p, 0, 0, 0, 0)),
        pl.BlockSpec((1, 2, 2, 1, RET_DV), lambda bi, p: (p, 0, 0, 0, 0)),
    ]
    out_shape = [jax.ShapeDtypeStruct((b, l, RET_HEADS * RET_DV), BF16)]
    out_specs = [seq(l, 2 * RET_DV)]
    if need_ctx:
        out_shape.append(jax.ShapeDtypeStruct((b, lc, RET_HEADS * RET_DV), BF16))
        out_specs.append(seq(lc, 2 * RET_DV))
    res = pl.pallas_call(
        functools.partial(_ret_body, need_ctx=need_ctx),
        grid=(b, npair),
        in_specs=specs,
        out_specs=out_specs,
        out_shape=out_shape,
        scratch_shapes=[
            pltpu.VMEM((lt, LANES), F32),
            pltpu.VMEM((lt, LANES), F32),
            pltpu.VMEM((lt, 2 * RET_DV), BF16),
            pltpu.VMEM((nt, 2, 2 * LANES, RET_DV), F32),
            pltpu.VMEM((nt, 2, 2 * LANES, RET_DV), BF16),
        ],
        compiler_params=_params(("parallel", "parallel")),
    )(*args)
    return (res[0], res[1]) if need_ctx else (res[0], None)


FFN_CHUNKS = ((0, 6 * MXU_TILE), (6 * MXU_TILE, D_FF))
assert all(lo % MXU_TILE == 0 and hi % MXU_TILE == 0 for lo, hi in FFN_CHUNKS)


def _mix_ffn_body(x_ref, rec_ref, gl_ref, yb_ref, rn_ref, gr_ref, gm_ref, mod_ref,
                  woa_ref, wob_ref, woc_ref, wout_ref, g_ref, w1_ref, w3_ref, w2_ref, *rest, final):
    if final:
        gf_ref, o_ref = rest
    else:
        (o_ref,) = rest
    d = x_ref.shape[2]
    ga_a, sh_f, sc_f, ga_f = (mod_ref[0, j:j + 1, :] for j in range(4))

    ya = (jax.nn.gelu(gl_ref[0].astype(F32)) * rec_ref[0].astype(F32)).astype(BF16)
    yc = (jax.nn.silu(gr_ref[0].astype(F32)) * rn_ref[0].astype(F32)).astype(BF16)
    yb = jnp.concatenate([yb_ref[0, p] for p in range(yb_ref.shape[1])], axis=1)
    gm = gm_ref[0]
    m = jax.nn.sigmoid(gm[:, :d].astype(F32)) * jnp.dot(ya, woa_ref[...], preferred_element_type=F32)
    m = m + jax.nn.sigmoid(gm[:, d:2 * d].astype(F32)) * jnp.dot(
        yb, wob_ref[...], preferred_element_type=F32)
    m = m + jax.nn.sigmoid(gm[:, 2 * d:].astype(F32)) * jnp.dot(
        yc, woc_ref[...], preferred_element_type=F32)
    x = x_ref[0] + ga_a * jnp.dot(m.astype(BF16), wout_ref[...], preferred_element_type=F32)

    hb = (_rms(x, g_ref[...]) * (1.0 + sc_f) + sh_f).astype(BF16)
    acc = None
    for c0, c1 in FFN_CHUNKS:
        a = jnp.dot(hb, w1_ref[:, c0:c1], preferred_element_type=F32)
        bgate = jnp.dot(hb, w3_ref[:, c0:c1], preferred_element_type=F32)
        t = (jax.nn.silu(a) * bgate).astype(BF16)
        part = jnp.dot(t, w2_ref[c0:c1, :], preferred_element_type=F32)
        acc = part if acc is None else acc + part
    y = x + ga_f * acc
    if final:
        y = _rms(y, gf_ref[...])
    o_ref[0] = y


def _mix_ffn(x, rec, gl, yb, rn, gr, gm, mod4, woa, wob, woc, wout, g, w1, w3, w2, li, g_final, tm):
    b, l, d = x.shape
    final = g_final is not None
    row = lambda wd: pl.BlockSpec((1, tm, wd), lambda bi, i: (bi, i, 0))
    mod_map = (lambda bi, i: (bi, 0, 0)) if mod4.shape[0] > 1 else (lambda bi, i: (0, 0, 0))
    args = [x, rec, gl, yb, rn, gr, gm, mod4, woa, wob, woc, wout, g.reshape(1, d), w1, w3, w2]
    specs = [
        row(d), row(LRU_W), row(LRU_W),
        pl.BlockSpec((1, MLA_HEADS // 2, tm, LANES), lambda bi, i: (bi, 0, i, 0)),
        row(RET_HEADS * RET_DV), row(RET_HEADS * RET_DV), row(3 * d),
        pl.BlockSpec((1, 4, d), mod_map),
        _layer_spec(woa, li), _layer_spec(wob, li), _layer_spec(woc, li), _layer_spec(wout, li),
        _resident((1, d)),
        _layer_spec(w1, li), _layer_spec(w3, li), _layer_spec(w2, li),
    ]
    if final:
        args.append(g_final.reshape(1, d))
        specs.append(_resident((1, d)))
    return pl.pallas_call(
        functools.partial(_mix_ffn_body, final=final),
        grid=(b, l // tm),
        in_specs=specs,
        out_specs=row(d),
        out_shape=jax.ShapeDtypeStruct((b, l, d), F32),
        compiler_params=_params(("parallel", "parallel")),
    )(*args)


def _arrange_w_in(w_in):
    k0 = LRU_W + KV_LORA
    z = lambda n: jnp.zeros(w_in.shape[:-1] + (n,), w_in.dtype)
    cols = [w_in[..., :k0], z(MLA_NOPE), w_in[..., k0:k0 + MLA_ROPE], z(LANES - MLA_NOPE - MLA_ROPE),
            w_in[..., k0 + MLA_ROPE:]]
    return jnp.concatenate(cols, axis=-1).astype(BF16)


def _arrange_mla(w_uq, w_ukv):
    nh = MLA_HEADS
    wq = w_uq.reshape(Q_LORA, nh, MLA_NOPE + MLA_ROPE)
    padq = jnp.zeros((Q_LORA, nh, LANES - MLA_NOPE - MLA_ROPE), w_uq.dtype)
    wq_arr = jnp.concatenate([wq, padq], axis=-1).reshape(Q_LORA, nh * LANES)
    wkv = w_ukv.reshape(KV_LORA, nh, MLA_NOPE + MLA_V)
    zk = jnp.zeros((KV_LORA, nh, LANES - MLA_NOPE), w_ukv.dtype)
    wk_arr = jnp.concatenate([wkv[..., :MLA_NOPE], zk], axis=-1).reshape(KV_LORA, nh * LANES)
    wv = wkv[..., MLA_NOPE:]
    zv = jnp.zeros_like(wv)
    even = (jnp.arange(nh) % 2 == 0)[None, :, None]
    wv_arr = jnp.concatenate([jnp.where(even, wv, zv), jnp.where(even, zv, wv)], axis=-1)
    wv_arr = wv_arr.reshape(KV_LORA, nh * LANES)
    return wq_arr.astype(BF16), jnp.concatenate([wk_arr, wv_arr], axis=1).astype(BF16)


def _arrange_lru(wa, ba, wx, bx):
    def dense(wblk):
        eye = jnp.eye(LRU_BLOCKS, dtype=wblk.dtype)
        full = jnp.einsum("nde,nm->ndme", wblk, eye)
        return full.reshape(LRU_W, LRU_W)

    wg = jnp.stack([jnp.concatenate([dense(wa[d]), dense(wx[d])], axis=1) for d in range(2)])
    bg = jnp.stack([jnp.concatenate([ba[d], bx[d]])[None, :] for d in range(2)])
    return wg.astype(BF16), bg


def _mla_tables(n_tokens, rotate):
    scale = (MLA_NOPE + MLA_ROPE) ** -0.5 * math.log2(math.e)
    pad = LANES - MLA_NOPE - MLA_ROPE
    ones = jnp.ones((n_tokens, MLA_NOPE), F32)
    zeros = jnp.zeros((n_tokens, MLA_NOPE), F32)
    zpad = jnp.zeros((n_tokens, pad), F32)
    if rotate:
        rows = n_tokens // GRID_W
        rowi = jnp.repeat(jnp.arange(rows, dtype=F32), GRID_W)
        coli = jnp.tile(jnp.arange(GRID_W, dtype=F32), rows)
        n_freq = MLA_ROPE // 4
        inv = jnp.power(ROPE_BASE, -jnp.arange(n_freq, dtype=F32) / n_freq)
        ang = jnp.concatenate([rowi[:, None] * inv, coli[:, None] * inv], axis=-1)
        cos, sin = jnp.cos(ang), jnp.sin(ang)
    else:
        cos = jnp.ones((n_tokens, MLA_ROPE // 2), F32)
        sin = jnp.zeros((n_tokens, MLA_ROPE // 2), F32)
    cosq = jnp.concatenate([ones, cos, cos, zpad], axis=1) * scale
    sinq = jnp.concatenate([zeros, -sin, sin, zpad], axis=1) * scale
    cosk = jnp.concatenate([zeros, cos, cos, zpad], axis=1)
    sink = jnp.concatenate([zeros, -sin, sin, zpad], axis=1)
    return cosq, sinq, cosk, sink


def _ret_tables(n):
    theta = 1.0 / jnp.power(10000.0, jnp.linspace(0.0, 1.0, RET_DK // 2, dtype=F32))
    pos = jnp.arange(n, dtype=F32)
    ang = pos[:, None] * theta
    cos, sin = jnp.cos(ang), jnp.sin(ang)
    reps = LANES // RET_DK
    cos_t = jnp.tile(jnp.concatenate([cos, cos], axis=1), (1, reps))
    sin_t = jnp.tile(jnp.concatenate([-sin, sin], axis=1), (1, reps))
    return cos_t, sin_t


def _ret_decay_tables():
    c = RET_CHUNK
    h = jnp.arange(RET_HEADS, dtype=F32)
    lgs = jnp.stack([jnp.log1p(-jnp.exp2(-5.0 - h)), jnp.log1p(-jnp.exp2(-5.5 - h))], axis=1)
    pos = jnp.arange(c, dtype=F32)
    diff = pos[:, None] - pos[None, :]
    lg = lgs[:, :, None, None]
    inner = jnp.where(diff >= 0, jnp.exp(lg * jnp.maximum(diff, 0.0)), 0.0)
    msum = inner[:, 0] + jnp.swapaxes(inner[:, 1], -1, -2)
    lg1 = lgs[:, :, None]
    qd_f = jnp.exp(lg1 * (pos + 1.0))
    kd_f = jnp.exp(lg1 * (c - 1.0 - pos))
    qd = jnp.stack([qd_f[:, 0], qd_f[:, 1, ::-1]], axis=1)
    kd = jnp.stack([kd_f[:, 0], kd_f[:, 1, ::-1]], axis=1)
    lane_head = (jnp.arange(LANES) // RET_DK)
    e_of_h = jnp.arange(RET_HEADS) % 2
    mask = (lane_head[None, :] == e_of_h[:, None]).astype(F32)
    qdec = qd[..., None] * mask[:, None, None, :]
    kdec = kd[..., None] * mask[:, None, None, :]
    cd = jnp.broadcast_to(jnp.exp(lgs * c)[:, :, None, None], (RET_HEADS, 2, 1, RET_DV))
    pair = lambda t: t.reshape(RET_HEADS // 2, 2, *t.shape[1:])
    return pair(msum), pair(qdec), pair(kdec), pair(cd)


def _row_tile(n, pref):
    t = min(pref, n)
    while n % t:
        t //= 2
    return t


def kernel(x, c, ctx, c_ctx, w_mod, b_mod, g_mix, g_ffn, w_in, conv_w, conv_b, lru_wa, lru_ba,
           lru_wx, lru_bx, lru_lam, g_q, w_uq, g_kv, w_ukv, w_oa, w_ob, w_oc, w_out,
           w_ff1, w_ff3, w_ff2, g_final):
    b, l, d = x.shape
    lc = ctx.shape[1]
    depth = w_mod.shape[0]
    assert d == D_MODEL and l % LRU_TC == 0 and lc % LRU_TC == 0 and l % GRID_W == 0

    tm = _row_tile(l, 512)
    tmc = _row_tile(lc, 256)
    tq = _row_tile(l, 256)
    tqc = _row_tile(lc, 256)

    lat_tabs = _mla_tables(l, True)
    ctx_tabs = _mla_tables(lc, False)
    ret_tabs = _ret_tables(lc + l)
    dec = _ret_decay_tables()

    rows = -(-(b + 1) // SUBLANES) * SUBLANES
    cvec = jnp.zeros((rows, d), F32).at[:b].set(c).at[b].set(c_ctx)
    mod_all = _modulation(cvec, w_mod, b_mod)

    w_arr = _arrange_w_in(w_in)
    woa, wob, woc, wout = (w.astype(BF16) for w in (w_oa, w_ob, w_oc, w_out))
    w1, w3, w2 = (w.astype(BF16) for w in (w_ff1, w_ff3, w_ff2))

    xc = ctx
    for li in range(depth):
        last = li == depth - 1
        mod = mod_all[li, :b].reshape(b, 1, 6, d)
        sh_a, sc_a = mod[:, :, 0], mod[:, :, 1]
        mc = mod_all[li, b].reshape(1, 1, 6, d)
        mcs = [mc[:, :, j] for j in range(6)]

        wq_arr, wkv_arr = _arrange_mla(w_uq[li], w_ukv[li])
        wg, bg = _arrange_lru(lru_wa[li], lru_ba[li], lru_wx[li], lru_bx[li])

        z = _inproj(x, g_mix[li], sh_a, sc_a, w_arr, li, KV_GROUPS + Q_GROUPS, tm)
        u, ckv, ra, rk, rv, cq, rq, g_lru, g_ret, gm = z
        if last:
            zc = _inproj(xc, g_mix[li], mcs[0], mcs[1], w_arr, li, KV_GROUPS, tmc)
            uc, ckvc, rac, rkc, rvc = zc
            cqc = rqc = None
        else:
            zc = _inproj(xc, g_mix[li], mcs[0], mcs[1], w_arr, li, KV_GROUPS + Q_GROUPS, tmc)
            uc, ckvc, rac, rkc, rvc, cqc, rqc, g_lru_c, g_ret_c, gm_c = zc
        need_ctx = not last

        rec, rec_c = _lru(u, uc, conv_w[li], conv_b[li], wg, bg, lru_lam[li], need_ctx)

        q, k, v = _mla_prep(cq, ckv, ra, *lat_tabs, g_q[li], wq_arr, g_kv[li], wkv_arr,
                            True, True, _row_tile(l, 1024))
        qc, kc, vc = _mla_prep(cqc, ckvc, rac, *ctx_tabs, g_q[li], wq_arr, g_kv[li], wkv_arr,
                               False, need_ctx, tmc)
        yb = _attention(q, [(kc, vc), (k, v)], tq)

        rn, rn_c = _retention(rq, rk, rv, rqc, rkc, rvc, ret_tabs, dec, need_ctx)

        x = _mix_ffn(x, rec, g_lru, yb, rn, g_ret, gm, mod[:, 0, 2:], woa, wob, woc, wout,
                     g_ffn[li], w1, w3, w2, li, g_final if last else None, tm)
        if need_ctx:
            yb_c = _attention(qc, [(kc, vc)], tqc)
            xc = _mix_ffn(xc, rec_c, g_lru_c, yb_c, rn_c, g_ret_c, gm_c, mc[:, 0, 2:], woa, wob, woc, wout,
                          g_ffn[li], w1, w3, w2, li, None, tmc)
    return x
```

```python
import functools
import math

import jax
import jax.numpy as jnp
from jax import lax
from jax.experimental import pallas as pl
from jax.experimental.pallas import tpu as pltpu

F32 = jnp.float32
BF16 = jnp.bfloat16

D_MODEL = 1024
EPS = 1e-6
GRID_W = 64
LRU_W = 512
LRU_BLOCKS = 8
LRU_BLOCK = LRU_W // LRU_BLOCKS
CONV_W = 4
CONV_PAD_L = 2
LRU_C = 8.0
MLA_HEADS = 8
MLA_NOPE = 64
MLA_ROPE = 32
MLA_V = 64
Q_LORA = 384
KV_LORA = 256
ROPE_BASE = 10000.0
RET_HEADS = 4
RET_DK = 64
RET_DV = 128
RET_CHUNK = 256
D_FF = 2816

LANES = 128
SUBLANES = 8
MXU_TILE = 256
MLA_PAIR_DIST = LANES // 2
VMEM_LIMIT = 56 * 1024 * 1024

KV_GROUPS = (LRU_W, KV_LORA, LANES, RET_HEADS * RET_DK, RET_HEADS * RET_DV)
Q_GROUPS = (Q_LORA, RET_HEADS * RET_DK, LRU_W, RET_HEADS * RET_DV, 3 * D_MODEL)
N_KV_ARR = sum(KV_GROUPS)


def _params(sem):
    return pltpu.CompilerParams(dimension_semantics=sem, vmem_limit_bytes=VMEM_LIMIT)


def _resident(shape):
    nd = len(shape)
    return pl.BlockSpec(shape, lambda *_: (0,) * nd, pipeline_mode=pl.Buffered(1))


def _layer_spec(stacked, li, cols=None):
    shape = stacked.shape[1:] if cols is None else stacked.shape[1:-1] + (cols,)
    nd = len(shape)
    return pl.BlockSpec((None,) + shape, lambda *_: (li,) + (0,) * nd, pipeline_mode=pl.Buffered(1))


def _rms(x, g):
    y = x * lax.rsqrt(jnp.mean(x * x, axis=-1, keepdims=True) + EPS)
    return y * g


def _mod_body(c_ref, w_ref, b_ref, o_ref):
    cv = c_ref[...]
    act = cv * jax.nn.sigmoid(cv)
    o_ref[0] = jnp.dot(act, w_ref[0], precision=lax.Precision.HIGHEST,
                       preferred_element_type=F32) + b_ref[0]


def _modulation(cvec, w_mod, b_mod):
    depth, d, n = w_mod.shape
    rows = cvec.shape[0]
    tn = n // 4
    return pl.pallas_call(
        _mod_body,
        grid=(depth, n // tn),
        in_specs=[
            pl.BlockSpec((rows, d), lambda l, j: (0, 0)),
            pl.BlockSpec((1, d, tn), lambda l, j: (l, 0, j)),
            pl.BlockSpec((1, 1, tn), lambda l, j: (l, 0, j)),
        ],
        out_specs=pl.BlockSpec((1, rows, tn), lambda l, j: (l, 0, j)),
        out_shape=jax.ShapeDtypeStruct((depth, rows, n), F32),
        compiler_params=_params(("arbitrary", "arbitrary")),
    )(cvec, w_mod, b_mod.reshape(depth, 1, n))


def _inproj_body(x_ref, g_ref, sh_ref, sc_ref, w_ref, *out_refs, widths):
    x = x_ref[0]
    h = _rms(x, g_ref[...]) * (1.0 + sc_ref[0]) + sh_ref[0]
    hb = h.astype(BF16)
    start, off = 0, 0
    for i, wd in enumerate(widths):
        off += wd
        if off % MXU_TILE == 0 or i == len(widths) - 1:
            base = sum(widths[:start])
            z = jnp.dot(hb, w_ref[:, base:off], preferred_element_type=F32)
            col = 0
            for j in range(start, i + 1):
                out_refs[j][0] = z[:, col:col + widths[j]].astype(out_refs[j].dtype)
                col += widths[j]
            start = i + 1


def _inproj(x, g, shift, scale, w_all, li, widths, tm):
    b, l, d = x.shape
    n = sum(widths)
    mod_map = (lambda bi, i: (bi, 0, 0)) if shift.shape[0] > 1 else (lambda bi, i: (0, 0, 0))
    return pl.pallas_call(
        functools.partial(_inproj_body, widths=widths),
        grid=(b, l // tm),
        in_specs=[
            pl.BlockSpec((1, tm, d), lambda bi, i: (bi, i, 0)),
            _resident((1, d)),
            pl.BlockSpec((1, 1, d), mod_map),
            pl.BlockSpec((1, 1, d), mod_map),
            _layer_spec(w_all, li, n),
        ],
        out_specs=[pl.BlockSpec((1, tm, wd), lambda bi, i: (bi, i, 0)) for wd in widths],
        out_shape=[jax.ShapeDtypeStruct((b, l, wd), BF16) for wd in widths],
        compiler_params=_params(("parallel", "parallel")),
    )(x, g.reshape(1, d), shift, scale, w_all)


LRU_TC = 256
LRU_HALO = SUBLANES


def _lru_body(u_ref, uc_ref, cw_ref, cb_ref, wg_ref, bg_ref, lam_ref, *rest, need_ctx):
    if need_ctx:
        out_ref, outc_ref, upad, upadc, ucv, ucvc, hbuf = rest
    else:
        out_ref, upad, upadc, ucv, ucvc, hbuf = rest
        outc_ref = None
    l = u_ref.shape[1]
    lc = uc_ref.shape[1]
    w = u_ref.shape[2]
    tc = LRU_TC
    nt = tc // SUBLANES

    zeros_halo = jnp.zeros((LRU_HALO, w), F32)
    for pad_ref, cv_ref, src_ref, n in ((upad, ucv, u_ref, l), (upadc, ucvc, uc_ref, lc)):
        pad_ref[0:LRU_HALO, :] = zeros_halo
        pad_ref[LRU_HALO + n:2 * LRU_HALO + n, :] = zeros_halo

        def copy(i, carry, pad_ref=pad_ref, src_ref=src_ref):
            t0 = pl.multiple_of(i * tc, tc)
            pad_ref[pl.ds(LRU_HALO + t0, tc), :] = src_ref[0, pl.ds(t0, tc), :].astype(F32)
            return carry

        lax.fori_loop(0, n // tc, copy, 0)

        def conv(i, carry, pad_ref=pad_ref, cv_ref=cv_ref):
            t0 = pl.multiple_of(i * tc, tc)
            win = pad_ref[pl.ds(t0, tc + 2 * LRU_HALO), :]
            u = cb_ref[...]
            for k in range(CONV_W):
                sh = (CONV_PAD_L - k) % (tc + 2 * LRU_HALO)
                tap = win if sh == 0 else pltpu.roll(win, sh, 0)
                u = u + cw_ref[k:k + 1, :] * tap[LRU_HALO:LRU_HALO + tc]
            cv_ref[pl.ds(t0, tc), :] = u
            return carry

        lax.fori_loop(0, n // tc, conv, 0)

    row = lax.broadcasted_iota(jnp.int32, (nt, SUBLANES, w), 1)

    def chunk_scan(cv_ref, t0, d, h):
        u = cv_ref[pl.ds(t0, tc), :]
        g = jnp.dot(u.astype(BF16), wg_ref[d], preferred_element_type=F32) + bg_ref[d]
        r = jax.nn.sigmoid(g[:, :w])
        gi = jax.nn.sigmoid(g[:, w:])
        z = -lam_ref[d]
        sp = jnp.maximum(z, 0.0) + jnp.log1p(jnp.exp(-jnp.abs(z)))
        log_a = r * (-LRU_C * sp)
        a = jnp.exp2(r * (-LRU_C * math.log2(math.e) * sp))
        th = jnp.tanh(log_a)
        om = -2.0 * th / (1.0 - th)
        root = jnp.where(om > 0.0, om * lax.rsqrt(om), 0.0)
        bb = root * (gi * u)
        a = a.reshape(nt, SUBLANES, w)
        bb = bb.reshape(nt, SUBLANES, w)
        s = 1
        while s < SUBLANES:
            if d == 0:
                keep = row >= s
                shift = s
            else:
                keep = row < SUBLANES - s
                shift = SUBLANES - s
            a_s = jnp.where(keep, pltpu.roll(a, shift, 1), 1.0)
            b_s = jnp.where(keep, pltpu.roll(bb, shift, 1), 0.0)
            bb = a * b_s + bb
            a = a * a_s
            s *= 2
        order = range(nt) if d == 0 else range(nt - 1, -1, -1)
        for k in order:
            hk = a[k] * h + bb[k]
            hbuf[k * SUBLANES:(k + 1) * SUBLANES, :] = hk
            h = hk[SUBLANES - 1:SUBLANES] if d == 0 else hk[0:1]
        return h

    def run(cv_ref, n, d, h, write):
        nchunks = n // tc

        def body(i, h):
            ci = i if d == 0 else nchunks - 1 - i
            t0 = pl.multiple_of(ci * tc, tc)
            h = chunk_scan(cv_ref, t0, d, h)
            write(t0)
            return h

        return lax.fori_loop(0, nchunks, body, h, unroll=2)

    def write_accc(t0):
        upadc[pl.ds(t0, tc), :] = hbuf[...]

    def write_acc(t0):
        upad[pl.ds(t0, tc), :] = hbuf[...]

    def write_outc(t0):
        if outc_ref is not None:
            outc_ref[0, pl.ds(t0, tc), :] = (upadc[pl.ds(t0, tc), :] + hbuf[...]).astype(outc_ref.dtype)

    def write_out(t0):
        out_ref[0, pl.ds(t0, tc), :] = (upad[pl.ds(t0, tc), :] + hbuf[...]).astype(out_ref.dtype)

    h0 = jnp.zeros((1, w), F32)
    h = run(ucvc, lc, 0, h0, write_accc)
    run(ucv, l, 0, h, write_acc)
    h = run(ucvc, lc, 1, h0, write_outc)
    run(ucv, l, 1, h, write_out)


def _lru(u, uc, conv_w, conv_b, wg, bg, lam, need_ctx):
    b, l, w = u.shape
    lc = uc.shape[1]
    out_shape = [jax.ShapeDtypeStruct((b, l, w), BF16)]
    out_specs = [pl.BlockSpec((1, l, w), lambda bi: (bi, 0, 0))]
    if need_ctx:
        out_shape.append(jax.ShapeDtypeStruct((b, lc, w), BF16))
        out_specs.append(pl.BlockSpec((1, lc, w), lambda bi: (bi, 0, 0)))
    res = pl.pallas_call(
        functools.partial(_lru_body, need_ctx=need_ctx),
        grid=(b,),
        in_specs=[
            pl.BlockSpec((1, l, w), lambda bi: (bi, 0, 0)),
            pl.BlockSpec((1, lc, w), lambda bi: (bi, 0, 0)),
            _resident((CONV_W, w)),
            _resident((1, w)),
            _resident((2, w, 2 * w)),
            _resident((2, 1, 2 * w)),
            _resident((2, 1, w)),
        ],
        out_specs=out_specs,
        out_shape=out_shape,
        scratch_shapes=[
            pltpu.VMEM((l + 2 * LRU_HALO, w), F32),
            pltpu.VMEM((lc + 2 * LRU_HALO, w), F32),
            pltpu.VMEM((l, w), F32),
            pltpu.VMEM((lc, w), F32),
            pltpu.VMEM((LRU_TC, w), F32),
        ],
        compiler_params=_params(("parallel",)),
    )(u, uc, conv_w, conv_b.reshape(1, w), wg, bg, lam.reshape(2, 1, w))
    return (res[0], res[1]) if need_ctx else (res[0], None)


def _mla_prep_body(*refs, rotate, need_q):
    it = iter(refs)
    cq_ref = next(it) if need_q else None
    ckv_ref, ra_ref = next(it), next(it)
    if need_q:
        cosq_ref = next(it)
        sinq_ref = next(it) if rotate else None
    cosk_ref = next(it)
    sink_ref = next(it) if rotate else None
    if need_q:
        gq_ref, wq_ref = next(it), next(it)
    gkv_ref, wkv_ref = next(it), next(it)
    q_out = next(it) if need_q else None
    k_out, v_out = next(it), next(it)
    nh = MLA_HEADS
    tm = ckv_ref.shape[1]
    lane = lax.broadcasted_iota(jnp.int32, (tm, LANES), 1)

    def rotated(x, cos, sin):
        if not rotate:
            return x * cos
        return x * cos + pltpu.roll(x, MLA_PAIR_DIST, 1) * sin

    if need_q:
        cqn = _rms(cq_ref[0].astype(F32), gq_ref[...]).astype(BF16)
        qm = jnp.dot(cqn, wq_ref[...], preferred_element_type=F32)
        cosq = cosq_ref[...]
        sinq = sinq_ref[...] if rotate else None
        for h in range(nh):
            qh = rotated(qm[:, h * LANES:(h + 1) * LANES], cosq, sinq)
            q_out[0, h] = qh.astype(q_out.dtype)

    kvn = _rms(ckv_ref[0].astype(F32), gkv_ref[...]).astype(BF16)
    kv = jnp.dot(kvn, wkv_ref[...], preferred_element_type=F32)
    rope = rotated(ra_ref[0].astype(F32), cosk_ref[...], sink_ref[...] if rotate else None)
    for h in range(nh):
        k_out[0, h] = (kv[:, h * LANES:(h + 1) * LANES] + rope).astype(k_out.dtype)
        vh = kv[:, (nh + h) * LANES:(nh + h + 1) * LANES]
        v_out[0, h] = jnp.where(lane == _sum_lane(h), 1.0, vh).astype(v_out.dtype)


def _sum_lane(h):
    return MLA_V if h % 2 == 0 else 0


def _mla_prep(cq, ckv, ra, cosq, sinq, cosk, sink, gq, wq, gkv, wkv, rotate, need_q, tm):
    b, l, _ = ckv.shape
    nh = MLA_HEADS
    row = lambda wd: pl.BlockSpec((1, tm, wd), lambda bi, i: (bi, i, 0))
    tab = pl.BlockSpec((tm, LANES), lambda bi, i: (i, 0))
    args, specs = [], []

    def add(a, s):
        args.append(a)
        specs.append(s)

    if need_q:
        add(cq, row(Q_LORA))
    add(ckv, row(KV_LORA))
    add(ra, row(LANES))
    if need_q:
        add(cosq, tab)
        if rotate:
            add(sinq, tab)
    add(cosk, tab)
    if rotate:
        add(sink, tab)
    if need_q:
        add(gq.reshape(1, Q_LORA), _resident((1, Q_LORA)))
        add(wq, _resident(wq.shape))
    add(gkv.reshape(1, KV_LORA), _resident((1, KV_LORA)))
    add(wkv, _resident(wkv.shape))

    head = pl.BlockSpec((1, nh, tm, LANES), lambda bi, i: (bi, 0, i, 0))
    hshape = jax.ShapeDtypeStruct((b, nh, l, LANES), ATTN_DT)
    n_out = 3 if need_q else 2
    res = pl.pallas_call(
        functools.partial(_mla_prep_body, rotate=rotate, need_q=need_q),
        grid=(b, l // tm),
        in_specs=specs,
        out_specs=[head] * n_out,
        out_shape=[hshape] * n_out,
        compiler_params=_params(("parallel", "parallel")),
    )(*args)
    return res if need_q else (None, res[0], res[1])


ATTN_PAIRS_PER_STEP = 4
ATTN_DT = jnp.float8_e4m3fn
ATTN_P_SHIFT = 8.0


def _attn_body(q_ref, *refs, n_kv, tq):
    kv_refs = refs[:2 * n_kv]
    o_ref = refs[2 * n_kv]
    s_bufs = refs[2 * n_kv + 1:2 * n_kv + 3]
    p_bufs = refs[2 * n_kv + 3:2 * n_kv + 5]
    m_bufs = refs[2 * n_kv + 5:2 * n_kv + 7]
    npair = ATTN_PAIRS_PER_STEP
    n_items = (q_ref.shape[2] // tq) * npair
    lks = [kv_refs[2 * j].shape[2] for j in range(n_kv)]
    offs = [sum(lks[:j]) for j in range(n_kv)]
    lane = lax.broadcasted_iota(jnp.int32, (tq, LANES), 1)
    nt_dims = (((1,), (1,)), ((), ()))

    s_bufs[1][...] = jnp.zeros(s_bufs[1].shape, s_bufs[1].dtype)
    p_bufs[0][...] = jnp.zeros(p_bufs[0].shape, p_bufs[0].dtype)
    m_bufs[1][...] = jnp.zeros(m_bufs[1].shape, m_bufs[1].dtype)

    def item(t):
        tile = lax.shift_right_logical(t, npair.bit_length() - 1)
        pair = jnp.bitwise_and(t, npair - 1)
        return pl.multiple_of(tile * tq, tq), pair

    def stages(t, slot):
        s_a, s_b = s_bufs[slot], s_bufs[1 - slot]
        p_b, p_c = p_bufs[1 - slot], p_bufs[slot]
        m_a, m_b = m_bufs[slot], m_bufs[1 - slot]

        row_a, pair_a = item(jnp.minimum(t, n_items - 1))
        for e in range(2):
            h = pair_a * 2 + e
            q = q_ref[0, h, pl.ds(row_a, tq), :]
            bm = None
            for j in range(n_kv):
                sj = lax.dot_general(q, kv_refs[2 * j][0, h], nt_dims,
                                     preferred_element_type=F32).astype(s_a.dtype)
                s_a[e, :, offs[j]:offs[j] + lks[j]] = sj
                for c0 in range(0, lks[j], LANES):
                    blk = sj[:, c0:c0 + LANES]
                    bm = blk if bm is None else jnp.maximum(bm, blk)
            m_a[e] = bm

        for e in range(2):
            s = s_b[e]
            m = m_b[e].astype(F32).max(axis=-1, keepdims=True)
            shift = (m - ATTN_P_SHIFT).astype(s.dtype)
            p_b[e] = jnp.exp2(s - shift).astype(p_b.dtype)

        row_c, pair_c = item(jnp.maximum(t - 2, 0))
        halves = []
        for e in range(2):
            h = pair_c * 2 + e
            o = None
            for j in range(n_kv):
                oj = jnp.dot(p_c[e, :, offs[j]:offs[j] + lks[j]], kv_refs[2 * j + 1][0, h],
                             preferred_element_type=F32)
                o = oj if o is None else o + oj
            one = _sum_lane(e)
            halves.append(o * (1.0 / o[:, one:one + 1]))
        blk = jnp.where(lane < MLA_V, halves[0], halves[1])
        o_ref[0, pair_c, pl.ds(row_c, tq), :] = blk.astype(o_ref.dtype)

    def body(i, carry):
        stages(2 * i, 0)
        stages(2 * i + 1, 1)
        return carry

    lax.fori_loop(0, (n_items + 2) // 2, body, 0)


def _attention(q, kvs, tq):
    b, nh, lq, _ = q.shape
    hs = 2 * ATTN_PAIRS_PER_STEP
    once = lambda n: pl.BlockSpec((1, hs, n, LANES), lambda bi, j: (bi, j, 0, 0))
    specs = [once(lq)]
    args = [q]
    lkt = 0
    for k, v in kvs:
        lk = k.shape[2]
        lkt += lk
        specs += [once(lk), once(lk)]
        args += [k, v]
    return pl.pallas_call(
        functools.partial(_attn_body, n_kv=len(kvs), tq=tq),
        grid=(b, nh // hs),
        in_specs=specs,
        out_specs=pl.BlockSpec((1, ATTN_PAIRS_PER_STEP, lq, LANES), lambda bi, j: (bi, j, 0, 0)),
        out_shape=jax.ShapeDtypeStruct((b, nh // 2, lq, LANES), BF16),
        scratch_shapes=[
            pltpu.VMEM((2, tq, lkt), BF16),
            pltpu.VMEM((2, tq, lkt), BF16),
            pltpu.VMEM((2, tq, lkt), ATTN_DT),
            pltpu.VMEM((2, tq, lkt), ATTN_DT),
            pltpu.VMEM((2, tq, LANES), BF16),
            pltpu.VMEM((2, tq, LANES), BF16),
        ],
        compiler_params=_params(("parallel", "parallel")),
    )(*args)


def _ret_body(*refs, need_ctx):
    it = iter(refs)
    q_ref, k_ref, v_ref = next(it), next(it), next(it)
    qc_ref = next(it) if need_ctx else None
    kc_ref, vc_ref = next(it), next(it)
    cos_ref, sin_ref = next(it), next(it)
    msum_ref, qdec_ref, kdec_ref, cd_ref = next(it), next(it), next(it), next(it)
    o_ref = next(it)
    oc_ref = next(it) if need_ctx else None
    qa, ka, va, u_s, r_s = (next(it) for _ in range(5))

    c = RET_CHUNK
    l = q_ref.shape[1]
    lc = kc_ref.shape[1]
    ncc = lc // c
    nt = (lc + l) // c
    kscale = RET_DK ** -0.5
    lane = lax.broadcasted_iota(jnp.int32, (c, LANES), 1)
    first_half = (lane % RET_DK) < (RET_DK // 2)

    def rotated(src_ref, t0, g0):
        x = src_ref[0, pl.ds(t0, c), :].astype(F32)
        sw = jnp.where(first_half, pltpu.roll(x, LANES - RET_DK // 2, 1),
                       pltpu.roll(x, RET_DK // 2, 1))
        return x * cos_ref[pl.ds(g0, c), :] + sw * sin_ref[pl.ds(g0, c), :]

    def stage(qsrc, ksrc, vsrc, n, base):
        def body(i, carry):
            t0 = pl.multiple_of(i * c, c)
            g0 = pl.multiple_of(base + i * c, c)
            if qsrc is not None:
                qa[pl.ds(g0, c), :] = rotated(qsrc, t0, g0)
            ka[pl.ds(g0, c), :] = rotated(ksrc, t0, g0) * kscale
            va[pl.ds(g0, c), :] = vsrc[0, pl.ds(t0, c), :]
            return carry

        lax.fori_loop(0, n // c, body, 0, unroll=2)

    stage(qc_ref, kc_ref, vc_ref, lc, 0)
    stage(q_ref, k_ref, v_ref, l, lc)

    tn_dims = (((0,), (0,)), ((), ()))
    nt_dims = (((1,), (1,)), ((), ()))

    def incr(g, carry):
        g0 = pl.multiple_of(g * c, c)
        kf = ka[pl.ds(g0, c), :]
        for e in range(2):
            v = va[pl.ds(g0, c), e * RET_DV:(e + 1) * RET_DV]
            kd = jnp.concatenate([kf * kdec_ref[0, e, 0], kf * kdec_ref[0, e, 1]], axis=1)
            u_s[g, e] = lax.dot_general(kd.astype(BF16), v, tn_dims, preferred_element_type=F32)
        return carry

    lax.fori_loop(0, nt, incr, 0, unroll=4)

    zero = tuple(jnp.zeros((LANES, RET_DV), F32) for _ in range(4))

    def recur(t, st):
        gs = (t, jnp.where(t < ncc, ncc - 1 - t, nt - 1 + ncc - t))
        new = []
        for d in range(2):
            rows = slice(d * LANES, (d + 1) * LANES)
            for e in range(2):
                r = st[2 * d + e]
                r_s[gs[d], e, rows, :] = r.astype(r_s.dtype)
                new.append(cd_ref[0, e, d] * r + u_s[gs[d], e, rows, :])
        return tuple(new)

    lax.fori_loop(0, nt, recur, zero, unroll=2)

    def emit(dst_ref, gbase):
        def body(i, carry):
            t0 = pl.multiple_of(i * c, c)
            g = gbase + i
            g0 = pl.multiple_of(g * c, c)
            qf = qa[pl.ds(g0, c), :]
            kb = ka[pl.ds(g0, c), :].astype(BF16)
            parts = []
            for e in range(2):
                v = va[pl.ds(g0, c), e * RET_DV:(e + 1) * RET_DV]
                qm = jnp.where((lane // RET_DK) == e, qf, 0.0).astype(BF16)
                s = lax.dot_general(qm, kb, nt_dims, preferred_element_type=F32)
                o = jnp.dot((s * msum_ref[0, e]).astype(BF16), v, preferred_element_type=F32)
                qd = jnp.concatenate([qf * qdec_ref[0, e, 0], qf * qdec_ref[0, e, 1]], axis=1)
                o = o + jnp.dot(qd.astype(BF16), r_s[g, e], preferred_element_type=F32)
                mu = jnp.mean(o, axis=-1, keepdims=True)
                oc = o - mu
                var = jnp.mean(oc * oc, axis=-1, keepdims=True)
                parts.append(oc * lax.rsqrt(var + EPS))
            dst_ref[0, pl.ds(t0, c), :] = jnp.concatenate(parts, axis=1).astype(dst_ref.dtype)
            return carry

        return body

    if need_ctx:
        lax.fori_loop(0, ncc, emit(oc_ref, 0), 0, unroll=2)
    lax.fori_loop(0, nt - ncc, emit(o_ref, ncc), 0, unroll=8)


def _retention(q, k, v, qc, kc, vc, tabs, dec, need_ctx):
    b, l, _ = q.shape
    lc = kc.shape[1]
    lt = lc + l
    c = RET_CHUNK
    nt = lt // c
    npair = RET_HEADS // 2
    seq = lambda n, wd: pl.BlockSpec((1, n, wd), lambda bi, p: (bi, 0, p))
    args = [q, k, v]
    specs = [seq(l, LANES), seq(l, LANES), seq(l, 2 * RET_DV)]
    if need_ctx:
        args.append(qc)
        specs.append(seq(lc, LANES))
    args += [kc, vc]
    specs += [seq(lc, LANES), seq(lc, 2 * RET_DV)]
    for t in tabs:
        args.append(t)
        specs.append(_resident((lt, LANES)))
    msum, qdec, kdec, cd = dec
    args += [msum, qdec, kdec, cd]
    specs += [
        pl.BlockSpec((1, 2, c, c), lambda bi, p: (p, 0, 0, 0)),
        pl.BlockSpec((1, 2, 2, c, LANES), lambda bi, p: (p, 0, 0, 0, 0)),
        pl.BlockSpec((1, 2, 2, c, LANES), lambda bi, p: (p, 0, 0, 0, 0)),
        pl.BlockSpec((1, 2, 2, 1, RET_DV), lambda bi, p: (p, 0, 0, 0, 0)),
    ]
    out_shape = [jax.ShapeDtypeStruct((b, l, RET_HEADS * RET_DV), BF16)]
    out_specs = [seq(l, 2 * RET_DV)]
    if need_ctx:
        out_shape.append(jax.ShapeDtypeStruct((b, lc, RET_HEADS * RET_DV), BF16))
        out_specs.append(seq(lc, 2 * RET_DV))
    res = pl.pallas_call(
        functools.partial(_ret_body, need_ctx=need_ctx),
        grid=(b, npair),
        in_specs=specs,
        out_specs=out_specs,
        out_shape=out_shape,
        scratch_shapes=[
            pltpu.VMEM((lt, LANES), F32),
            pltpu.VMEM((lt, LANES), F32),
            pltpu.VMEM((lt, 2 * RET_DV), BF16),
            pltpu.VMEM((nt, 2, 2 * LANES, RET_DV), F32),
            pltpu.VMEM((nt, 2, 2 * LANES, RET_DV), BF16),
        ],
        compiler_params=_params(("parallel", "parallel")),
    )(*args)
    return (res[0], res[1]) if need_ctx else (res[0], None)


FFN_CHUNKS = ((0, 6 * MXU_TILE), (6 * MXU_TILE, D_FF))
assert all(lo % MXU_TILE == 0 and hi % MXU_TILE == 0 for lo, hi in FFN_CHUNKS)


def _mix_ffn_body(x_ref, rec_ref, gl_ref, yb_ref, rn_ref, gr_ref, gm_ref, mod_ref,
                  woa_ref, wob_ref, woc_ref, wout_ref, g_ref, w1_ref, w3_ref, w2_ref, *rest, final):
    if final:
        gf_ref, o_ref = rest
    else:
        (o_ref,) = rest
    d = x_ref.shape[2]
    ga_a, sh_f, sc_f, ga_f = (mod_ref[0, j:j + 1, :] for j in range(4))

    ya = (jax.nn.gelu(gl_ref[0].astype(F32)) * rec_ref[0].astype(F32)).astype(BF16)
    yc = (jax.nn.silu(gr_ref[0].astype(F32)) * rn_ref[0].astype(F32)).astype(BF16)
    yb = jnp.concatenate([yb_ref[0, p] for p in range(yb_ref.shape[1])], axis=1)
    gm = gm_ref[0]
    m = jax.nn.sigmoid(gm[:, :d].astype(F32)) * jnp.dot(ya, woa_ref[...], preferred_element_type=F32)
    m = m + jax.nn.sigmoid(gm[:, d:2 * d].astype(F32)) * jnp.dot(
        yb, wob_ref[...], preferred_element_type=F32)
    m = m + jax.nn.sigmoid(gm[:, 2 * d:].astype(F32)) * jnp.dot(
        yc, woc_ref[...], preferred_element_type=F32)
    x = x_ref[0] + ga_a * jnp.dot(m.astype(BF16), wout_ref[...], preferred_element_type=F32)

    hb = (_rms(x, g_ref[...]) * (1.0 + sc_f) + sh_f).astype(BF16)
    acc = None
    for c0, c1 in FFN_CHUNKS:
        a = jnp.dot(hb, w1_ref[:, c0:c1], preferred_element_type=F32)
        bgate = jnp.dot(hb, w3_ref[:, c0:c1], preferred_element_type=F32)
        t = (jax.nn.silu(a) * bgate).astype(BF16)
        part = jnp.dot(t, w2_ref[c0:c1, :], preferred_element_type=F32)
        acc = part if acc is None else acc + part
    y = x + ga_f * acc
    if final:
        y = _rms(y, gf_ref[...])
    o_ref[0] = y


def _mix_ffn(x, rec, gl, yb, rn, gr, gm, mod4, woa, wob, woc, wout, g, w1, w3, w2, li, g_final, tm):
    b, l, d = x.shape
    final = g_final is not None
    row = lambda wd: pl.BlockSpec((1, tm, wd), lambda bi, i: (bi, i, 0))
    mod_map = (lambda bi, i: (bi, 0, 0)) if mod4.shape[0] > 1 else (lambda bi, i: (0, 0, 0))
    args = [x, rec, gl, yb, rn, gr, gm, mod4, woa, wob, woc, wout, g.reshape(1, d), w1, w3, w2]
    specs = [
        row(d), row(LRU_W), row(LRU_W),
        pl.BlockSpec((1, MLA_HEADS // 2, tm, LANES), lambda bi, i: (bi, 0, i, 0)),
        row(RET_HEADS * RET_DV), row(RET_HEADS * RET_DV), row(3 * d),
        pl.BlockSpec((1, 4, d), mod_map),
        _layer_spec(woa, li), _layer_spec(wob, li), _layer_spec(woc, li), _layer_spec(wout, li),
        _resident((1, d)),
        _layer_spec(w1, li), _layer_spec(w3, li), _layer_spec(w2, li),
    ]
    if final:
        args.append(g_final.reshape(1, d))
        specs.append(_resident((1, d)))
    return pl.pallas_call(
        functools.partial(_mix_ffn_body, final=final),
        grid=(b, l // tm),
        in_specs=specs,
        out_specs=row(d),
        out_shape=jax.ShapeDtypeStruct((b, l, d), F32),
        compiler_params=_params(("parallel", "parallel")),
    )(*args)


def _head_block(nope, rope):
    ref = nope if nope is not None else rope
    z = lambda n: jnp.zeros(ref.shape[:-1] + (n,), ref.dtype)
    half = MLA_ROPE // 2
    lo = MLA_PAIR_DIST - half
    r1, r2 = (z(half), z(half)) if rope is None else (rope[..., :half], rope[..., half:])
    n1, n2 = (z(lo), z(MLA_NOPE - lo)) if nope is None else (nope[..., :lo], nope[..., lo:])
    used = 2 * half + MLA_NOPE
    return jnp.concatenate([r1, n1, r2, n2, z(LANES - used)], axis=-1)


def _arrange_w_in(w_in):
    k0 = LRU_W + KV_LORA
    cols = [w_in[..., :k0], _head_block(None, w_in[..., k0:k0 + MLA_ROPE]), w_in[..., k0 + MLA_ROPE:]]
    return jnp.concatenate(cols, axis=-1).astype(BF16)


def _arrange_mla(w_uq, w_ukv):
    nh = MLA_HEADS
    wq = w_uq.reshape(Q_LORA, nh, MLA_NOPE + MLA_ROPE)
    wq_arr = _head_block(wq[..., :MLA_NOPE], wq[..., MLA_NOPE:]).reshape(Q_LORA, nh * LANES)
    wkv = w_ukv.reshape(KV_LORA, nh, MLA_NOPE + MLA_V)
    wk_arr = _head_block(wkv[..., :MLA_NOPE], None).reshape(KV_LORA, nh * LANES)
    wv = wkv[..., MLA_NOPE:]
    zv = jnp.zeros_like(wv)
    even = (jnp.arange(nh) % 2 == 0)[None, :, None]
    wv_arr = jnp.concatenate([jnp.where(even, wv, zv), jnp.where(even, zv, wv)], axis=-1)
    wv_arr = wv_arr.reshape(KV_LORA, nh * LANES)
    return wq_arr.astype(BF16), jnp.concatenate([wk_arr, wv_arr], axis=1).astype(BF16)


def _arrange_lru(wa, ba, wx, bx):
    def dense(wblk):
        eye = jnp.eye(LRU_BLOCKS, dtype=wblk.dtype)
        full = jnp.einsum("nde,nm->ndme", wblk, eye)
        return full.reshape(LRU_W, LRU_W)

    wg = jnp.stack([jnp.concatenate([dense(wa[d]), dense(wx[d])], axis=1) for d in range(2)])
    bg = jnp.stack([jnp.concatenate([ba[d], bx[d]])[None, :] for d in range(2)])
    return wg.astype(BF16), bg


def _mla_tables(n_tokens, rotate):
    scale = (MLA_NOPE + MLA_ROPE) ** -0.5 * math.log2(math.e)
    ones = jnp.ones((n_tokens, MLA_NOPE), F32)
    if rotate:
        rows = n_tokens // GRID_W
        rowi = jnp.repeat(jnp.arange(rows, dtype=F32), GRID_W)
        coli = jnp.tile(jnp.arange(GRID_W, dtype=F32), rows)
        n_freq = MLA_ROPE // 4
        inv = jnp.power(ROPE_BASE, -jnp.arange(n_freq, dtype=F32) / n_freq)
        ang = jnp.concatenate([rowi[:, None] * inv, coli[:, None] * inv], axis=-1)
        cos, sin = jnp.cos(ang), jnp.sin(ang)
    else:
        cos = jnp.ones((n_tokens, MLA_ROPE // 2), F32)
        sin = jnp.zeros((n_tokens, MLA_ROPE // 2), F32)
    cos2 = jnp.concatenate([cos, cos], axis=1)
    sin2 = jnp.concatenate([-sin, sin], axis=1)
    cosq = _head_block(ones, cos2) * scale
    sinq = _head_block(None, sin2) * scale
    cosk = _head_block(None, cos2)
    sink = _head_block(None, sin2)
    return cosq, sinq, cosk, sink


def _ret_tables(n):
    theta = 1.0 / jnp.power(10000.0, jnp.linspace(0.0, 1.0, RET_DK // 2, dtype=F32))
    pos = jnp.arange(n, dtype=F32)
    ang = pos[:, None] * theta
    cos, sin = jnp.cos(ang), jnp.sin(ang)
    reps = LANES // RET_DK
    cos_t = jnp.tile(jnp.concatenate([cos, cos], axis=1), (1, reps))
    sin_t = jnp.tile(jnp.concatenate([-sin, sin], axis=1), (1, reps))
    return cos_t, sin_t


def _ret_decay_tables():
    c = RET_CHUNK
    h = jnp.arange(RET_HEADS, dtype=F32)
    lgs = jnp.stack([jnp.log1p(-jnp.exp2(-5.0 - h)), jnp.log1p(-jnp.exp2(-5.5 - h))], axis=1)
    pos = jnp.arange(c, dtype=F32)
    diff = pos[:, None] - pos[None, :]
    lg = lgs[:, :, None, None]
    inner = jnp.where(diff >= 0, jnp.exp(lg * jnp.maximum(diff, 0.0)), 0.0)
    msum = inner[:, 0] + jnp.swapaxes(inner[:, 1], -1, -2)
    lg1 = lgs[:, :, None]
    qd_f = jnp.exp(lg1 * (pos + 1.0))
    kd_f = jnp.exp(lg1 * (c - 1.0 - pos))
    qd = jnp.stack([qd_f[:, 0], qd_f[:, 1, ::-1]], axis=1)
    kd = jnp.stack([kd_f[:, 0], kd_f[:, 1, ::-1]], axis=1)
    lane_head = (jnp.arange(LANES) // RET_DK)
    e_of_h = jnp.arange(RET_HEADS) % 2
    mask = (lane_head[None, :] == e_of_h[:, None]).astype(F32)
    qdec = qd[..., None] * mask[:, None, None, :]
    kdec = kd[..., None] * mask[:, None, None, :]
    cd = jnp.broadcast_to(jnp.exp(lgs * c)[:, :, None, None], (RET_HEADS, 2, 1, RET_DV))
    pair = lambda t: t.reshape(RET_HEADS // 2, 2, *t.shape[1:])
    return pair(msum), pair(qdec), pair(kdec), pair(cd)


def _row_tile(n, pref):
    t = min(pref, n)
    while n % t:
        t //= 2
    return t


def kernel(x, c, ctx, c_ctx, w_mod, b_mod, g_mix, g_ffn, w_in, conv_w, conv_b, lru_wa, lru_ba,
           lru_wx, lru_bx, lru_lam, g_q, w_uq, g_kv, w_ukv, w_oa, w_ob, w_oc, w_out,
           w_ff1, w_ff3, w_ff2, g_final):
    b, l, d = x.shape
    lc = ctx.shape[1]
    depth = w_mod.shape[0]
    assert d == D_MODEL and l % LRU_TC == 0 and lc % LRU_TC == 0 and l % GRID_W == 0

    tm = _row_tile(l, 512)
    tmc = _row_tile(lc, 256)
    tq = _row_tile(l, 256)
    tqc = _row_tile(lc, 256)

    lat_tabs = _mla_tables(l, True)
    ctx_tabs = _mla_tables(lc, False)
    ret_tabs = _ret_tables(lc + l)
    dec = _ret_decay_tables()

    rows = -(-(b + 1) // SUBLANES) * SUBLANES
    cvec = jnp.zeros((rows, d), F32).at[:b].set(c).at[b].set(c_ctx)
    mod_all = _modulation(cvec, w_mod, b_mod)

    w_arr = _arrange_w_in(w_in)
    woa, wob, woc, wout = (w.astype(BF16) for w in (w_oa, w_ob, w_oc, w_out))
    w1, w3, w2 = (w.astype(BF16) for w in (w_ff1, w_ff3, w_ff2))

    xc = ctx
    for li in range(depth):
        last = li == depth - 1
        mod = mod_all[li, :b].reshape(b, 1, 6, d)
        sh_a, sc_a = mod[:, :, 0], mod[:, :, 1]
        mc = mod_all[li, b].reshape(1, 1, 6, d)
        mcs = [mc[:, :, j] for j in range(6)]

        wq_arr, wkv_arr = _arrange_mla(w_uq[li], w_ukv[li])
        wg, bg = _arrange_lru(lru_wa[li], lru_ba[li], lru_wx[li], lru_bx[li])

        z = _inproj(x, g_mix[li], sh_a, sc_a, w_arr, li, KV_GROUPS + Q_GROUPS, tm)
        u, ckv, ra, rk, rv, cq, rq, g_lru, g_ret, gm = z
        if last:
            zc = _inproj(xc, g_mix[li], mcs[0], mcs[1], w_arr, li, KV_GROUPS, tmc)
            uc, ckvc, rac, rkc, rvc = zc
            cqc = rqc = None
        else:
            zc = _inproj(xc, g_mix[li], mcs[0], mcs[1], w_arr, li, KV_GROUPS + Q_GROUPS, tmc)
            uc, ckvc, rac, rkc, rvc, cqc, rqc, g_lru_c, g_ret_c, gm_c = zc
        need_ctx = not last

        rec, rec_c = _lru(u, uc, conv_w[li], conv_b[li], wg, bg, lru_lam[li], need_ctx)

        q, k, v = _mla_prep(cq, ckv, ra, *lat_tabs, g_q[li], wq_arr, g_kv[li], wkv_arr,
                            True, True, _row_tile(l, 1024))
        qc, kc, vc = _mla_prep(cqc, ckvc, rac, *ctx_tabs, g_q[li], wq_arr, g_kv[li], wkv_arr,
                               False, need_ctx, tmc)
        yb = _attention(q, [(kc, vc), (k, v)], tq)

        rn, rn_c = _retention(rq, rk, rv, rqc, rkc, rvc, ret_tabs, dec, need_ctx)

        x = _mix_ffn(x, rec, g_lru, yb, rn, g_ret, gm, mod[:, 0, 2:], woa, wob, woc, wout,
                     g_ffn[li], w1, w3, w2, li, g_final if last else None, tm)
        if need_ctx:
            yb_c = _attention(qc, [(kc, vc)], tqc)
            xc = _mix_ffn(xc, rec_c, g_lru_c, yb_c, rn_c, g_ret_c, gm_c, mc[:, 0, 2:], woa, wob, woc, wout,
                          g_ffn[li], w1, w3, w2, li, None, tmc)
    return x
```

```python
import functools
import math

import jax
import jax.numpy as jnp
from jax import lax
from jax.experimental import pallas as pl
from jax.experimental.pallas import tpu as pltpu

F32 = jnp.float32
BF16 = jnp.bfloat16

D_MODEL = 1024
EPS = 1e-6
GRID_W = 64
LRU_W = 512
LRU_BLOCKS = 8
LRU_BLOCK = LRU_W // LRU_BLOCKS
CONV_W = 4
CONV_PAD_L = 2
LRU_C = 8.0
MLA_HEADS = 8
MLA_NOPE = 64
MLA_ROPE = 32
MLA_V = 64
Q_LORA = 384
KV_LORA = 256
ROPE_BASE = 10000.0
RET_HEADS = 4
RET_DK = 64
RET_DV = 128
RET_CHUNK = 256
D_FF = 2816

LANES = 128
SUBLANES = 8
MXU_TILE = 256
MLA_PAIR_DIST = LANES // 2
VMEM_LIMIT = 56 * 1024 * 1024

KV_GROUPS = (LRU_W, KV_LORA, LANES, RET_HEADS * RET_DK, RET_HEADS * RET_DV)
Q_GROUPS = (Q_LORA, RET_HEADS * RET_DK, LRU_W, RET_HEADS * RET_DV, 3 * D_MODEL)
N_KV_ARR = sum(KV_GROUPS)


def _params(sem):
    return pltpu.CompilerParams(dimension_semantics=sem, vmem_limit_bytes=VMEM_LIMIT)


def _resident(shape):
    nd = len(shape)
    return pl.BlockSpec(shape, lambda *_: (0,) * nd, pipeline_mode=pl.Buffered(1))


def _layer_spec(stacked, li, cols=None):
    shape = stacked.shape[1:] if cols is None else stacked.shape[1:-1] + (cols,)
    nd = len(shape)
    return pl.BlockSpec((None,) + shape, lambda *_: (li,) + (0,) * nd, pipeline_mode=pl.Buffered(1))


def _rms(x, g):
    y = x * lax.rsqrt(jnp.mean(x * x, axis=-1, keepdims=True) + EPS)
    return y * g


def _mod_body(c_ref, w_ref, b_ref, o_ref):
    cv = c_ref[...]
    act = cv * jax.nn.sigmoid(cv)
    o_ref[0] = jnp.dot(act, w_ref[0], precision=lax.Precision.HIGHEST,
                       preferred_element_type=F32) + b_ref[0]


def _modulation(cvec, w_mod, b_mod):
    depth, d, n = w_mod.shape
    rows = cvec.shape[0]
    tn = n // 4
    return pl.pallas_call(
        _mod_body,
        grid=(depth, n // tn),
        in_specs=[
            pl.BlockSpec((rows, d), lambda l, j: (0, 0)),
            pl.BlockSpec((1, d, tn), lambda l, j: (l, 0, j)),
            pl.BlockSpec((1, 1, tn), lambda l, j: (l, 0, j)),
        ],
        out_specs=pl.BlockSpec((1, rows, tn), lambda l, j: (l, 0, j)),
        out_shape=jax.ShapeDtypeStruct((depth, rows, n), F32),
        compiler_params=_params(("arbitrary", "arbitrary")),
    )(cvec, w_mod, b_mod.reshape(depth, 1, n))


def _inproj_body(x_ref, g_ref, sh_ref, sc_ref, w_ref, *out_refs, widths):
    x = x_ref[0]
    h = _rms(x, g_ref[...]) * (1.0 + sc_ref[0]) + sh_ref[0]
    hb = h.astype(BF16)
    start, off = 0, 0
    for i, wd in enumerate(widths):
        off += wd
        if off % MXU_TILE == 0 or i == len(widths) - 1:
            base = sum(widths[:start])
            z = jnp.dot(hb, w_ref[:, base:off], preferred_element_type=F32)
            col = 0
            for j in range(start, i + 1):
                out_refs[j][0] = z[:, col:col + widths[j]].astype(out_refs[j].dtype)
                col += widths[j]
            start = i + 1


def _inproj(x, g, shift, scale, w_all, li, widths, tm):
    b, l, d = x.shape
    n = sum(widths)
    mod_map = (lambda bi, i: (bi, 0, 0)) if shift.shape[0] > 1 else (lambda bi, i: (0, 0, 0))
    return pl.pallas_call(
        functools.partial(_inproj_body, widths=widths),
        grid=(b, l // tm),
        in_specs=[
            pl.BlockSpec((1, tm, d), lambda bi, i: (bi, i, 0)),
            _resident((1, d)),
            pl.BlockSpec((1, 1, d), mod_map),
            pl.BlockSpec((1, 1, d), mod_map),
            _layer_spec(w_all, li, n),
        ],
        out_specs=[pl.BlockSpec((1, tm, wd), lambda bi, i: (bi, i, 0)) for wd in widths],
        out_shape=[jax.ShapeDtypeStruct((b, l, wd), BF16) for wd in widths],
        compiler_params=_params(("parallel", "parallel")),
    )(x, g.reshape(1, d), shift, scale, w_all)


LRU_TC = 256
LRU_HALO = SUBLANES


def _lru_body(u_ref, uc_ref, cw_ref, cb_ref, wg_ref, bg_ref, lam_ref, *rest, need_ctx):
    if need_ctx:
        out_ref, outc_ref, upad, upadc, ucv, ucvc, hbuf = rest
    else:
        out_ref, upad, upadc, ucv, ucvc, hbuf = rest
        outc_ref = None
    l = u_ref.shape[1]
    lc = uc_ref.shape[1]
    w = u_ref.shape[2]
    tc = LRU_TC
    nt = tc // SUBLANES

    zeros_halo = jnp.zeros((LRU_HALO, w), F32)
    for pad_ref, cv_ref, src_ref, n in ((upad, ucv, u_ref, l), (upadc, ucvc, uc_ref, lc)):
        pad_ref[0:LRU_HALO, :] = zeros_halo
        pad_ref[LRU_HALO + n:2 * LRU_HALO + n, :] = zeros_halo

        def copy(i, carry, pad_ref=pad_ref, src_ref=src_ref):
            t0 = pl.multiple_of(i * tc, tc)
            pad_ref[pl.ds(LRU_HALO + t0, tc), :] = src_ref[0, pl.ds(t0, tc), :].astype(F32)
            return carry

        lax.fori_loop(0, n // tc, copy, 0)

        def conv(i, carry, pad_ref=pad_ref, cv_ref=cv_ref):
            t0 = pl.multiple_of(i * tc, tc)
            win = pad_ref[pl.ds(t0, tc + 2 * LRU_HALO), :]
            u = cb_ref[...]
            for k in range(CONV_W):
                sh = (CONV_PAD_L - k) % (tc + 2 * LRU_HALO)
                tap = win if sh == 0 else pltpu.roll(win, sh, 0)
                u = u + cw_ref[k:k + 1, :] * tap[LRU_HALO:LRU_HALO + tc]
            cv_ref[pl.ds(t0, tc), :] = u
            return carry

        lax.fori_loop(0, n // tc, conv, 0)

    row = lax.broadcasted_iota(jnp.int32, (nt, SUBLANES, w), 1)

    def chunk_scan(cv_ref, t0, d, h):
        u = cv_ref[pl.ds(t0, tc), :]
        g = jnp.dot(u.astype(BF16), wg_ref[d], preferred_element_type=F32) + bg_ref[d]
        r = jax.nn.sigmoid(g[:, :w])
        gi = jax.nn.sigmoid(g[:, w:])
        z = -lam_ref[d]
        sp = jnp.maximum(z, 0.0) + jnp.log1p(jnp.exp(-jnp.abs(z)))
        log_a = r * (-LRU_C * sp)
        a = jnp.exp2(r * (-LRU_C * math.log2(math.e) * sp))
        th = jnp.tanh(log_a)
        om = -2.0 * th / (1.0 - th)
        root = jnp.where(om > 0.0, om * lax.rsqrt(om), 0.0)
        bb = root * (gi * u)
        a = a.reshape(nt, SUBLANES, w)
        bb = bb.reshape(nt, SUBLANES, w)
        s = 1
        while s < SUBLANES:
            if d == 0:
                keep = row >= s
                shift = s
            else:
                keep = row < SUBLANES - s
                shift = SUBLANES - s
            a_s = jnp.where(keep, pltpu.roll(a, shift, 1), 1.0)
            b_s = jnp.where(keep, pltpu.roll(bb, shift, 1), 0.0)
            bb = a * b_s + bb
            a = a * a_s
            s *= 2
        order = range(nt) if d == 0 else range(nt - 1, -1, -1)
        for k in order:
            hk = a[k] * h + bb[k]
            hbuf[k * SUBLANES:(k + 1) * SUBLANES, :] = hk
            h = hk[SUBLANES - 1:SUBLANES] if d == 0 else hk[0:1]
        return h

    def run(cv_ref, n, d, h, write):
        nchunks = n // tc

        def body(i, h):
            ci = i if d == 0 else nchunks - 1 - i
            t0 = pl.multiple_of(ci * tc, tc)
            h = chunk_scan(cv_ref, t0, d, h)
            write(t0)
            return h

        return lax.fori_loop(0, nchunks, body, h, unroll=2)

    def write_accc(t0):
        upadc[pl.ds(t0, tc), :] = hbuf[...]

    def write_acc(t0):
        upad[pl.ds(t0, tc), :] = hbuf[...]

    def write_outc(t0):
        if outc_ref is not None:
            outc_ref[0, pl.ds(t0, tc), :] = (upadc[pl.ds(t0, tc), :] + hbuf[...]).astype(outc_ref.dtype)

    def write_out(t0):
        out_ref[0, pl.ds(t0, tc), :] = (upad[pl.ds(t0, tc), :] + hbuf[...]).astype(out_ref.dtype)

    h0 = jnp.zeros((1, w), F32)
    h = run(ucvc, lc, 0, h0, write_accc)
    run(ucv, l, 0, h, write_acc)
    h = run(ucvc, lc, 1, h0, write_outc)
    run(ucv, l, 1, h, write_out)


def _lru(u, uc, conv_w, conv_b, wg, bg, lam, need_ctx):
    b, l, w = u.shape
    lc = uc.shape[1]
    out_shape = [jax.ShapeDtypeStruct((b, l, w), BF16)]
    out_specs = [pl.BlockSpec((1, l, w), lambda bi: (bi, 0, 0))]
    if need_ctx:
        out_shape.append(jax.ShapeDtypeStruct((b, lc, w), BF16))
        out_specs.append(pl.BlockSpec((1, lc, w), lambda bi: (bi, 0, 0)))
    res = pl.pallas_call(
        functools.partial(_lru_body, need_ctx=need_ctx),
        grid=(b,),
        in_specs=[
            pl.BlockSpec((1, l, w), lambda bi: (bi, 0, 0)),
            pl.BlockSpec((1, lc, w), lambda bi: (bi, 0, 0)),
            _resident((CONV_W, w)),
            _resident((1, w)),
            _resident((2, w, 2 * w)),
            _resident((2, 1, 2 * w)),
            _resident((2, 1, w)),
        ],
        out_specs=out_specs,
        out_shape=out_shape,
        scratch_shapes=[
            pltpu.VMEM((l + 2 * LRU_HALO, w), F32),
            pltpu.VMEM((lc + 2 * LRU_HALO, w), F32),
            pltpu.VMEM((l, w), F32),
            pltpu.VMEM((lc, w), F32),
            pltpu.VMEM((LRU_TC, w), F32),
        ],
        compiler_params=_params(("parallel",)),
    )(u, uc, conv_w, conv_b.reshape(1, w), wg, bg, lam.reshape(2, 1, w))
    return (res[0], res[1]) if need_ctx else (res[0], None)


def _mla_prep_body(*refs, rotate, need_q):
    it = iter(refs)
    cq_ref = next(it) if need_q else None
    ckv_ref, ra_ref = next(it), next(it)
    if need_q:
        cosq_ref = next(it)
        sinq_ref = next(it) if rotate else None
    cosk_ref = next(it)
    sink_ref = next(it) if rotate else None
    if need_q:
        gq_ref, wq_ref = next(it), next(it)
    gkv_ref, wkv_ref, vinv_ref = next(it), next(it), next(it)
    q_out = next(it) if need_q else None
    k_out, v_out = next(it), next(it)
    nh = MLA_HEADS
    tm = ckv_ref.shape[1]
    lane = lax.broadcasted_iota(jnp.int32, (tm, LANES), 1)

    def rotated(x, cos, sin):
        if not rotate:
            return x * cos
        return x * cos + pltpu.roll(x, MLA_PAIR_DIST, 1) * sin

    if need_q:
        cqn = _rms(cq_ref[0].astype(F32), gq_ref[...]).astype(BF16)
        qm = jnp.dot(cqn, wq_ref[...], preferred_element_type=F32)
        cosq = cosq_ref[...]
        sinq = sinq_ref[...] if rotate else None
        for h in range(nh):
            qh = rotated(qm[:, h * LANES:(h + 1) * LANES], cosq, sinq)
            q_out[0, h] = qh.astype(q_out.dtype)

    kvn = _rms(ckv_ref[0].astype(F32), gkv_ref[...]).astype(BF16)
    kv = jnp.dot(kvn, wkv_ref[...], preferred_element_type=F32)
    rope = rotated(ra_ref[0].astype(F32), cosk_ref[...], sink_ref[...] if rotate else None)
    for h in range(nh):
        k_out[0, h] = (kv[:, h * LANES:(h + 1) * LANES] + rope).astype(k_out.dtype)
        vh = kv[:, (nh + h) * LANES:(nh + h + 1) * LANES] * vinv_ref[h]
        v_out[0, h] = jnp.where(lane == _sum_lane(h), 1.0, vh).astype(v_out.dtype)


def _sum_lane(h):
    return MLA_V if h % 2 == 0 else 0


def _mla_prep(cq, ckv, ra, cosq, sinq, cosk, sink, gq, wq, gkv, wkv, vinv, rotate, need_q, tm):
    b, l, _ = ckv.shape
    nh = MLA_HEADS
    row = lambda wd: pl.BlockSpec((1, tm, wd), lambda bi, i: (bi, i, 0))
    tab = pl.BlockSpec((tm, LANES), lambda bi, i: (i, 0))
    args, specs = [], []

    def add(a, s):
        args.append(a)
        specs.append(s)

    if need_q:
        add(cq, row(Q_LORA))
    add(ckv, row(KV_LORA))
    add(ra, row(LANES))
    if need_q:
        add(cosq, tab)
        if rotate:
            add(sinq, tab)
    add(cosk, tab)
    if rotate:
        add(sink, tab)
    if need_q:
        add(gq.reshape(1, Q_LORA), _resident((1, Q_LORA)))
        add(wq, _resident(wq.shape))
    add(gkv.reshape(1, KV_LORA), _resident((1, KV_LORA)))
    add(wkv, _resident(wkv.shape))
    add(vinv, _resident(vinv.shape))

    head = pl.BlockSpec((1, nh, tm, LANES), lambda bi, i: (bi, 0, i, 0))
    qk_shape = jax.ShapeDtypeStruct((b, nh, l, LANES), BF16)
    v_shape = jax.ShapeDtypeStruct((b, nh, l, LANES), ATTN_DT)
    n_out = 3 if need_q else 2
    res = pl.pallas_call(
        functools.partial(_mla_prep_body, rotate=rotate, need_q=need_q),
        grid=(b, l // tm),
        in_specs=specs,
        out_specs=[head] * n_out,
        out_shape=[qk_shape] * (n_out - 1) + [v_shape],
        compiler_params=_params(("parallel", "parallel")),
    )(*args)
    return res if need_q else (None, res[0], res[1])


ATTN_PAIRS_PER_STEP = 2
ATTN_DT = jnp.float8_e4m3fn
ATTN_P_SHIFT = 8.0


def _attn_body(q_ref, *refs, n_kv, tq):
    kv_refs = refs[:2 * n_kv]
    vb_ref, o_ref = refs[2 * n_kv], refs[2 * n_kv + 1]
    s_bufs = refs[2 * n_kv + 2:2 * n_kv + 4]
    p_bufs = refs[2 * n_kv + 4:2 * n_kv + 6]
    m_bufs = refs[2 * n_kv + 6:2 * n_kv + 8]
    npair = ATTN_PAIRS_PER_STEP
    n_items = (q_ref.shape[2] // tq) * npair
    lks = [kv_refs[2 * j].shape[2] for j in range(n_kv)]
    offs = [sum(lks[:j]) for j in range(n_kv)]
    lane = lax.broadcasted_iota(jnp.int32, (tq, LANES), 1)
    nt_dims = (((1,), (1,)), ((), ()))

    s_bufs[1][...] = jnp.zeros(s_bufs[1].shape, s_bufs[1].dtype)
    p_bufs[0][...] = jnp.zeros(p_bufs[0].shape, p_bufs[0].dtype)
    m_bufs[1][...] = jnp.zeros(m_bufs[1].shape, m_bufs[1].dtype)

    def item(t):
        tile = lax.shift_right_logical(t, npair.bit_length() - 1)
        pair = jnp.bitwise_and(t, npair - 1)
        return pl.multiple_of(tile * tq, tq), pair

    def stages(t, slot):
        s_a, s_b = s_bufs[slot], s_bufs[1 - slot]
        p_b, p_c = p_bufs[1 - slot], p_bufs[slot]
        m_a, m_b = m_bufs[slot], m_bufs[1 - slot]

        row_a, pair_a = item(jnp.minimum(t, n_items - 1))
        for e in range(2):
            h = pair_a * 2 + e
            q = q_ref[0, h, pl.ds(row_a, tq), :]
            bm = None
            for j in range(n_kv):
                sj = lax.dot_general(q, kv_refs[2 * j][0, h], nt_dims,
                                     preferred_element_type=F32).astype(s_a.dtype)
                s_a[e, :, offs[j]:offs[j] + lks[j]] = sj
                for c0 in range(0, lks[j], LANES):
                    blk = sj[:, c0:c0 + LANES]
                    bm = blk if bm is None else jnp.maximum(bm, blk)
            m_a[e] = bm

        for e in range(2):
            s = s_b[e]
            m = m_b[e].astype(F32).max(axis=-1, keepdims=True)
            shift = (m - ATTN_P_SHIFT).astype(s.dtype)
            p_b[e] = jnp.exp2(s - shift).astype(p_b.dtype)

        row_c, pair_c = item(jnp.maximum(t - 2, 0))
        halves = []
        for e in range(2):
            h = pair_c * 2 + e
            o = None
            for j in range(n_kv):
                oj = jnp.dot(p_c[e, :, offs[j]:offs[j] + lks[j]], kv_refs[2 * j + 1][0, h],
                             preferred_element_type=F32)
                o = oj if o is None else o + oj
            one = _sum_lane(e)
            halves.append(o * (1.0 / o[:, one:one + 1]) * vb_ref[h])
        blk = jnp.where(lane < MLA_V, halves[0], halves[1])
        o_ref[0, pair_c, pl.ds(row_c, tq), :] = blk.astype(o_ref.dtype)

    def body(i, carry):
        stages(2 * i, 0)
        stages(2 * i + 1, 1)
        return carry

    lax.fori_loop(0, (n_items + 2) // 2, body, 0)


def _attention(q, kvs, vbound, tq):
    b, nh, lq, _ = q.shape
    hs = 2 * ATTN_PAIRS_PER_STEP
    once = lambda n: pl.BlockSpec((1, hs, n, LANES), lambda bi, j: (bi, j, 0, 0))
    specs = [once(lq)]
    args = [q]
    lkt = 0
    for k, v in kvs:
        lk = k.shape[2]
        lkt += lk
        specs += [once(lk), once(lk)]
        args += [k, v]
    specs.append(pl.BlockSpec((hs, 1, LANES), lambda bi, j: (j, 0, 0)))
    args.append(vbound)
    return pl.pallas_call(
        functools.partial(_attn_body, n_kv=len(kvs), tq=tq),
        grid=(b, nh // hs),
        in_specs=specs,
        out_specs=pl.BlockSpec((1, ATTN_PAIRS_PER_STEP, lq, LANES), lambda bi, j: (bi, j, 0, 0)),
        out_shape=jax.ShapeDtypeStruct((b, nh // 2, lq, LANES), BF16),
        scratch_shapes=[
            pltpu.VMEM((2, tq, lkt), BF16),
            pltpu.VMEM((2, tq, lkt), BF16),
            pltpu.VMEM((2, tq, lkt), ATTN_DT),
            pltpu.VMEM((2, tq, lkt), ATTN_DT),
            pltpu.VMEM((2, tq, LANES), BF16),
            pltpu.VMEM((2, tq, LANES), BF16),
        ],
        compiler_params=_params(("parallel", "parallel")),
    )(*args)


def _ret_body(*refs, need_ctx):
    it = iter(refs)
    q_ref, k_ref, v_ref = next(it), next(it), next(it)
    qc_ref = next(it) if need_ctx else None
    kc_ref, vc_ref = next(it), next(it)
    cos_ref, sin_ref = next(it), next(it)
    msum_ref, qdec_ref, kdec_ref, cd_ref = next(it), next(it), next(it), next(it)
    o_ref = next(it)
    oc_ref = next(it) if need_ctx else None
    qa, ka, va, u_s, r_s = (next(it) for _ in range(5))

    c = RET_CHUNK
    l = q_ref.shape[1]
    lc = kc_ref.shape[1]
    ncc = lc // c
    nt = (lc + l) // c
    kscale = RET_DK ** -0.5
    lane = lax.broadcasted_iota(jnp.int32, (c, LANES), 1)
    first_half = (lane % RET_DK) < (RET_DK // 2)

    def rotated(src_ref, t0, g0):
        x = src_ref[0, pl.ds(t0, c), :].astype(F32)
        sw = jnp.where(first_half, pltpu.roll(x, LANES - RET_DK // 2, 1),
                       pltpu.roll(x, RET_DK // 2, 1))
        return x * cos_ref[pl.ds(g0, c), :] + sw * sin_ref[pl.ds(g0, c), :]

    def stage(qsrc, ksrc, vsrc, n, base):
        def body(i, carry):
            t0 = pl.multiple_of(i * c, c)
            g0 = pl.multiple_of(base + i * c, c)
            if qsrc is not None:
                qa[pl.ds(g0, c), :] = rotated(qsrc, t0, g0)
            ka[pl.ds(g0, c), :] = rotated(ksrc, t0, g0) * kscale
            va[pl.ds(g0, c), :] = vsrc[0, pl.ds(t0, c), :]
            return carry

        lax.fori_loop(0, n // c, body, 0, unroll=2)

    stage(qc_ref, kc_ref, vc_ref, lc, 0)
    stage(q_ref, k_ref, v_ref, l, lc)

    tn_dims = (((0,), (0,)), ((), ()))
    nt_dims = (((1,), (1,)), ((), ()))

    def incr(g, carry):
        g0 = pl.multiple_of(g * c, c)
        kf = ka[pl.ds(g0, c), :]
        for e in range(2):
            v = va[pl.ds(g0, c), e * RET_DV:(e + 1) * RET_DV]
            kd = jnp.concatenate([kf * kdec_ref[0, e, 0], kf * kdec_ref[0, e, 1]], axis=1)
            u_s[g, e] = lax.dot_general(kd.astype(BF16), v, tn_dims, preferred_element_type=F32)
        return carry

    lax.fori_loop(0, nt, incr, 0, unroll=4)

    zero = tuple(jnp.zeros((LANES, RET_DV), F32) for _ in range(4))

    def recur(t, st):
        gs = (t, jnp.where(t < ncc, ncc - 1 - t, nt - 1 + ncc - t))
        new = []
        for d in range(2):
            rows = slice(d * LANES, (d + 1) * LANES)
            for e in range(2):
                r = st[2 * d + e]
                r_s[gs[d], e, rows, :] = r.astype(r_s.dtype)
                new.append(cd_ref[0, e, d] * r + u_s[gs[d], e, rows, :])
        return tuple(new)

    lax.fori_loop(0, nt, recur, zero, unroll=2)

    def emit(dst_ref, gbase):
        def body(i, carry):
            t0 = pl.multiple_of(i * c, c)
            g = gbase + i
            g0 = pl.multiple_of(g * c, c)
            qf = qa[pl.ds(g0, c), :]
            kb = ka[pl.ds(g0, c), :].astype(BF16)
            parts = []
            for e in range(2):
                v = va[pl.ds(g0, c), e * RET_DV:(e + 1) * RET_DV]
                qm = jnp.where((lane // RET_DK) == e, qf, 0.0).astype(BF16)
                s = lax.dot_general(qm, kb, nt_dims, preferred_element_type=F32)
                o = jnp.dot((s * msum_ref[0, e]).astype(BF16), v, preferred_element_type=F32)
                qd = jnp.concatenate([qf * qdec_ref[0, e, 0], qf * qdec_ref[0, e, 1]], axis=1)
                o = o + jnp.dot(qd.astype(BF16), r_s[g, e], preferred_element_type=F32)
                mu = jnp.mean(o, axis=-1, keepdims=True)
                oc = o - mu
                var = jnp.mean(oc * oc, axis=-1, keepdims=True)
                parts.append(oc * lax.rsqrt(var + EPS))
            dst_ref[0, pl.ds(t0, c), :] = jnp.concatenate(parts, axis=1).astype(dst_ref.dtype)
            return carry

        return body

    if need_ctx:
        lax.fori_loop(0, ncc, emit(oc_ref, 0), 0, unroll=2)
    lax.fori_loop(0, nt - ncc, emit(o_ref, ncc), 0, unroll=8)


def _retention(q, k, v, qc, kc, vc, tabs, dec, need_ctx):
    b, l, _ = q.shape
    lc = kc.shape[1]
    lt = lc + l
    c = RET_CHUNK
    nt = lt // c
    npair = RET_HEADS // 2
    seq = lambda n, wd: pl.BlockSpec((1, n, wd), lambda bi, p: (bi, 0, p))
    args = [q, k, v]
    specs = [seq(l, LANES), seq(l, LANES), seq(l, 2 * RET_DV)]
    if need_ctx:
        args.append(qc)
        specs.append(seq(lc, LANES))
    args += [kc, vc]
    specs += [seq(lc, LANES), seq(lc, 2 * RET_DV)]
    for t in tabs:
        args.append(t)
        specs.append(_resident((lt, LANES)))
    msum, qdec, kdec, cd = dec
    args += [msum, qdec, kdec, cd]
    specs += [
        pl.BlockSpec((1, 2, c, c), lambda bi, p: (p, 0, 0, 0)),
        pl.BlockSpec((1, 2, 2, c, LANES), lambda bi, p: (p, 0, 0, 0, 0)),
        pl.BlockSpec((1, 2, 2, c, LANES), lambda bi, p: (p, 0, 0, 0, 0)),
        pl.BlockSpec((1, 2, 2, 1, RET_DV), lambda bi, p: (p, 0, 0, 0, 0)),
    ]
    out_shape = [jax.ShapeDtypeStruct((b, l, RET_HEADS * RET_DV), BF16)]
    out_specs = [seq(l, 2 * RET_DV)]
    if need_ctx:
        out_shape.append(jax.ShapeDtypeStruct((b, lc, RET_HEADS * RET_DV), BF16))
        out_specs.append(seq(lc, 2 * RET_DV))
    res = pl.pallas_call(
        functools.partial(_ret_body, need_ctx=need_ctx),
        grid=(b, npair),
        in_specs=specs,
        out_specs=out_specs,
        out_shape=out_shape,
        scratch_shapes=[
            pltpu.VMEM((lt, LANES), F32),
            pltpu.VMEM((lt, LANES), F32),
            pltpu.VMEM((lt, 2 * RET_DV), BF16),
            pltpu.VMEM((nt, 2, 2 * LANES, RET_DV), F32),
            pltpu.VMEM((nt, 2, 2 * LANES, RET_DV), BF16),
        ],
        compiler_params=_params(("parallel", "parallel")),
    )(*args)
    return (res[0], res[1]) if need_ctx else (res[0], None)


FFN_CHUNKS = ((0, 6 * MXU_TILE), (6 * MXU_TILE, D_FF))
assert all(lo % MXU_TILE == 0 and hi % MXU_TILE == 0 for lo, hi in FFN_CHUNKS)


def _mix_ffn_body(x_ref, rec_ref, gl_ref, yb_ref, rn_ref, gr_ref, gm_ref, mod_ref,
                  woa_ref, wob_ref, woc_ref, wout_ref, g_ref, w1_ref, w3_ref, w2_ref, *rest, final):
    if final:
        gf_ref, o_ref = rest
    else:
        (o_ref,) = rest
    d = x_ref.shape[2]
    ga_a, sh_f, sc_f, ga_f = (mod_ref[0, j:j + 1, :] for j in range(4))

    ya = (jax.nn.gelu(gl_ref[0].astype(F32)) * rec_ref[0].astype(F32)).astype(BF16)
    yc = (jax.nn.silu(gr_ref[0].astype(F32)) * rn_ref[0].astype(F32)).astype(BF16)
    yb = jnp.concatenate([yb_ref[0, p] for p in range(yb_ref.shape[1])], axis=1)
    gm = gm_ref[0]
    m = jax.nn.sigmoid(gm[:, :d].astype(F32)) * jnp.dot(ya, woa_ref[...], preferred_element_type=F32)
    m = m + jax.nn.sigmoid(gm[:, d:2 * d].astype(F32)) * jnp.dot(
        yb, wob_ref[...], preferred_element_type=F32)
    m = m + jax.nn.sigmoid(gm[:, 2 * d:].astype(F32)) * jnp.dot(
        yc, woc_ref[...], preferred_element_type=F32)
    x = x_ref[0] + ga_a * jnp.dot(m.astype(BF16), wout_ref[...], preferred_element_type=F32)

    hb = (_rms(x, g_ref[...]) * (1.0 + sc_f) + sh_f).astype(BF16)
    acc = None
    for c0, c1 in FFN_CHUNKS:
        a = jnp.dot(hb, w1_ref[:, c0:c1], preferred_element_type=F32)
        bgate = jnp.dot(hb, w3_ref[:, c0:c1], preferred_element_type=F32)
        t = (jax.nn.silu(a) * bgate).astype(BF16)
        part = jnp.dot(t, w2_ref[c0:c1, :], preferred_element_type=F32)
        acc = part if acc is None else acc + part
    y = x + ga_f * acc
    if final:
        y = _rms(y, gf_ref[...])
    o_ref[0] = y


def _mix_ffn(x, rec, gl, yb, rn, gr, gm, mod4, woa, wob, woc, wout, g, w1, w3, w2, li, g_final, tm):
    b, l, d = x.shape
    final = g_final is not None
    row = lambda wd: pl.BlockSpec((1, tm, wd), lambda bi, i: (bi, i, 0))
    mod_map = (lambda bi, i: (bi, 0, 0)) if mod4.shape[0] > 1 else (lambda bi, i: (0, 0, 0))
    args = [x, rec, gl, yb, rn, gr, gm, mod4, woa, wob, woc, wout, g.reshape(1, d), w1, w3, w2]
    specs = [
        row(d), row(LRU_W), row(LRU_W),
        pl.BlockSpec((1, MLA_HEADS // 2, tm, LANES), lambda bi, i: (bi, 0, i, 0)),
        row(RET_HEADS * RET_DV), row(RET_HEADS * RET_DV), row(3 * d),
        pl.BlockSpec((1, 4, d), mod_map),
        _layer_spec(woa, li), _layer_spec(wob, li), _layer_spec(woc, li), _layer_spec(wout, li),
        _resident((1, d)),
        _layer_spec(w1, li), _layer_spec(w3, li), _layer_spec(w2, li),
    ]
    if final:
        args.append(g_final.reshape(1, d))
        specs.append(_resident((1, d)))
    return pl.pallas_call(
        functools.partial(_mix_ffn_body, final=final),
        grid=(b, l // tm),
        in_specs=specs,
        out_specs=row(d),
        out_shape=jax.ShapeDtypeStruct((b, l, d), F32),
        compiler_params=_params(("parallel", "parallel")),
    )(*args)


def _head_block(nope, rope):
    ref = nope if nope is not None else rope
    z = lambda n: jnp.zeros(ref.shape[:-1] + (n,), ref.dtype)
    half = MLA_ROPE // 2
    lo = MLA_PAIR_DIST - half
    r1, r2 = (z(half), z(half)) if rope is None else (rope[..., :half], rope[..., half:])
    n1, n2 = (z(lo), z(MLA_NOPE - lo)) if nope is None else (nope[..., :lo], nope[..., lo:])
    used = 2 * half + MLA_NOPE
    return jnp.concatenate([r1, n1, r2, n2, z(LANES - used)], axis=-1)


def _arrange_w_in(w_in):
    k0 = LRU_W + KV_LORA
    cols = [w_in[..., :k0], _head_block(None, w_in[..., k0:k0 + MLA_ROPE]), w_in[..., k0 + MLA_ROPE:]]
    return jnp.concatenate(cols, axis=-1).astype(BF16)


def _arrange_mla(w_uq, w_ukv):
    nh = MLA_HEADS
    wq = w_uq.reshape(Q_LORA, nh, MLA_NOPE + MLA_ROPE)
    wq_arr = _head_block(wq[..., :MLA_NOPE], wq[..., MLA_NOPE:]).reshape(Q_LORA, nh * LANES)
    wkv = w_ukv.reshape(KV_LORA, nh, MLA_NOPE + MLA_V)
    wk_arr = _head_block(wkv[..., :MLA_NOPE], None).reshape(KV_LORA, nh * LANES)
    wv = wkv[..., MLA_NOPE:]
    zv = jnp.zeros_like(wv)
    even = (jnp.arange(nh) % 2 == 0)[None, :, None]
    wv_arr = jnp.concatenate([jnp.where(even, wv, zv), jnp.where(even, zv, wv)], axis=-1)
    wv_arr = wv_arr.reshape(KV_LORA, nh * LANES)
    return wq_arr.astype(BF16), jnp.concatenate([wk_arr, wv_arr], axis=1).astype(BF16)


def _value_bounds(w_ukv, g_kv):
    nh = MLA_HEADS
    wv = w_ukv.reshape(KV_LORA, nh, MLA_NOPE + MLA_V)[..., MLA_NOPE:]
    bound = jnp.sqrt(jnp.sum(wv * wv, axis=0)) * (KV_LORA ** 0.5 * jnp.max(jnp.abs(g_kv)))
    bound = jnp.where(bound > 0.0, bound, 1.0)
    one = jnp.ones_like(bound)
    even = (jnp.arange(nh) % 2 == 0)[:, None]
    full = jnp.concatenate([jnp.where(even, bound, one), jnp.where(even, one, bound)], axis=-1)
    return full.reshape(nh, 1, LANES)


def _arrange_lru(wa, ba, wx, bx):
    def dense(wblk):
        eye = jnp.eye(LRU_BLOCKS, dtype=wblk.dtype)
        full = jnp.einsum("nde,nm->ndme", wblk, eye)
        return full.reshape(LRU_W, LRU_W)

    wg = jnp.stack([jnp.concatenate([dense(wa[d]), dense(wx[d])], axis=1) for d in range(2)])
    bg = jnp.stack([jnp.concatenate([ba[d], bx[d]])[None, :] for d in range(2)])
    return wg.astype(BF16), bg


def _mla_tables(n_tokens, rotate):
    scale = (MLA_NOPE + MLA_ROPE) ** -0.5 * math.log2(math.e)
    ones = jnp.ones((n_tokens, MLA_NOPE), F32)
    if rotate:
        rows = n_tokens // GRID_W
        rowi = jnp.repeat(jnp.arange(rows, dtype=F32), GRID_W)
        coli = jnp.tile(jnp.arange(GRID_W, dtype=F32), rows)
        n_freq = MLA_ROPE // 4
        inv = jnp.power(ROPE_BASE, -jnp.arange(n_freq, dtype=F32) / n_freq)
        ang = jnp.concatenate([rowi[:, None] * inv, coli[:, None] * inv], axis=-1)
        cos, sin = jnp.cos(ang), jnp.sin(ang)
    else:
        cos = jnp.ones((n_tokens, MLA_ROPE // 2), F32)
        sin = jnp.zeros((n_tokens, MLA_ROPE // 2), F32)
    cos2 = jnp.concatenate([cos, cos], axis=1)
    sin2 = jnp.concatenate([-sin, sin], axis=1)
    cosq = _head_block(ones, cos2) * scale
    sinq = _head_block(None, sin2) * scale
    cosk = _head_block(None, cos2)
    sink = _head_block(None, sin2)
    return cosq, sinq, cosk, sink


def _ret_tables(n):
    theta = 1.0 / jnp.power(10000.0, jnp.linspace(0.0, 1.0, RET_DK // 2, dtype=F32))
    pos = jnp.arange(n, dtype=F32)
    ang = pos[:, None] * theta
    cos, sin = jnp.cos(ang), jnp.sin(ang)
    reps = LANES // RET_DK
    cos_t = jnp.tile(jnp.concatenate([cos, cos], axis=1), (1, reps))
    sin_t = jnp.tile(jnp.concatenate([-sin, sin], axis=1), (1, reps))
    return cos_t, sin_t


def _ret_decay_tables():
    c = RET_CHUNK
    h = jnp.arange(RET_HEADS, dtype=F32)
    lgs = jnp.stack([jnp.log1p(-jnp.exp2(-5.0 - h)), jnp.log1p(-jnp.exp2(-5.5 - h))], axis=1)
    pos = jnp.arange(c, dtype=F32)
    diff = pos[:, None] - pos[None, :]
    lg = lgs[:, :, None, None]
    inner = jnp.where(diff >= 0, jnp.exp(lg * jnp.maximum(diff, 0.0)), 0.0)
    msum = inner[:, 0] + jnp.swapaxes(inner[:, 1], -1, -2)
    lg1 = lgs[:, :, None]
    qd_f = jnp.exp(lg1 * (pos + 1.0))
    kd_f = jnp.exp(lg1 * (c - 1.0 - pos))
    qd = jnp.stack([qd_f[:, 0], qd_f[:, 1, ::-1]], axis=1)
    kd = jnp.stack([kd_f[:, 0], kd_f[:, 1, ::-1]], axis=1)
    lane_head = (jnp.arange(LANES) // RET_DK)
    e_of_h = jnp.arange(RET_HEADS) % 2
    mask = (lane_head[None, :] == e_of_h[:, None]).astype(F32)
    qdec = qd[..., None] * mask[:, None, None, :]
    kdec = kd[..., None] * mask[:, None, None, :]
    cd = jnp.broadcast_to(jnp.exp(lgs * c)[:, :, None, None], (RET_HEADS, 2, 1, RET_DV))
    pair = lambda t: t.reshape(RET_HEADS // 2, 2, *t.shape[1:])
    return pair(msum), pair(qdec), pair(kdec), pair(cd)


def _row_tile(n, pref):
    t = min(pref, n)
    while n % t:
        t //= 2
    return t


def kernel(x, c, ctx, c_ctx, w_mod, b_mod, g_mix, g_ffn, w_in, conv_w, conv_b, lru_wa, lru_ba,
           lru_wx, lru_bx, lru_lam, g_q, w_uq, g_kv, w_ukv, w_oa, w_ob, w_oc, w_out,
           w_ff1, w_ff3, w_ff2, g_final):
    b, l, d = x.shape
    lc = ctx.shape[1]
    depth = w_mod.shape[0]
    assert d == D_MODEL and l % LRU_TC == 0 and lc % LRU_TC == 0 and l % GRID_W == 0

    tm = _row_tile(l, 512)
    tmc = _row_tile(lc, 256)
    tq = _row_tile(l, 256)
    tqc = _row_tile(lc, 256)

    lat_tabs = _mla_tables(l, True)
    ctx_tabs = _mla_tables(lc, False)
    ret_tabs = _ret_tables(lc + l)
    dec = _ret_decay_tables()

    rows = -(-(b + 1) // SUBLANES) * SUBLANES
    cvec = jnp.zeros((rows, d), F32).at[:b].set(c).at[b].set(c_ctx)
    mod_all = _modulation(cvec, w_mod, b_mod)

    w_arr = _arrange_w_in(w_in)
    woa, wob, woc, wout = (w.astype(BF16) for w in (w_oa, w_ob, w_oc, w_out))
    w1, w3, w2 = (w.astype(BF16) for w in (w_ff1, w_ff3, w_ff2))

    xc = ctx
    for li in range(depth):
        last = li == depth - 1
        mod = mod_all[li, :b].reshape(b, 1, 6, d)
        sh_a, sc_a = mod[:, :, 0], mod[:, :, 1]
        mc = mod_all[li, b].reshape(1, 1, 6, d)
        mcs = [mc[:, :, j] for j in range(6)]

        wq_arr, wkv_arr = _arrange_mla(w_uq[li], w_ukv[li])
        wg, bg = _arrange_lru(lru_wa[li], lru_ba[li], lru_wx[li], lru_bx[li])

        z = _inproj(x, g_mix[li], sh_a, sc_a, w_arr, li, KV_GROUPS + Q_GROUPS, tm)
        u, ckv, ra, rk, rv, cq, rq, g_lru, g_ret, gm = z
        if last:
            zc = _inproj(xc, g_mix[li], mcs[0], mcs[1], w_arr, li, KV_GROUPS, tmc)
            uc, ckvc, rac, rkc, rvc = zc
            cqc = rqc = None
        else:
            zc = _inproj(xc, g_mix[li], mcs[0], mcs[1], w_arr, li, KV_GROUPS + Q_GROUPS, tmc)
            uc, ckvc, rac, rkc, rvc, cqc, rqc, g_lru_c, g_ret_c, gm_c = zc
        need_ctx = not last

        rec, rec_c = _lru(u, uc, conv_w[li], conv_b[li], wg, bg, lru_lam[li], need_ctx)

        vbound = _value_bounds(w_ukv[li], g_kv[li])
        vinv = 1.0 / vbound
        q, k, v = _mla_prep(cq, ckv, ra, *lat_tabs, g_q[li], wq_arr, g_kv[li], wkv_arr, vinv,
                            True, True, _row_tile(l, 1024))
        qc, kc, vc = _mla_prep(cqc, ckvc, rac, *ctx_tabs, g_q[li], wq_arr, g_kv[li], wkv_arr, vinv,
                               False, need_ctx, tmc)
        yb = _attention(q, [(kc, vc), (k, v)], vbound, tq)

        rn, rn_c = _retention(rq, rk, rv, rqc, rkc, rvc, ret_tabs, dec, need_ctx)

        x = _mix_ffn(x, rec, g_lru, yb, rn, g_ret, gm, mod[:, 0, 2:], woa, wob, woc, wout,
                     g_ffn[li], w1, w3, w2, li, g_final if last else None, tm)
        if need_ctx:
            yb_c = _attention(qc, [(kc, vc)], vbound, tqc)
            xc = _mix_ffn(xc, rec_c, g_lru_c, yb_c, rn_c, g_ret_c, gm_c, mc[:, 0, 2:], woa, wob, woc, wout,
                          g_ffn[li], w1, w3, w2, li, None, tmc)
    return x
```

```python
import functools
import math

import jax
import jax.numpy as jnp
from jax import lax
from jax.experimental import pallas as pl
from jax.experimental.pallas import tpu as pltpu

F32 = jnp.float32
BF16 = jnp.bfloat16

D_MODEL = 1024
EPS = 1e-6
GRID_W = 64
LRU_W = 512
LRU_BLOCKS = 8
LRU_BLOCK = LRU_W // LRU_BLOCKS
CONV_W = 4
CONV_PAD_L = 2
LRU_C = 8.0
MLA_HEADS = 8
MLA_NOPE = 64
MLA_ROPE = 32
MLA_V = 64
Q_LORA = 384
KV_LORA = 256
ROPE_BASE = 10000.0
RET_HEADS = 4
RET_DK = 64
RET_DV = 128
RET_CHUNK = 256
D_FF = 2816

LANES = 128
SUBLANES = 8
MXU_TILE = 256
MLA_PAIR_DIST = LANES // 2
VMEM_LIMIT = 56 * 1024 * 1024

KV_GROUPS = (LRU_W, KV_LORA, LANES, RET_HEADS * RET_DK, RET_HEADS * RET_DV)
Q_GROUPS = (Q_LORA, RET_HEADS * RET_DK, LRU_W, RET_HEADS * RET_DV, 3 * D_MODEL)
N_KV_ARR = sum(KV_GROUPS)


def _params(sem):
    return pltpu.CompilerParams(dimension_semantics=sem, vmem_limit_bytes=VMEM_LIMIT)


def _resident(shape):
    nd = len(shape)
    return pl.BlockSpec(shape, lambda *_: (0,) * nd, pipeline_mode=pl.Buffered(1))


def _layer_spec(stacked, li, cols=None):
    shape = stacked.shape[1:] if cols is None else stacked.shape[1:-1] + (cols,)
    nd = len(shape)
    return pl.BlockSpec((None,) + shape, lambda *_: (li,) + (0,) * nd, pipeline_mode=pl.Buffered(1))


def _rms(x, g):
    y = x * lax.rsqrt(jnp.mean(x * x, axis=-1, keepdims=True) + EPS)
    return y * g


def _mod_body(c_ref, w_ref, b_ref, o_ref):
    cv = c_ref[...]
    act = cv * jax.nn.sigmoid(cv)
    o_ref[0] = jnp.dot(act, w_ref[0], precision=lax.Precision.HIGHEST,
                       preferred_element_type=F32) + b_ref[0]


def _modulation(cvec, w_mod, b_mod):
    depth, d, n = w_mod.shape
    rows = cvec.shape[0]
    tn = n // 4
    return pl.pallas_call(
        _mod_body,
        grid=(depth, n // tn),
        in_specs=[
            pl.BlockSpec((rows, d), lambda l, j: (0, 0)),
            pl.BlockSpec((1, d, tn), lambda l, j: (l, 0, j)),
            pl.BlockSpec((1, 1, tn), lambda l, j: (l, 0, j)),
        ],
        out_specs=pl.BlockSpec((1, rows, tn), lambda l, j: (l, 0, j)),
        out_shape=jax.ShapeDtypeStruct((depth, rows, n), F32),
        compiler_params=_params(("arbitrary", "arbitrary")),
    )(cvec, w_mod, b_mod.reshape(depth, 1, n))


def _inproj_body(x_ref, g_ref, sh_ref, sc_ref, w_ref, *out_refs, widths):
    x = x_ref[0]
    h = _rms(x, g_ref[...]) * (1.0 + sc_ref[0]) + sh_ref[0]
    hb = h.astype(BF16)
    start, off = 0, 0
    for i, wd in enumerate(widths):
        off += wd
        if off % MXU_TILE == 0 or i == len(widths) - 1:
            base = sum(widths[:start])
            z = jnp.dot(hb, w_ref[:, base:off], preferred_element_type=F32)
            col = 0
            for j in range(start, i + 1):
                out_refs[j][0] = z[:, col:col + widths[j]].astype(out_refs[j].dtype)
                col += widths[j]
            start = i + 1


def _inproj(x, g, shift, scale, w_all, li, widths, tm):
    b, l, d = x.shape
    n = sum(widths)
    mod_map = (lambda bi, i: (bi, 0, 0)) if shift.shape[0] > 1 else (lambda bi, i: (0, 0, 0))
    return pl.pallas_call(
        functools.partial(_inproj_body, widths=widths),
        grid=(b, l // tm),
        in_specs=[
            pl.BlockSpec((1, tm, d), lambda bi, i: (bi, i, 0)),
            _resident((1, d)),
            pl.BlockSpec((1, 1, d), mod_map),
            pl.BlockSpec((1, 1, d), mod_map),
            _layer_spec(w_all, li, n),
        ],
        out_specs=[pl.BlockSpec((1, tm, wd), lambda bi, i: (bi, i, 0)) for wd in widths],
        out_shape=[jax.ShapeDtypeStruct((b, l, wd), BF16) for wd in widths],
        compiler_params=_params(("parallel", "parallel")),
    )(x, g.reshape(1, d), shift, scale, w_all)


LRU_TC = 256
LRU_HALO = SUBLANES


def _lru_body(u_ref, uc_ref, cw_ref, cb_ref, wg_ref, bg_ref, lam_ref, *rest, need_ctx):
    if need_ctx:
        out_ref, outc_ref, upad, upadc, ucv, ucvc, hbuf = rest
    else:
        out_ref, upad, upadc, ucv, ucvc, hbuf = rest
        outc_ref = None
    l = u_ref.shape[1]
    lc = uc_ref.shape[1]
    w = u_ref.shape[2]
    tc = LRU_TC
    nt = tc // SUBLANES

    zeros_halo = jnp.zeros((LRU_HALO, w), F32)
    for pad_ref, cv_ref, src_ref, n in ((upad, ucv, u_ref, l), (upadc, ucvc, uc_ref, lc)):
        pad_ref[0:LRU_HALO, :] = zeros_halo
        pad_ref[LRU_HALO + n:2 * LRU_HALO + n, :] = zeros_halo

        def copy(i, carry, pad_ref=pad_ref, src_ref=src_ref):
            t0 = pl.multiple_of(i * tc, tc)
            pad_ref[pl.ds(LRU_HALO + t0, tc), :] = src_ref[0, pl.ds(t0, tc), :].astype(F32)
            return carry

        lax.fori_loop(0, n // tc, copy, 0)

        def conv(i, carry, pad_ref=pad_ref, cv_ref=cv_ref):
            t0 = pl.multiple_of(i * tc, tc)
            win = pad_ref[pl.ds(t0, tc + 2 * LRU_HALO), :]
            u = cb_ref[...]
            for k in range(CONV_W):
                sh = (CONV_PAD_L - k) % (tc + 2 * LRU_HALO)
                tap = win if sh == 0 else pltpu.roll(win, sh, 0)
                u = u + cw_ref[k:k + 1, :] * tap[LRU_HALO:LRU_HALO + tc]
            cv_ref[pl.ds(t0, tc), :] = u
            return carry

        lax.fori_loop(0, n // tc, conv, 0)

    row = lax.broadcasted_iota(jnp.int32, (nt, SUBLANES, w), 1)

    def chunk_scan(cv_ref, t0, d, h):
        u = cv_ref[pl.ds(t0, tc), :]
        g = jnp.dot(u.astype(BF16), wg_ref[d], preferred_element_type=F32) + bg_ref[d]
        r = jax.nn.sigmoid(g[:, :w])
        gi = jax.nn.sigmoid(g[:, w:])
        z = -lam_ref[d]
        sp = jnp.maximum(z, 0.0) + jnp.log1p(jnp.exp(-jnp.abs(z)))
        log_a = r * (-LRU_C * sp)
        a = jnp.exp2(r * (-LRU_C * math.log2(math.e) * sp))
        th = jnp.tanh(log_a)
        om = -2.0 * th / (1.0 - th)
        root = jnp.where(om > 0.0, om * lax.rsqrt(om), 0.0)
        bb = root * (gi * u)
        a = a.reshape(nt, SUBLANES, w)
        bb = bb.reshape(nt, SUBLANES, w)
        s = 1
        while s < SUBLANES:
            if d == 0:
                keep = row >= s
                shift = s
            else:
                keep = row < SUBLANES - s
                shift = SUBLANES - s
            a_s = jnp.where(keep, pltpu.roll(a, shift, 1), 1.0)
            b_s = jnp.where(keep, pltpu.roll(bb, shift, 1), 0.0)
            bb = a * b_s + bb
            a = a * a_s
            s *= 2
        order = range(nt) if d == 0 else range(nt - 1, -1, -1)
        for k in order:
            hk = a[k] * h + bb[k]
            hbuf[k * SUBLANES:(k + 1) * SUBLANES, :] = hk
            h = hk[SUBLANES - 1:SUBLANES] if d == 0 else hk[0:1]
        return h

    def run(cv_ref, n, d, h, write):
        nchunks = n // tc

        def body(i, h):
            ci = i if d == 0 else nchunks - 1 - i
            t0 = pl.multiple_of(ci * tc, tc)
            h = chunk_scan(cv_ref, t0, d, h)
            write(t0)
            return h

        return lax.fori_loop(0, nchunks, body, h, unroll=2)

    def write_accc(t0):
        upadc[pl.ds(t0, tc), :] = hbuf[...]

    def write_acc(t0):
        upad[pl.ds(t0, tc), :] = hbuf[...]

    def write_outc(t0):
        if outc_ref is not None:
            outc_ref[0, pl.ds(t0, tc), :] = (upadc[pl.ds(t0, tc), :] + hbuf[...]).astype(outc_ref.dtype)

    def write_out(t0):
        out_ref[0, pl.ds(t0, tc), :] = (upad[pl.ds(t0, tc), :] + hbuf[...]).astype(out_ref.dtype)

    h0 = jnp.zeros((1, w), F32)
    h = run(ucvc, lc, 0, h0, write_accc)
    run(ucv, l, 0, h, write_acc)
    h = run(ucvc, lc, 1, h0, write_outc)
    run(ucv, l, 1, h, write_out)


def _lru(u, uc, conv_w, conv_b, wg, bg, lam, need_ctx):
    b, l, w = u.shape
    lc = uc.shape[1]
    out_shape = [jax.ShapeDtypeStruct((b, l, w), BF16)]
    out_specs = [pl.BlockSpec((1, l, w), lambda bi: (bi, 0, 0))]
    if need_ctx:
        out_shape.append(jax.ShapeDtypeStruct((b, lc, w), BF16))
        out_specs.append(pl.BlockSpec((1, lc, w), lambda bi: (bi, 0, 0)))
    res = pl.pallas_call(
        functools.partial(_lru_body, need_ctx=need_ctx),
        grid=(b,),
        in_specs=[
            pl.BlockSpec((1, l, w), lambda bi: (bi, 0, 0)),
            pl.BlockSpec((1, lc, w), lambda bi: (bi, 0, 0)),
            _resident((CONV_W, w)),
            _resident((1, w)),
            _resident((2, w, 2 * w)),
            _resident((2, 1, 2 * w)),
            _resident((2, 1, w)),
        ],
        out_specs=out_specs,
        out_shape=out_shape,
        scratch_shapes=[
            pltpu.VMEM((l + 2 * LRU_HALO, w), F32),
            pltpu.VMEM((lc + 2 * LRU_HALO, w), F32),
            pltpu.VMEM((l, w), F32),
            pltpu.VMEM((lc, w), F32),
            pltpu.VMEM((LRU_TC, w), F32),
        ],
        compiler_params=_params(("parallel",)),
    )(u, uc, conv_w, conv_b.reshape(1, w), wg, bg, lam.reshape(2, 1, w))
    return (res[0], res[1]) if need_ctx else (res[0], None)


def _mla_prep_body(*refs, rotate, need_q):
    it = iter(refs)
    cq_ref = next(it) if need_q else None
    ckv_ref, ra_ref = next(it), next(it)
    if need_q:
        cosq_ref = next(it)
        sinq_ref = next(it) if rotate else None
    cosk_ref = next(it)
    sink_ref = next(it) if rotate else None
    if need_q:
        gq_ref, wq_ref = next(it), next(it)
    gkv_ref, wkv_ref, vinv_ref = next(it), next(it), next(it)
    q_out = next(it) if need_q else None
    k_out, v_out = next(it), next(it)
    nh = MLA_HEADS
    tm = ckv_ref.shape[1]
    lane = lax.broadcasted_iota(jnp.int32, (tm, LANES), 1)

    def rotated(x, cos, sin):
        if not rotate:
            return x * cos
        return x * cos + pltpu.roll(x, MLA_PAIR_DIST, 1) * sin

    if need_q:
        cqn = _rms(cq_ref[0].astype(F32), gq_ref[...]).astype(BF16)
        qm = jnp.dot(cqn, wq_ref[...], preferred_element_type=F32)
        cosq = cosq_ref[...]
        sinq = sinq_ref[...] if rotate else None
        for h in range(nh):
            qh = rotated(qm[:, h * LANES:(h + 1) * LANES], cosq, sinq)
            q_out[0, h] = qh.astype(q_out.dtype)

    kvn = _rms(ckv_ref[0].astype(F32), gkv_ref[...]).astype(BF16)
    kv = jnp.dot(kvn, wkv_ref[...], preferred_element_type=F32)
    rope = rotated(ra_ref[0].astype(F32), cosk_ref[...], sink_ref[...] if rotate else None)
    for h in range(nh):
        k_out[0, h] = (kv[:, h * LANES:(h + 1) * LANES] + rope).T.astype(k_out.dtype)
        vh = kv[:, (nh + h) * LANES:(nh + h + 1) * LANES] * vinv_ref[h]
        v_out[0, h] = jnp.where(lane == _sum_lane(h), 1.0, vh).astype(v_out.dtype)


def _sum_lane(h):
    return MLA_V if h % 2 == 0 else 0


def _mla_prep(cq, ckv, ra, cosq, sinq, cosk, sink, gq, wq, gkv, wkv, vinv, rotate, need_q, tm):
    b, l, _ = ckv.shape
    nh = MLA_HEADS
    row = lambda wd: pl.BlockSpec((1, tm, wd), lambda bi, i: (bi, i, 0))
    tab = pl.BlockSpec((tm, LANES), lambda bi, i: (i, 0))
    args, specs = [], []

    def add(a, s):
        args.append(a)
        specs.append(s)

    if need_q:
        add(cq, row(Q_LORA))
    add(ckv, row(KV_LORA))
    add(ra, row(LANES))
    if need_q:
        add(cosq, tab)
        if rotate:
            add(sinq, tab)
    add(cosk, tab)
    if rotate:
        add(sink, tab)
    if need_q:
        add(gq.reshape(1, Q_LORA), _resident((1, Q_LORA)))
        add(wq, _resident(wq.shape))
    add(gkv.reshape(1, KV_LORA), _resident((1, KV_LORA)))
    add(wkv, _resident(wkv.shape))
    add(vinv, _resident(vinv.shape))

    head = pl.BlockSpec((1, nh, tm, LANES), lambda bi, i: (bi, 0, i, 0))
    qk_shape = jax.ShapeDtypeStruct((b, nh, l, LANES), BF16)
    v_shape = jax.ShapeDtypeStruct((b, nh, l, LANES), ATTN_DT)
    n_out = 3 if need_q else 2
    res = pl.pallas_call(
        functools.partial(_mla_prep_body, rotate=rotate, need_q=need_q),
        grid=(b, l // tm),
        in_specs=specs,
        out_specs=[head] * (n_out - 2) + [pl.BlockSpec((1, nh, LANES, tm), lambda bi, i: (bi, 0, 0, i)), head],
        out_shape=[qk_shape] * (n_out - 2) + [jax.ShapeDtypeStruct((b, nh, LANES, l), BF16), v_shape],
        compiler_params=_params(("parallel", "parallel")),
    )(*args)
    return res if need_q else (None, res[0], res[1])


ATTN_PAIRS_PER_STEP = 2
ATTN_DT = jnp.float8_e4m3fn
ATTN_P_SHIFT = 8.0


def _attn_body(q_ref, *refs, n_kv, tq):
    kv_refs = refs[:2 * n_kv]
    vb_ref, o_ref = refs[2 * n_kv], refs[2 * n_kv + 1]
    s_bufs = refs[2 * n_kv + 2:2 * n_kv + 4]
    p_bufs = refs[2 * n_kv + 4:2 * n_kv + 6]
    m_bufs = refs[2 * n_kv + 6:2 * n_kv + 8]
    npair = ATTN_PAIRS_PER_STEP
    n_items = (q_ref.shape[2] // tq) * npair
    lks = [kv_refs[2 * j + 1].shape[2] for j in range(n_kv)]
    offs = [sum(lks[:j]) for j in range(n_kv)]
    lane = lax.broadcasted_iota(jnp.int32, (tq, LANES), 1)
    nt_dims = (((1,), (1,)), ((), ()))

    s_bufs[1][...] = jnp.zeros(s_bufs[1].shape, s_bufs[1].dtype)
    p_bufs[0][...] = jnp.zeros(p_bufs[0].shape, p_bufs[0].dtype)
    m_bufs[1][...] = jnp.zeros(m_bufs[1].shape, m_bufs[1].dtype)

    def item(t):
        tile = lax.shift_right_logical(t, npair.bit_length() - 1)
        pair = jnp.bitwise_and(t, npair - 1)
        return pl.multiple_of(tile * tq, tq), pair

    def stages(t, slot):
        s_a, s_b = s_bufs[slot], s_bufs[1 - slot]
        p_b, p_c = p_bufs[1 - slot], p_bufs[slot]
        m_a, m_b = m_bufs[slot], m_bufs[1 - slot]

        row_a, pair_a = item(jnp.minimum(t, n_items - 1))
        for e in range(2):
            h = pair_a * 2 + e
            q = q_ref[0, h, pl.ds(row_a, tq), :]
            bm = None
            for j in range(n_kv):
                sj = jnp.dot(q, kv_refs[2 * j][0, h], preferred_element_type=F32).astype(s_a.dtype)
                s_a[e, :, offs[j]:offs[j] + lks[j]] = sj
                for c0 in range(0, lks[j], LANES):
                    blk = sj[:, c0:c0 + LANES]
                    bm = blk if bm is None else jnp.maximum(bm, blk)
            m_a[e] = bm

        for e in range(2):
            s = s_b[e]
            m = m_b[e].astype(F32).max(axis=-1, keepdims=True)
            shift = (m - ATTN_P_SHIFT).astype(s.dtype)
            p_b[e] = jnp.exp2(s - shift).astype(p_b.dtype)

        row_c, pair_c = item(jnp.maximum(t - 2, 0))
        halves = []
        for e in range(2):
            h = pair_c * 2 + e
            o = None
            for j in range(n_kv):
                oj = jnp.dot(p_c[e, :, offs[j]:offs[j] + lks[j]], kv_refs[2 * j + 1][0, h],
                             preferred_element_type=F32)
                o = oj if o is None else o + oj
            one = _sum_lane(e)
            halves.append(o * (1.0 / o[:, one:one + 1]) * vb_ref[h])
        blk = jnp.where(lane < MLA_V, halves[0], halves[1])
        o_ref[0, pair_c, pl.ds(row_c, tq), :] = blk.astype(o_ref.dtype)

    def body(i, carry):
        stages(2 * i, 0)
        stages(2 * i + 1, 1)
        return carry

    lax.fori_loop(0, (n_items + 2) // 2, body, 0)


def _attention(q, kvs, vbound, tq):
    b, nh, lq, _ = q.shape
    hs = 2 * ATTN_PAIRS_PER_STEP
    once = lambda n: pl.BlockSpec((1, hs, n, LANES), lambda bi, j: (bi, j, 0, 0))
    specs = [once(lq)]
    args = [q]
    lkt = 0
    for k, v in kvs:
        lk = v.shape[2]
        lkt += lk
        specs += [pl.BlockSpec((1, hs, LANES, lk), lambda bi, j: (bi, j, 0, 0)), once(lk)]
        args += [k, v]
    specs.append(pl.BlockSpec((hs, 1, LANES), lambda bi, j: (j, 0, 0)))
    args.append(vbound)
    return pl.pallas_call(
        functools.partial(_attn_body, n_kv=len(kvs), tq=tq),
        grid=(b, nh // hs),
        in_specs=specs,
        out_specs=pl.BlockSpec((1, ATTN_PAIRS_PER_STEP, lq, LANES), lambda bi, j: (bi, j, 0, 0)),
        out_shape=jax.ShapeDtypeStruct((b, nh // 2, lq, LANES), BF16),
        scratch_shapes=[
            pltpu.VMEM((2, tq, lkt), BF16),
            pltpu.VMEM((2, tq, lkt), BF16),
            pltpu.VMEM((2, tq, lkt), ATTN_DT),
            pltpu.VMEM((2, tq, lkt), ATTN_DT),
            pltpu.VMEM((2, tq, LANES), BF16),
            pltpu.VMEM((2, tq, LANES), BF16),
        ],
        compiler_params=_params(("parallel", "parallel")),
    )(*args)


def _ret_body(*refs, need_ctx):
    it = iter(refs)
    q_ref, k_ref, v_ref = next(it), next(it), next(it)
    qc_ref = next(it) if need_ctx else None
    kc_ref, vc_ref = next(it), next(it)
    cos_ref, sin_ref = next(it), next(it)
    msum_ref, qdec_ref, kdec_ref, cd_ref = next(it), next(it), next(it), next(it)
    o_ref = next(it)
    oc_ref = next(it) if need_ctx else None
    qa, ka, va, u_s, r_s = (next(it) for _ in range(5))

    c = RET_CHUNK
    l = q_ref.shape[1]
    lc = kc_ref.shape[1]
    ncc = lc // c
    nt = (lc + l) // c
    kscale = RET_DK ** -0.5
    lane = lax.broadcasted_iota(jnp.int32, (c, LANES), 1)
    first_half = (lane % RET_DK) < (RET_DK // 2)

    def rotated(src_ref, t0, g0):
        x = src_ref[0, pl.ds(t0, c), :].astype(F32)
        sw = jnp.where(first_half, pltpu.roll(x, LANES - RET_DK // 2, 1),
                       pltpu.roll(x, RET_DK // 2, 1))
        return x * cos_ref[pl.ds(g0, c), :] + sw * sin_ref[pl.ds(g0, c), :]

    def stage(qsrc, ksrc, vsrc, n, base):
        def body(i, carry):
            t0 = pl.multiple_of(i * c, c)
            g0 = pl.multiple_of(base + i * c, c)
            if qsrc is not None:
                qa[pl.ds(g0, c), :] = rotated(qsrc, t0, g0)
            ka[pl.ds(g0, c), :] = rotated(ksrc, t0, g0) * kscale
            va[pl.ds(g0, c), :] = vsrc[0, pl.ds(t0, c), :]
            return carry

        lax.fori_loop(0, n // c, body, 0, unroll=2)

    stage(qc_ref, kc_ref, vc_ref, lc, 0)
    stage(q_ref, k_ref, v_ref, l, lc)

    tn_dims = (((0,), (0,)), ((), ()))
    nt_dims = (((1,), (1,)), ((), ()))

    def incr(g, carry):
        g0 = pl.multiple_of(g * c, c)
        kf = ka[pl.ds(g0, c), :]
        for e in range(2):
            v = va[pl.ds(g0, c), e * RET_DV:(e + 1) * RET_DV]
            kd = jnp.concatenate([kf * kdec_ref[0, e, 0], kf * kdec_ref[0, e, 1]], axis=1)
            u_s[g, e] = lax.dot_general(kd.astype(BF16), v, tn_dims, preferred_element_type=F32)
        return carry

    lax.fori_loop(0, nt, incr, 0, unroll=4)

    zero = tuple(jnp.zeros((LANES, RET_DV), F32) for _ in range(4))

    def recur(t, st):
        gs = (t, jnp.where(t < ncc, ncc - 1 - t, nt - 1 + ncc - t))
        new = []
        for d in range(2):
            rows = slice(d * LANES, (d + 1) * LANES)
            for e in range(2):
                r = st[2 * d + e]
                r_s[gs[d], e, rows, :] = r.astype(r_s.dtype)
                new.append(cd_ref[0, e, d] * r + u_s[gs[d], e, rows, :])
        return tuple(new)

    lax.fori_loop(0, nt, recur, zero, unroll=2)

    def emit(dst_ref, gbase):
        def body(i, carry):
            t0 = pl.multiple_of(i * c, c)
            g = gbase + i
            g0 = pl.multiple_of(g * c, c)
            qf = qa[pl.ds(g0, c), :]
            kb = ka[pl.ds(g0, c), :].astype(BF16)
            parts = []
            for e in range(2):
                v = va[pl.ds(g0, c), e * RET_DV:(e + 1) * RET_DV]
                qm = jnp.where((lane // RET_DK) == e, qf, 0.0).astype(BF16)
                s = lax.dot_general(qm, kb, nt_dims, preferred_element_type=F32)
                o = jnp.dot((s * msum_ref[0, e]).astype(BF16), v, preferred_element_type=F32)
                qd = jnp.concatenate([qf * qdec_ref[0, e, 0], qf * qdec_ref[0, e, 1]], axis=1)
                o = o + jnp.dot(qd.astype(BF16), r_s[g, e], preferred_element_type=F32)
                mu = jnp.mean(o, axis=-1, keepdims=True)
                oc = o - mu
                var = jnp.mean(oc * oc, axis=-1, keepdims=True)
                parts.append(oc * lax.rsqrt(var + EPS))
            dst_ref[0, pl.ds(t0, c), :] = jnp.concatenate(parts, axis=1).astype(dst_ref.dtype)
            return carry

        return body

    if need_ctx:
        lax.fori_loop(0, ncc, emit(oc_ref, 0), 0, unroll=2)
    lax.fori_loop(0, nt - ncc, emit(o_ref, ncc), 0, unroll=8)


def _retention(q, k, v, qc, kc, vc, tabs, dec, need_ctx):
    b, l, _ = q.shape
    lc = kc.shape[1]
    lt = lc + l
    c = RET_CHUNK
    nt = lt // c
    npair = RET_HEADS // 2
    seq = lambda n, wd: pl.BlockSpec((1, n, wd), lambda bi, p: (bi, 0, p))
    args = [q, k, v]
    specs = [seq(l, LANES), seq(l, LANES), seq(l, 2 * RET_DV)]
    if need_ctx:
        args.append(qc)
        specs.append(seq(lc, LANES))
    args += [kc, vc]
    specs += [seq(lc, LANES), seq(lc, 2 * RET_DV)]
    for t in tabs:
        args.append(t)
        specs.append(_resident((lt, LANES)))
    msum, qdec, kdec, cd = dec
    args += [msum, qdec, kdec, cd]
    specs += [
        pl.BlockSpec((1, 2, c, c), lambda bi, p: (p, 0, 0, 0)),
        pl.BlockSpec((1, 2, 2, c, LANES), lambda bi, p: (p, 0, 0, 0, 0)),
        pl.BlockSpec((1, 2, 2, c, LANES), lambda bi, p: (p, 0, 0, 0, 0)),
        pl.BlockSpec((1, 2, 2, 1, RET_DV), lambda bi, p: (p, 0, 0, 0, 0)),
    ]
    out_shape = [jax.ShapeDtypeStruct((b, l, RET_HEADS * RET_DV), BF16)]
    out_specs = [seq(l, 2 * RET_DV)]
    if need_ctx:
        out_shape.append(jax.ShapeDtypeStruct((b, lc, RET_HEADS * RET_DV), BF16))
        out_specs.append(seq(lc, 2 * RET_DV))
    res = pl.pallas_call(
        functools.partial(_ret_body, need_ctx=need_ctx),
        grid=(b, npair),
        in_specs=specs,
        out_specs=out_specs,
        out_shape=out_shape,
        scratch_shapes=[
            pltpu.VMEM((lt, LANES), F32),
            pltpu.VMEM((lt, LANES), F32),
            pltpu.VMEM((lt, 2 * RET_DV), BF16),
            pltpu.VMEM((nt, 2, 2 * LANES, RET_DV), F32),
            pltpu.VMEM((nt, 2, 2 * LANES, RET_DV), BF16),
        ],
        compiler_params=_params(("parallel", "parallel")),
    )(*args)
    return (res[0], res[1]) if need_ctx else (res[0], None)


FFN_CHUNKS = ((0, 6 * MXU_TILE), (6 * MXU_TILE, D_FF))
assert all(lo % MXU_TILE == 0 and hi % MXU_TILE == 0 for lo, hi in FFN_CHUNKS)


def _mix_ffn_body(x_ref, rec_ref, gl_ref, yb_ref, rn_ref, gr_ref, gm_ref, mod_ref,
                  woa_ref, wob_ref, woc_ref, wout_ref, g_ref, w1_ref, w3_ref, w2_ref, *rest, final):
    if final:
        gf_ref, o_ref = rest
    else:
        (o_ref,) = rest
    d = x_ref.shape[2]
    ga_a, sh_f, sc_f, ga_f = (mod_ref[0, j:j + 1, :] for j in range(4))

    ya = (jax.nn.gelu(gl_ref[0].astype(F32)) * rec_ref[0].astype(F32)).astype(BF16)
    yc = (jax.nn.silu(gr_ref[0].astype(F32)) * rn_ref[0].astype(F32)).astype(BF16)
    yb = jnp.concatenate([yb_ref[0, p] for p in range(yb_ref.shape[1])], axis=1)
    gm = gm_ref[0]
    m = jax.nn.sigmoid(gm[:, :d].astype(F32)) * jnp.dot(ya, woa_ref[...], preferred_element_type=F32)
    m = m + jax.nn.sigmoid(gm[:, d:2 * d].astype(F32)) * jnp.dot(
        yb, wob_ref[...], preferred_element_type=F32)
    m = m + jax.nn.sigmoid(gm[:, 2 * d:].astype(F32)) * jnp.dot(
        yc, woc_ref[...], preferred_element_type=F32)
    x = x_ref[0] + ga_a * jnp.dot(m.astype(BF16), wout_ref[...], preferred_element_type=F32)

    hb = (_rms(x, g_ref[...]) * (1.0 + sc_f) + sh_f).astype(BF16)
    acc = None
    for c0, c1 in FFN_CHUNKS:
        a = jnp.dot(hb, w1_ref[:, c0:c1], preferred_element_type=F32)
        bgate = jnp.dot(hb, w3_ref[:, c0:c1], preferred_element_type=F32)
        t = (jax.nn.silu(a) * bgate).astype(BF16)
        part = jnp.dot(t, w2_ref[c0:c1, :], preferred_element_type=F32)
        acc = part if acc is None else acc + part
    y = x + ga_f * acc
    if final:
        y = _rms(y, gf_ref[...])
    o_ref[0] = y


def _mix_ffn(x, rec, gl, yb, rn, gr, gm, mod4, woa, wob, woc, wout, g, w1, w3, w2, li, g_final, tm):
    b, l, d = x.shape
    final = g_final is not None
    row = lambda wd: pl.BlockSpec((1, tm, wd), lambda bi, i: (bi, i, 0))
    mod_map = (lambda bi, i: (bi, 0, 0)) if mod4.shape[0] > 1 else (lambda bi, i: (0, 0, 0))
    args = [x, rec, gl, yb, rn, gr, gm, mod4, woa, wob, woc, wout, g.reshape(1, d), w1, w3, w2]
    specs = [
        row(d), row(LRU_W), row(LRU_W),
        pl.BlockSpec((1, MLA_HEADS // 2, tm, LANES), lambda bi, i: (bi, 0, i, 0)),
        row(RET_HEADS * RET_DV), row(RET_HEADS * RET_DV), row(3 * d),
        pl.BlockSpec((1, 4, d), mod_map),
        _layer_spec(woa, li), _layer_spec(wob, li), _layer_spec(woc, li), _layer_spec(wout, li),
        _resident((1, d)),
        _layer_spec(w1, li), _layer_spec(w3, li), _layer_spec(w2, li),
    ]
    if final:
        args.append(g_final.reshape(1, d))
        specs.append(_resident((1, d)))
    return pl.pallas_call(
        functools.partial(_mix_ffn_body, final=final),
        grid=(b, l // tm),
        in_specs=specs,
        out_specs=row(d),
        out_shape=jax.ShapeDtypeStruct((b, l, d), F32),
        compiler_params=_params(("parallel", "parallel")),
    )(*args)


def _head_block(nope, rope):
    ref = nope if nope is not None else rope
    z = lambda n: jnp.zeros(ref.shape[:-1] + (n,), ref.dtype)
    half = MLA_ROPE // 2
    lo = MLA_PAIR_DIST - half
    r1, r2 = (z(half), z(half)) if rope is None else (rope[..., :half], rope[..., half:])
    n1, n2 = (z(lo), z(MLA_NOPE - lo)) if nope is None else (nope[..., :lo], nope[..., lo:])
    used = 2 * half + MLA_NOPE
    return jnp.concatenate([r1, n1, r2, n2, z(LANES - used)], axis=-1)


def _arrange_w_in(w_in):
    k0 = LRU_W + KV_LORA
    cols = [w_in[..., :k0], _head_block(None, w_in[..., k0:k0 + MLA_ROPE]), w_in[..., k0 + MLA_ROPE:]]
    return jnp.concatenate(cols, axis=-1).astype(BF16)


def _arrange_mla(w_uq, w_ukv):
    nh = MLA_HEADS
    wq = w_uq.reshape(Q_LORA, nh, MLA_NOPE + MLA_ROPE)
    wq_arr = _head_block(wq[..., :MLA_NOPE], wq[..., MLA_NOPE:]).reshape(Q_LORA, nh * LANES)
    wkv = w_ukv.reshape(KV_LORA, nh, MLA_NOPE + MLA_V)
    wk_arr = _head_block(wkv[..., :MLA_NOPE], None).reshape(KV_LORA, nh * LANES)
    wv = wkv[..., MLA_NOPE:]
    zv = jnp.zeros_like(wv)
    even = (jnp.arange(nh) % 2 == 0)[None, :, None]
    wv_arr = jnp.concatenate([jnp.where(even, wv, zv), jnp.where(even, zv, wv)], axis=-1)
    wv_arr = wv_arr.reshape(KV_LORA, nh * LANES)
    return wq_arr.astype(BF16), jnp.concatenate([wk_arr, wv_arr], axis=1).astype(BF16)


def _value_bounds(w_ukv, g_kv):
    nh = MLA_HEADS
    wv = w_ukv.reshape(KV_LORA, nh, MLA_NOPE + MLA_V)[..., MLA_NOPE:]
    bound = jnp.sqrt(jnp.sum(wv * wv, axis=0)) * (KV_LORA ** 0.5 * jnp.max(jnp.abs(g_kv)))
    bound = jnp.where(bound > 0.0, bound, 1.0)
    one = jnp.ones_like(bound)
    even = (jnp.arange(nh) % 2 == 0)[:, None]
    full = jnp.concatenate([jnp.where(even, bound, one), jnp.where(even, one, bound)], axis=-1)
    return full.reshape(nh, 1, LANES)


def _arrange_lru(wa, ba, wx, bx):
    def dense(wblk):
        eye = jnp.eye(LRU_BLOCKS, dtype=wblk.dtype)
        full = jnp.einsum("nde,nm->ndme", wblk, eye)
        return full.reshape(LRU_W, LRU_W)

    wg = jnp.stack([jnp.concatenate([dense(wa[d]), dense(wx[d])], axis=1) for d in range(2)])
    bg = jnp.stack([jnp.concatenate([ba[d], bx[d]])[None, :] for d in range(2)])
    return wg.astype(BF16), bg


def _mla_tables(n_tokens, rotate):
    scale = (MLA_NOPE + MLA_ROPE) ** -0.5 * math.log2(math.e)
    ones = jnp.ones((n_tokens, MLA_NOPE), F32)
    if rotate:
        rows = n_tokens // GRID_W
        rowi = jnp.repeat(jnp.arange(rows, dtype=F32), GRID_W)
        coli = jnp.tile(jnp.arange(GRID_W, dtype=F32), rows)
        n_freq = MLA_ROPE // 4
        inv = jnp.power(ROPE_BASE, -jnp.arange(n_freq, dtype=F32) / n_freq)
        ang = jnp.concatenate([rowi[:, None] * inv, coli[:, None] * inv], axis=-1)
        cos, sin = jnp.cos(ang), jnp.sin(ang)
    else:
        cos = jnp.ones((n_tokens, MLA_ROPE // 2), F32)
        sin = jnp.zeros((n_tokens, MLA_ROPE // 2), F32)
    cos2 = jnp.concatenate([cos, cos], axis=1)
    sin2 = jnp.concatenate([-sin, sin], axis=1)
    cosq = _head_block(ones, cos2) * scale
    sinq = _head_block(None, sin2) * scale
    cosk = _head_block(None, cos2)
    sink = _head_block(None, sin2)
    return cosq, sinq, cosk, sink


def _ret_tables(n):
    theta = 1.0 / jnp.power(10000.0, jnp.linspace(0.0, 1.0, RET_DK // 2, dtype=F32))
    pos = jnp.arange(n, dtype=F32)
    ang = pos[:, None] * theta
    cos, sin = jnp.cos(ang), jnp.sin(ang)
    reps = LANES // RET_DK
    cos_t = jnp.tile(jnp.concatenate([cos, cos], axis=1), (1, reps))
    sin_t = jnp.tile(jnp.concatenate([-sin, sin], axis=1), (1, reps))
    return cos_t, sin_t


def _ret_decay_tables():
    c = RET_CHUNK
    h = jnp.arange(RET_HEADS, dtype=F32)
    lgs = jnp.stack([jnp.log1p(-jnp.exp2(-5.0 - h)), jnp.log1p(-jnp.exp2(-5.5 - h))], axis=1)
    pos = jnp.arange(c, dtype=F32)
    diff = pos[:, None] - pos[None, :]
    lg = lgs[:, :, None, None]
    inner = jnp.where(diff >= 0, jnp.exp(lg * jnp.maximum(diff, 0.0)), 0.0)
    msum = inner[:, 0] + jnp.swapaxes(inner[:, 1], -1, -2)
    lg1 = lgs[:, :, None]
    qd_f = jnp.exp(lg1 * (pos + 1.0))
    kd_f = jnp.exp(lg1 * (c - 1.0 - pos))
    qd = jnp.stack([qd_f[:, 0], qd_f[:, 1, ::-1]], axis=1)
    kd = jnp.stack([kd_f[:, 0], kd_f[:, 1, ::-1]], axis=1)
    lane_head = (jnp.arange(LANES) // RET_DK)
    e_of_h = jnp.arange(RET_HEADS) % 2
    mask = (lane_head[None, :] == e_of_h[:, None]).astype(F32)
    qdec = qd[..., None] * mask[:, None, None, :]
    kdec = kd[..., None] * mask[:, None, None, :]
    cd = jnp.broadcast_to(jnp.exp(lgs * c)[:, :, None, None], (RET_HEADS, 2, 1, RET_DV))
    pair = lambda t: t.reshape(RET_HEADS // 2, 2, *t.shape[1:])
    return pair(msum), pair(qdec), pair(kdec), pair(cd)


def _row_tile(n, pref):
    t = min(pref, n)
    while n % t:
        t //= 2
    return t


def kernel(x, c, ctx, c_ctx, w_mod, b_mod, g_mix, g_ffn, w_in, conv_w, conv_b, lru_wa, lru_ba,
           lru_wx, lru_bx, lru_lam, g_q, w_uq, g_kv, w_ukv, w_oa, w_ob, w_oc, w_out,
           w_ff1, w_ff3, w_ff2, g_final):
    b, l, d = x.shape
    lc = ctx.shape[1]
    depth = w_mod.shape[0]
    assert d == D_MODEL and l % LRU_TC == 0 and lc % LRU_TC == 0 and l % GRID_W == 0

    tm = _row_tile(l, 512)
    tmc = _row_tile(lc, 256)
    tq = _row_tile(l, 256)
    tqc = _row_tile(lc, 256)

    lat_tabs = _mla_tables(l, True)
    ctx_tabs = _mla_tables(lc, False)
    ret_tabs = _ret_tables(lc + l)
    dec = _ret_decay_tables()

    rows = -(-(b + 1) // SUBLANES) * SUBLANES
    cvec = jnp.zeros((rows, d), F32).at[:b].set(c).at[b].set(c_ctx)
    mod_all = _modulation(cvec, w_mod, b_mod)

    w_arr = _arrange_w_in(w_in)
    woa, wob, woc, wout = (w.astype(BF16) for w in (w_oa, w_ob, w_oc, w_out))
    w1, w3, w2 = (w.astype(BF16) for w in (w_ff1, w_ff3, w_ff2))

    xc = ctx
    for li in range(depth):
        last = li == depth - 1
        mod = mod_all[li, :b].reshape(b, 1, 6, d)
        sh_a, sc_a = mod[:, :, 0], mod[:, :, 1]
        mc = mod_all[li, b].reshape(1, 1, 6, d)
        mcs = [mc[:, :, j] for j in range(6)]

        wq_arr, wkv_arr = _arrange_mla(w_uq[li], w_ukv[li])
        wg, bg = _arrange_lru(lru_wa[li], lru_ba[li], lru_wx[li], lru_bx[li])

        z = _inproj(x, g_mix[li], sh_a, sc_a, w_arr, li, KV_GROUPS + Q_GROUPS, tm)
        u, ckv, ra, rk, rv, cq, rq, g_lru, g_ret, gm = z
        if last:
            zc = _inproj(xc, g_mix[li], mcs[0], mcs[1], w_arr, li, KV_GROUPS, tmc)
            uc, ckvc, rac, rkc, rvc = zc
            cqc = rqc = None
        else:
            zc = _inproj(xc, g_mix[li], mcs[0], mcs[1], w_arr, li, KV_GROUPS + Q_GROUPS, tmc)
            uc, ckvc, rac, rkc, rvc, cqc, rqc, g_lru_c, g_ret_c, gm_c = zc
        need_ctx = not last

        rec, rec_c = _lru(u, uc, conv_w[li], conv_b[li], wg, bg, lru_lam[li], need_ctx)

        vbound = _value_bounds(w_ukv[li], g_kv[li])
        vinv = 1.0 / vbound
        q, k, v = _mla_prep(cq, ckv, ra, *lat_tabs, g_q[li], wq_arr, g_kv[li], wkv_arr, vinv,
                            True, True, _row_tile(l, 1024))
        qc, kc, vc = _mla_prep(cqc, ckvc, rac, *ctx_tabs, g_q[li], wq_arr, g_kv[li], wkv_arr, vinv,
                               False, need_ctx, tmc)
        yb = _attention(q, [(kc, vc), (k, v)], vbound, tq)

        rn, rn_c = _retention(rq, rk, rv, rqc, rkc, rvc, ret_tabs, dec, need_ctx)

        x = _mix_ffn(x, rec, g_lru, yb, rn, g_ret, gm, mod[:, 0, 2:], woa, wob, woc, wout,
                     g_ffn[li], w1, w3, w2, li, g_final if last else None, tm)
        if need_ctx:
            yb_c = _attention(qc, [(kc, vc)], vbound, tqc)
            xc = _mix_ffn(xc, rec_c, g_lru_c, yb_c, rn_c, g_ret_c, gm_c, mc[:, 0, 2:], woa, wob, woc, wout,
                          g_ffn[li], w1, w3, w2, li, None, tmc)
    return x
```
